```python
import jax, jax.numpy as jnp
from jax import lax
import numpy as np

D_MODEL = 1024
BATCH = 8
SEQ = 2048
DEPTH = 2
DEC_BATCH = 128
DEC_SEQ = 4
PAST_LEN = 16384
PAGE_SIZE = 128

CHUNK = 128
D_A = D_MODEL
GROUP_A = 128
H_A = D_A // GROUP_A
D_B = D_MODEL
HEAD_B = 64
H_B = D_B // HEAD_B
LORA_W = 64
LORA_A = 64
LORA_G = 128
LORA_V = 32
N_SHIFT = 3 * D_B + LORA_W + LORA_A + LORA_G
N_IN = 2 * D_A + N_SHIFT + 2 * D_MODEL
D_FF = 2816
N_EXPERTS = 8
TOP_K = 2
D_FF_E = 3584
MOE_BLOCK = 128
N_DENSE = (DEPTH + 1) // 2
N_MOE = DEPTH // 2
RMS_EPS = 1e-6
LN_EPS = 1e-5
GN_EPS = 64e-5

kernel_name = "gmlp_rwkv7_gated_hybrid_step"


def rms_norm(x, g):
    xf = x.astype(jnp.float32)
    y = xf * lax.rsqrt(jnp.mean(xf * xf, axis=-1, keepdims=True) + RMS_EPS)
    return (y * g.astype(jnp.float32)).astype(x.dtype)


def layer_norm(x, g, b):
    xf = x.astype(jnp.float32)
    mu = jnp.mean(xf, axis=-1, keepdims=True)
    var = jnp.mean(jnp.square(xf - mu), axis=-1, keepdims=True)
    y = (xf - mu) * lax.rsqrt(var + LN_EPS) * g.astype(jnp.float32) + b.astype(jnp.float32)
    return y.astype(x.dtype)


def chunk_spatial_gate(u, v, w_s, b_s):
    B, T, _ = v.shape
    n_chunks = -(-T // CHUNK)
    pad = n_chunks * CHUNK - T
    vp = jnp.pad(v, ((0, 0), (0, pad), (0, 0))).reshape(B, n_chunks, CHUNK, H_A, GROUP_A)
    mask = jnp.tril(jnp.ones((CHUNK, CHUNK), dtype=bool))
    w = jnp.where(mask[None], w_s, jnp.zeros((), w_s.dtype))
    mixed = jnp.einsum('hts,bnshc->bnthc', w, vp) + b_s.T[None, None, :, :, None]
    mixed = mixed.reshape(B, n_chunks * CHUNK, D_A)[:, :T]
    return u * mixed


def wkv_scan(r, decay, k, v, a_vec, b_vec, s0):
    def step(S, inp):
        r_t, w_t, k_t, v_t, a_t, b_t = inp
        Sa = jnp.einsum('bhvk,bhk->bhv', S, a_t)
        S = S * w_t[:, :, None, :] + Sa[..., None] * b_t[:, :, None, :] + v_t[..., None] * k_t[:, :, None, :]
        y = jnp.einsum('bhvk,bhk->bhv', S, r_t)
        return S, y
    xs = tuple(jnp.moveaxis(t, 1, 0) for t in (r, decay, k, v, a_vec, b_vec))
    S, ys = lax.scan(step, s0, xs)
    return jnp.moveaxis(ys, 0, 1), S


def gated_mixer(h, shift_prev, wkv_prev, v_first, layer, p):
    B, T, _ = h.shape
    f32 = jnp.float32
    proj = h @ p["w_in"]
    pu, pv, ps, pgate = jnp.split(proj, [D_A, 2 * D_A, 2 * D_A + N_SHIFT], axis=-1)
    u = jax.nn.gelu(pu, approximate=False)
    va = layer_norm(jax.nn.gelu(pv, approximate=False), p["sgu_ln_g"], p["sgu_ln_b"])
    ya = chunk_spatial_gate(u, va, p["sgu_w"], p["sgu_b"])
    prev_rows = jnp.concatenate([shift_prev[:, None, :].astype(ps.dtype), ps[:, :-1]], axis=1)
    xs = ps + p["shift_mu"] * (prev_rows - ps)
    r, k, v, dw, da, dg = jnp.split(
        xs, [D_B, 2 * D_B, 3 * D_B, 3 * D_B + LORA_W, 3 * D_B + LORA_W + LORA_A], axis=-1)
    r = r.astype(f32)
    k = k.astype(f32)
    v = v.astype(f32)
    if layer == 0:
        v_first = v
    else:
        vgate = jax.nn.sigmoid((p["v0"] + (h @ p["v1"]) @ p["v2"]).astype(f32))
        v = v + (v_first - v) * vgate
    w_log = -jax.nn.softplus(-(p["w0"] + jnp.tanh(dw) @ p["w2"]).astype(f32)) - 0.5
    decay = jnp.exp(-jnp.exp(w_log))
    a = jax.nn.sigmoid((p["a0"] + da @ p["a2"]).astype(f32))
    g = jax.nn.sigmoid(dg) @ p["g2"]

    def heads(t):
        return t.reshape(B, T, H_B, HEAD_B)

    kk = heads(k * p["k_k"])
    kk = kk * lax.rsqrt(jnp.sum(kk * kk, axis=-1, keepdims=True) + 1e-12)
    k = k * (1.0 + (a - 1.0) * p["k_a"])
    rh, kh, vh, ah = heads(r), heads(k), heads(v), heads(a)
    yh, wkv_last = wkv_scan(rh, heads(decay), kh, vh, -kk, kk * ah, wkv_prev.astype(f32))
    mu = jnp.mean(yh, axis=-1, keepdims=True)
    var = jnp.mean(jnp.square(yh - mu), axis=-1, keepdims=True)
    yh = (yh - mu) * lax.rsqrt(var + GN_EPS)
    y = yh.reshape(B, T, D_B) * p["ln_g"] + p["ln_b"]
    y = y + (jnp.sum(rh * kh * p["r_k"], axis=-1, keepdims=True) * vh).reshape(B, T, D_B)
    yb = (y * g).astype(h.dtype)
    gate_a, gate_b = jnp.split(jax.nn.sigmoid(pgate), 2, axis=-1)
    merged = gate_a * (ya @ p["w_branch_a"]) + gate_b * (yb @ p["w_branch_b"])
    out = merged @ p["w_out"]
    return out, v_first, ps[:, -1], wkv_last.astype(wkv_prev.dtype), va


def swiglu(x, w1, w3, w2):
    return (jax.nn.silu(x @ w1) * (x @ w3)) @ w2


def moe_swiglu(x, w_router, w1, w3, w2):
    B, T, D = x.shape
    x2 = x.reshape(B * T, D)
    n_tok = B * T
    logits = (x2 @ w_router).astype(jnp.float32)
    top_vals, top_idx = lax.top_k(logits, TOP_K)
    gates = jax.nn.softmax(top_vals, axis=-1).astype(x.dtype)
    flat_e = top_idx.reshape(-1)
    flat_tok = jnp.repeat(jnp.arange(n_tok, dtype=jnp.int32), TOP_K)
    flat_gate = gates.reshape(-1)
    order = jnp.argsort(flat_e)
    sorted_e = flat_e[order]
    counts = jnp.bincount(flat_e, length=N_EXPERTS)
    starts = jnp.cumsum(counts) - counts
    padded = ((counts + MOE_BLOCK - 1) // MOE_BLOCK) * MOE_BLOCK
    pends = jnp.cumsum(padded)
    pstarts = pends - padded
    rank = jnp.arange(n_tok * TOP_K) - starts[sorted_e]
    dest = pstarts[sorted_e] + rank
    n_blocks = (n_tok * TOP_K + MOE_BLOCK - 1) // MOE_BLOCK + N_EXPERTS
    n_rows = n_blocks * MOE_BLOCK
    row_tok = jnp.zeros((n_rows,), jnp.int32).at[dest].set(flat_tok[order])
    row_gate = jnp.zeros((n_rows,), x.dtype).at[dest].set(flat_gate[order])
    block_e = jnp.searchsorted(pends, jnp.arange(n_blocks) * MOE_BLOCK, side='right')
    block_e = jnp.minimum(block_e, N_EXPERTS - 1)
    xb = x2[row_tok].reshape(n_blocks, MOE_BLOCK, D)

    def expert_block(args):
        xblk, e = args
        return swiglu(xblk, w1[e], w3[e], w2[e])

    yb = lax.map(expert_block, (xb, block_e))
    y = jnp.zeros_like(x2).at[row_tok].add(yb.reshape(n_rows, D) * row_gate[:, None])
    return y.reshape(B, T, D)


def setup_inputs(seed: int = 0) -> dict:
    key = jax.random.key(seed)
    it = iter(jax.random.split(key, 48))
    f32 = jnp.float32

    def nrm(shape, scale):
        return jax.random.normal(next(it), shape, f32) * scale

    def unif(shape, lo, hi):
        return jax.random.uniform(next(it), shape, f32, lo, hi)

    d = D_MODEL
    return {
        "x_prompt": nrm((BATCH, SEQ, d), 1.0),
        "x_sample": nrm((DEC_BATCH, DEC_SEQ, d), 1.0),
        "state_wkv": nrm((DEPTH, DEC_BATCH, H_B, HEAD_B, HEAD_B), 0.3),
        "state_shift": nrm((DEPTH, DEC_BATCH, N_SHIFT), 1.0),
        "norm_mix_g": 1.0 + nrm((DEPTH, d), 0.02),
        "w_in": nrm((DEPTH, d, N_IN), d ** -0.5),
        "shift_mu": unif((DEPTH, N_SHIFT), 0.0, 1.0),
        "sgu_ln_g": 1.0 + nrm((DEPTH, D_A), 0.02),
        "sgu_ln_b": nrm((DEPTH, D_A), 0.02),
        "sgu_w": nrm((DEPTH, H_A, CHUNK, CHUNK), CHUNK ** -0.5),
        "sgu_b": 1.0 + nrm((DEPTH, H_A, CHUNK), 0.1),
        "rwkv_w0": unif((DEPTH, D_B), -3.0, 1.0),
        "rwkv_w2": nrm((DEPTH, LORA_W, D_B), 0.5 * LORA_W ** -0.5),
        "rwkv_a0": nrm((DEPTH, D_B), 0.5),
        "rwkv_a2": nrm((DEPTH, LORA_A, D_B), 0.5 * LORA_A ** -0.5),
        "rwkv_g2": nrm((DEPTH, LORA_G, D_B), LORA_G ** -0.5),
        "rwkv_v0": nrm((DEPTH - 1, D_B), 0.5),
        "rwkv_v1": nrm((DEPTH - 1, d, LORA_V), d ** -0.5),
        "rwkv_v2": nrm((DEPTH - 1, LORA_V, D_B), LORA_V ** -0.5),
        "rwkv_k_k": 0.85 + nrm((DEPTH, D_B), 0.05),
        "rwkv_k_a": 1.0 + nrm((DEPTH, D_B), 0.05),
        "rwkv_r_k": nrm((DEPTH, H_B, HEAD_B), 0.1),
        "rwkv_ln_g": 1.0 + nrm((DEPTH, D_B), 0.02),
        "rwkv_ln_b": nrm((DEPTH, D_B), 0.02),
        "w_branch_a": nrm((DEPTH, D_A, d), D_A ** -0.5),
        "w_branch_b": nrm((DEPTH, D_B, d), D_B ** -0.5),
        "w_out": nrm((DEPTH, d, d), d ** -0.5),
        "norm_ffn_g": 1.0 + nrm((DEPTH, d), 0.02),
        "ffn_w1": nrm((N_DENSE, d, D_FF), d ** -0.5),
        "ffn_w3": nrm((N_DENSE, d, D_FF), d ** -0.5),
        "ffn_w2": nrm((N_DENSE, D_FF, d), D_FF ** -0.5),
        "moe_router": nrm((N_MOE, d, N_EXPERTS), d ** -0.5),
        "moe_w1": nrm((N_MOE, N_EXPERTS, d, D_FF_E), d ** -0.5),
        "moe_w3": nrm((N_MOE, N_EXPERTS, d, D_FF_E), d ** -0.5),
        "moe_w2": nrm((N_MOE, N_EXPERTS, D_FF_E, d), D_FF_E ** -0.5),
        "norm_final_g": 1.0 + nrm((d,), 0.02),
    }


def reference(x_prompt, x_sample, state_wkv, state_shift, norm_mix_g, w_in, shift_mu,
              sgu_ln_g, sgu_ln_b, sgu_w, sgu_b, rwkv_w0, rwkv_w2, rwkv_a0, rwkv_a2, rwkv_g2,
              rwkv_v0, rwkv_v1, rwkv_v2, rwkv_k_k, rwkv_k_a, rwkv_r_k, rwkv_ln_g, rwkv_ln_b,
              w_branch_a, w_branch_b, w_out, norm_ffn_g, ffn_w1, ffn_w3, ffn_w2,
              moe_router, moe_w1, moe_w3, moe_w2, norm_final_g):
    xp, xs = x_prompt, x_sample
    bp = x_prompt.shape[0]
    shift0_p = jnp.zeros((bp, N_SHIFT), x_prompt.dtype)
    wkv0_p = jnp.zeros((bp, H_B, HEAD_B, HEAD_B), state_wkv.dtype)
    vf_p = None
    vf_s = None
    wkv_p_list, shift_p_list, wkv_s_list, shift_s_list, chunk_v_list = [], [], [], [], []
    for l in range(DEPTH):
        p = {
            "w_in": w_in[l], "shift_mu": shift_mu[l],
            "sgu_ln_g": sgu_ln_g[l], "sgu_ln_b": sgu_ln_b[l], "sgu_w": sgu_w[l], "sgu_b": sgu_b[l],
            "w0": rwkv_w0[l], "w2": rwkv_w2[l], "a0": rwkv_a0[l], "a2": rwkv_a2[l],
            "g2": rwkv_g2[l], "k_k": rwkv_k_k[l], "k_a": rwkv_k_a[l], "r_k": rwkv_r_k[l],
            "ln_g": rwkv_ln_g[l], "ln_b": rwkv_ln_b[l],
            "w_branch_a": w_branch_a[l], "w_branch_b": w_branch_b[l], "w_out": w_out[l],
        }
        if l > 0:
            p["v0"] = rwkv_v0[l - 1]
            p["v1"] = rwkv_v1[l - 1]
            p["v2"] = rwkv_v2[l - 1]
        hp = rms_norm(xp, norm_mix_g[l])
        hs = rms_norm(xs, norm_mix_g[l])
        mp, vf_p, sh_p, wkv_p, _ = gated_mixer(hp, shift0_p, wkv0_p, vf_p, l, p)
        ms, vf_s, sh_s, wkv_s, va_s = gated_mixer(hs, state_shift[l], state_wkv[l], vf_s, l, p)
        xp = xp + mp
        xs = xs + ms
        wkv_p_list.append(wkv_p)
        shift_p_list.append(sh_p)
        wkv_s_list.append(wkv_s)
        shift_s_list.append(sh_s)
        chunk_v_list.append(va_s)
        hp = rms_norm(xp, norm_ffn_g[l])
        hs = rms_norm(xs, norm_ffn_g[l])
        if l % 2 == 0:
            j = l // 2
            xp = xp + swiglu(hp, ffn_w1[j], ffn_w3[j], ffn_w2[j])
            xs = xs + swiglu(hs, ffn_w1[j], ffn_w3[j], ffn_w2[j])
        else:
            j = l // 2
            xp = xp + moe_swiglu(hp, moe_router[j], moe_w1[j], moe_w3[j], moe_w2[j])
            xs = xs + moe_swiglu(hs, moe_router[j], moe_w1[j], moe_w3[j], moe_w2[j])
    y_prompt = rms_norm(xp, norm_final_g)
    y_sample = rms_norm(xs, norm_final_g)
    wkv_prompt = jnp.stack(wkv_p_list)
    shift_prompt = jnp.stack(shift_p_list)
    wkv_sample = jnp.stack(wkv_s_list)
    shift_sample = jnp.stack(shift_s_list)
    chunk_v_sample = jnp.stack(chunk_v_list)
    return (y_prompt, y_sample, wkv_prompt, shift_prompt, wkv_sample, shift_sample, chunk_v_sample)
```

```python
import functools

import jax
import jax.numpy as jnp
from jax import lax
from jax.experimental import pallas as pl
from jax.experimental.pallas import tpu as pltpu

f32 = jnp.float32
bf16 = jnp.bfloat16

D_MODEL = 1024
CHUNK = 128
D_A = D_MODEL
GROUP_A = 128
H_A = D_A // GROUP_A
D_B = D_MODEL
HEAD_B = 64
H_B = D_B // HEAD_B
LORA_W = 64
LORA_A = 64
LORA_G = 128
N_SHIFT = 3 * D_B + LORA_W + LORA_A + LORA_G
N_IN = 2 * D_A + N_SHIFT + 2 * D_MODEL
N_EXPERTS = 8
RMS_EPS = 1e-6
LN_EPS = 1e-5
GN_EPS = 64e-5

PAIR = 2 * HEAD_B
N_PAIR = H_B // 2
MIB = 1024 * 1024
VMEM_CAP_V7X = 56 * MIB


def _params(vmem_bytes, n_grid):
    return pltpu.CompilerParams(
        dimension_semantics=("arbitrary",) * n_grid,
        vmem_limit_bytes=int(min(max(vmem_bytes, 16 * MIB), VMEM_CAP_V7X)),
    )


def _const_spec(shape):
    nd = len(shape)
    return pl.BlockSpec(shape, lambda *_: (0,) * nd, pipeline_mode=pl.Buffered(1))


def _gelu(x):
    return 0.5 * x * (1.0 + lax.erf(x * (2.0 ** -0.5)))


def _sigmoid(x):
    return 1.0 / (1.0 + jnp.exp(-x))


def _dot(a, b):
    return jnp.dot(a, b, preferred_element_type=f32)


def _dot_nt(a, b):
    return lax.dot_general(a, b, (((1,), (1,)), ((), ())), preferred_element_type=f32)


def _dot_tn(a, b):
    return lax.dot_general(a, b, (((0,), (0,)), ((), ())), preferred_element_type=f32)


def _head_sum(x, bd_ref):
    w = bd_ref.shape[0]
    cols = [_dot(x[:, c:c + w].astype(bf16), bd_ref[...]) for c in range(0, x.shape[1], w)]
    return jnp.concatenate(cols, axis=1)


IN_TM = 256
IN_CW = 512


def _in_proj_body(has_v, x_ref, g_ref, w_ref, lng_ref, lnb_ref, *rest):
    if has_v:
        v1_ref, v2_ref, v0_ref, u_ref, va_ref, ps_ref, gate_ref, vg_ref, h_scr = rest
    else:
        u_ref, va_ref, ps_ref, gate_ref, h_scr = rest
    x = x_ref[...]
    h = x * lax.rsqrt(jnp.mean(x * x, axis=-1, keepdims=True) + RMS_EPS) * g_ref[...]
    h_scr[...] = h.astype(bf16)

    def mm(c0, c1):
        return _dot(h_scr[...], w_ref[:, c0:c1])

    for c in range(0, D_A, IN_CW):
        u_ref[:, c:c + IN_CW] = _gelu(mm(c, c + IN_CW))
    for c in range(0, D_A, IN_CW):
        va_ref[:, c:c + IN_CW] = _gelu(mm(D_A + c, D_A + c + IN_CW))
    t = va_ref[...]
    mu = jnp.mean(t, axis=-1, keepdims=True)
    d = t - mu
    var = jnp.mean(d * d, axis=-1, keepdims=True)
    va_ref[...] = d * lax.rsqrt(var + LN_EPS) * lng_ref[...] + lnb_ref[...]
    for c in range(0, N_SHIFT, IN_CW):
        c1 = min(c + IN_CW, N_SHIFT)
        ps_ref[:, c:c1] = mm(2 * D_A + c, 2 * D_A + c1)
    base = 2 * D_A + N_SHIFT
    for c in range(0, 2 * D_MODEL, IN_CW):
        gate_ref[:, c:c + IN_CW] = _sigmoid(mm(base + c, base + c + IN_CW))
    if has_v:
        lv = _dot(h_scr[...], v1_ref[...])
        vg_ref[...] = _sigmoid(v0_ref[...] + _dot(lv.astype(bf16), v2_ref[...]))


def _in_proj(x, g, w_in, lng, lnb, vparams):
    n = x.shape[0]
    tm = IN_TM
    has_v = vparams is not None
    row = lambda w: pl.BlockSpec((tm, w), lambda i: (i, 0))
    in_specs = [row(D_MODEL), _const_spec((1, D_MODEL)), _const_spec((D_MODEL, N_IN)),
                _const_spec((1, D_A)), _const_spec((1, D_A))]
    args = [x, g, w_in, lng, lnb]
    out_shape = [jax.ShapeDtypeStruct((n, D_A), f32), jax.ShapeDtypeStruct((n, D_A), f32),
                 jax.ShapeDtypeStruct((n, N_SHIFT), f32), jax.ShapeDtypeStruct((n, 2 * D_MODEL), f32)]
    out_specs = [row(D_A), row(D_A), row(N_SHIFT), row(2 * D_MODEL)]
    if has_v:
        v1, v2, v0 = vparams
        in_specs += [_const_spec(v1.shape), _const_spec(v2.shape), _const_spec((1, D_B))]
        args += [v1, v2, v0]
        out_shape.append(jax.ShapeDtypeStruct((n, D_B), f32))
        out_specs.append(row(D_B))
    out_w = 2 * D_A + N_SHIFT + 2 * D_MODEL + (D_B if has_v else 0)
    vmem = 2 * D_MODEL * N_IN + 2 * tm * 4 * (D_MODEL + out_w) + 6 * tm * D_MODEL * 4 + 4 * MIB
    return pl.pallas_call(
        functools.partial(_in_proj_body, has_v),
        grid=(n // tm,), in_specs=in_specs, out_specs=out_specs, out_shape=out_shape,
        scratch_shapes=[pltpu.VMEM((tm, D_MODEL), bf16)],
        compiler_params=_params(vmem, 1), name="in_proj",
    )(*args)


PREP_TM = 256
N_PREP_OUT = 8


def _prep_body(period, tiles_per_seq, has_v, ps_ref, first_ref, *rest):
    if period == 0:
        prevblk_ref, rest = rest[0], rest[1:]
    (mu_ref, wa2_ref, w0_ref, a0_ref, g2_ref, kk_ref, ka_ref, rk_ref, bd_ref), rest = rest[:9], rest[9:]
    if has_v:
        (vg_ref, vf_ref), rest = rest[:2], rest[2:]
    r_out, lw_out, k_out, v_out, na_out, b_out, g_out, rkv_out = rest
    ps = ps_ref[...]
    tm = ps.shape[0]
    row = lax.broadcasted_iota(jnp.int32, (tm, 1), 0)
    rolled = pltpu.roll(ps, 1, 0)
    if period == 0:
        is_first = (pl.program_id(0) % tiles_per_seq) == 0
        row0 = jnp.where(is_first, first_ref[pl.ds(pl.program_id(0) // tiles_per_seq, 1), :],
                         prevblk_ref[7:8, :])
        prev = jnp.where(row == 0, row0, rolled)
    else:
        prev = jnp.where(row % period == 0, first_ref[...], rolled)
    xs = ps + mu_ref[...] * (prev - ps)
    r = xs[:, 0:D_B]
    k = xs[:, D_B:2 * D_B]
    v = xs[:, 2 * D_B:3 * D_B]
    dwa = xs[:, 3 * D_B:3 * D_B + LORA_W + LORA_A]
    dg = xs[:, 3 * D_B + LORA_W + LORA_A:N_SHIFT]
    lane = lax.broadcasted_iota(jnp.int32, dwa.shape, 1)
    lhs = jnp.where(lane < LORA_W, jnp.tanh(dwa), dwa).astype(bf16)
    wa = _dot(lhs, wa2_ref[...])
    w_log = -jax.nn.softplus(-(w0_ref[...] + wa[:, :D_B])) - 0.5
    lw_out[...] = -jnp.exp(w_log)
    a = _sigmoid(a0_ref[...] + wa[:, D_B:])
    g_out[...] = _dot(_sigmoid(dg).astype(bf16), g2_ref[...])
    kkr = k * kk_ref[...]
    kk = kkr * lax.rsqrt(_head_sum(kkr * kkr, bd_ref) + 1e-12)
    k2 = k * (1.0 + (a - 1.0) * ka_ref[...])
    if has_v:
        v = v + (vf_ref[...] - v) * vg_ref[...]
    r_out[...] = r
    k_out[...] = k2
    v_out[...] = v
    na_out[...] = -kk
    b_out[...] = kk * a
    rkv_out[...] = _head_sum(r * k2 * rk_ref[...], bd_ref) * v


def _prep(ps, first, period, wts, vextra):
    n = ps.shape[0]
    tm = PREP_TM
    has_v = vextra is not None
    row = lambda w: pl.BlockSpec((tm, w), lambda i: (i, 0))
    in_specs = [row(N_SHIFT)]
    args = [ps]
    if period == 0:
        in_specs += [_const_spec(first.shape),
                     pl.BlockSpec((8, N_SHIFT), lambda i: (jnp.maximum(i * (tm // 8) - 1, 0), 0))]
        args += [first, ps]
    else:
        in_specs += [row(N_SHIFT)]
        args += [first]
    in_specs += [_const_spec(w.shape) for w in wts]
    args += list(wts)
    if has_v:
        in_specs += [row(D_B), row(D_B)]
        args += list(vextra)
    vmem = 2 * tm * 4 * (2 * N_SHIFT + (N_PREP_OUT + 2) * D_B) + 16 * tm * D_B * 4 + 4 * MIB
    return pl.pallas_call(
        functools.partial(_prep_body, period, n // tm // first.shape[0] if period == 0 else 0, has_v),
        grid=(n // tm,), in_specs=in_specs,
        out_specs=[row(D_B)] * N_PREP_OUT,
        out_shape=[jax.ShapeDtypeStruct((n, D_B), f32)] * N_PREP_OUT,
        compiler_params=_params(vmem, 1), name="rwkv_prep",
    )(*args)


def _scan_body(n_seq, clen, n_real, r_ref, lw_ref, k_ref, v_ref, na_ref, b_ref, s0_ref, tri_ref,
               y_ref, sl_ref, s_scr):
    c = pl.program_id(1)

    @pl.when(c == 0)
    def _():
        s_scr[...] = s0_ref[...]

    L = clen
    row = lax.broadcasted_iota(jnp.int32, (L, 2 * L), 0)
    col = lax.broadcasted_iota(jnp.int32, (L, 2 * L), 1)
    mask_x = (col >= L) & (col - L < row)
    mask_y = jnp.where(col >= L, col - L, col) <= row
    mask_m = col < row
    lane_e = lax.broadcasted_iota(jnp.int32, (1, PAIR), 1) < HEAD_B
    r128 = lax.broadcasted_iota(jnp.int32, (PAIR, PAIR), 0) < HEAD_B
    c128 = lax.broadcasted_iota(jnp.int32, (PAIR, PAIR), 1) < HEAD_B
    diag_blocks = r128 == c128

    def one_seq(q, carry):
        lw = lw_ref[q]
        cum = jnp.dot(tri_ref[...], lw, precision=lax.Precision.HIGHEST, preferred_element_type=f32)
        tail = cum[L - 1:L, :]
        g_inv = jnp.exp(-cum)
        g_tail = jnp.exp(tail - cum)
        at = na_ref[q] * jnp.exp(cum - lw)
        rt = r_ref[q] * jnp.exp(cum)
        bt = b_ref[q] * g_inv
        kt = k_ref[q] * g_inv
        bh = b_ref[q] * g_tail
        kh = k_ref[q] * g_tail
        g_last = jnp.exp(tail)
        vv = v_ref[q]
        for p in range(N_PAIR):
            sl = slice(p * PAIR, (p + 1) * PAIR)
            a2 = jnp.concatenate([at[:, sl], rt[:, sl]], axis=0)
            b2 = jnp.concatenate([bt[:, sl], kt[:, sl]], axis=0).astype(bf16)
            a2b = a2.astype(bf16)
            pe = _dot_nt(jnp.where(lane_e, a2, 0.0).astype(bf16), b2)
            po = _dot_nt(jnp.where(lane_e, 0.0, a2).astype(bf16), b2)
            sp = s_scr[q, p]
            qq = _dot_nt(a2b, sp.astype(bf16))
            vp = vv[:, sl]
            vpb = vp.astype(bf16)
            v2 = jnp.concatenate([vpb, vpb], axis=0)
            x = qq[:L] + jnp.where(
                lane_e,
                _dot(jnp.where(mask_x, pe[:L], 0.0).astype(bf16), v2),
                _dot(jnp.where(mask_x, po[:L], 0.0).astype(bf16), v2))
            me = jnp.where(mask_m, pe[:L], 0.0).astype(bf16)
            mo = jnp.where(mask_m, po[:L], 0.0).astype(bf16)
            zpad = jnp.zeros((L, 2 * L), bf16)
            span = 1
            while span < n_real:
                xb = x.astype(bf16)
                x2 = jnp.concatenate([xb, xb], axis=0)
                x = x + jnp.where(lane_e, _dot(me, x2), _dot(mo, x2))
                span *= 2
                if span < n_real:
                    me = _dot(me, jnp.concatenate([me, zpad], axis=0)).astype(bf16)
                    mo = _dot(mo, jnp.concatenate([mo, zpad], axis=0)).astype(bf16)
            uv = jnp.concatenate([x.astype(bf16), vpb], axis=0)
            y = qq[L:] + jnp.where(
                lane_e,
                _dot(jnp.where(mask_y, pe[L:], 0.0).astype(bf16), uv),
                _dot(jnp.where(mask_y, po[L:], 0.0).astype(bf16), uv))
            y_ref[q, :, sl] = y
            bk = jnp.concatenate([bh[:, sl], kh[:, sl]], axis=0).astype(bf16)
            s_new = sp * g_last[:, sl] + _dot_tn(uv, bk)
            s_scr[q, p] = jnp.where(diag_blocks, s_new, 0.0)
        return carry

    lax.fori_loop(0, n_seq, one_seq, 0)

    @pl.when(c == pl.num_programs(1) - 1)
    def _():
        sl_ref[...] = s_scr[...]


def _scan(ops, s0, clen, n_real, seq_per_step):
    nseq, t, _ = ops[0].shape
    g = seq_per_step
    tok = pl.BlockSpec((g, clen, D_B), lambda i, c: (i, c, 0))
    st = pl.BlockSpec((g, N_PAIR, PAIR, PAIR), lambda i, c: (i, 0, 0, 0))
    tri = jnp.tril(jnp.ones((clen, clen), f32))
    vmem = 2 * 7 * g * clen * D_B * 4 + 5 * g * N_PAIR * PAIR * PAIR * 4 + 24 * clen * D_B * 4 + 4 * MIB
    return pl.pallas_call(
        functools.partial(_scan_body, g, clen, n_real),
        grid=(nseq // g, t // clen),
        in_specs=[tok] * 6 + [st, pl.BlockSpec((clen, clen), lambda i, c: (0, 0))],
        out_specs=[tok, st],
        out_shape=[jax.ShapeDtypeStruct((nseq, t, D_B), f32),
                   jax.ShapeDtypeStruct((nseq, N_PAIR, PAIR, PAIR), f32)],
        scratch_shapes=[pltpu.VMEM((g, N_PAIR, PAIR, PAIR), f32)],
        compiler_params=_params(vmem, 2), name="wkv_scan",
    )(*ops, s0, tri)


MIX_TM = 256


def _mix_body(u_ref, va_ref, y_ref, rkv_ref, g_ref, gate_ref, x_ref, sw_ref, sm_ref, sb_ref,
              lng_ref, lnb_ref, bd_ref, wa_ref, wb_ref, wo_ref, o_ref, ya_scr):
    tm = u_ref.shape[0]
    for r0 in range(0, tm, CHUNK):
        for grp in range(H_A):
            cs = slice(grp * GROUP_A, (grp + 1) * GROUP_A)
            w = jnp.where(sm_ref[...] > 0.5, sw_ref[grp], 0.0).astype(bf16)
            mixed = _dot(w, va_ref[r0:r0 + CHUNK, cs].astype(bf16)) + sb_ref[:, cs]
            ya_scr[r0:r0 + CHUNK, cs] = (u_ref[r0:r0 + CHUNK, cs] * mixed).astype(bf16)
    y = y_ref[...]
    mu = _head_sum(y, bd_ref) * (1.0 / HEAD_B)
    d = y - mu
    var = _head_sum(d * d, bd_ref) * (1.0 / HEAD_B)
    yn = d * lax.rsqrt(var + GN_EPS) * lng_ref[...] + lnb_ref[...] + rkv_ref[...]
    yb = (yn * g_ref[...]).astype(bf16)
    merged = (gate_ref[:, :D_MODEL] * _dot(ya_scr[...], wa_ref[...])
              + gate_ref[:, D_MODEL:] * _dot(yb, wb_ref[...]))
    o_ref[...] = x_ref[...] + _dot(merged.astype(bf16), wo_ref[...])


def _mix_out(u, va, y, rkv, g, gates, x, sgu_w2, sgu_m2, sgu_b2, n_long, lng, lnb, bd, wa, wb, wo):
    n = x.shape[0]
    tm = MIX_TM
    row = lambda w: pl.BlockSpec((tm, w), lambda i: (i, 0))
    kind = lambda i: jnp.where(i * tm >= n_long, 1, 0)
    in_specs = [row(D_A), row(D_A), row(D_B), row(D_B), row(D_B), row(2 * D_MODEL), row(D_MODEL),
                pl.BlockSpec((None, H_A, CHUNK, CHUNK), lambda i: (kind(i), 0, 0, 0)),
                pl.BlockSpec((None, CHUNK, CHUNK), lambda i: (kind(i), 0, 0)),
                pl.BlockSpec((None, CHUNK, D_A), lambda i: (kind(i), 0, 0)),
                _const_spec((1, D_B)), _const_spec((1, D_B)), _const_spec(bd.shape),
                _const_spec(wa.shape), _const_spec(wb.shape), _const_spec(wo.shape)]
    vmem = 2 * tm * 4 * 9 * D_MODEL + 3 * 2 * D_MODEL * D_MODEL + 10 * tm * D_MODEL * 4 + 6 * MIB
    return pl.pallas_call(
        _mix_body, grid=(n // tm,), in_specs=in_specs, out_specs=row(D_MODEL),
        out_shape=jax.ShapeDtypeStruct((n, D_MODEL), f32),
        scratch_shapes=[pltpu.VMEM((tm, D_A), bf16)],
        compiler_params=_params(vmem, 1), name="mix_out",
    )(u, va, y, rkv, g, gates, x, sgu_w2, sgu_m2, sgu_b2, lng, lnb, bd, wa, wb, wo)


FFN_TM = 512
FFN_CW = 256


def _ffn_body(x_ref, g_ref, w1_ref, w3_ref, w2_ref, o_ref, h_scr):
    x = x_ref[...]
    h = x * lax.rsqrt(jnp.mean(x * x, axis=-1, keepdims=True) + RMS_EPS) * g_ref[...]
    h_scr[...] = h.astype(bf16)
    o_ref[...] = x
    d_ff = w1_ref.shape[1]
    for c in range(0, d_ff, FFN_CW):
        a = _dot(h_scr[...], w1_ref[:, c:c + FFN_CW])
        b = _dot(h_scr[...], w3_ref[:, c:c + FFN_CW])
        t = (a * _sigmoid(a) * b).astype(bf16)
        o_ref[...] += _dot(t, w2_ref[c:c + FFN_CW, :])


def _ffn_dense(x, g, w1, w3, w2):
    n = x.shape[0]
    tm = FFN_TM
    row = pl.BlockSpec((tm, D_MODEL), lambda i: (i, 0))
    vmem = 3 * 2 * D_MODEL * w1.shape[1] + 4 * tm * D_MODEL * 4 + 8 * tm * D_MODEL * 4 + 4 * MIB
    return pl.pallas_call(
        _ffn_body, grid=(n // tm,),
        in_specs=[row, _const_spec((1, D_MODEL)), _const_spec(w1.shape), _const_spec(w3.shape),
                  _const_spec(w2.shape)],
        out_specs=row, out_shape=jax.ShapeDtypeStruct((n, D_MODEL), f32),
        scratch_shapes=[pltpu.VMEM((tm, D_MODEL), bf16)],
        compiler_params=_params(vmem, 1), name="ffn_dense",
    )(x, g, w1, w3, w2)


MOE_TM = 1536
MOE_BLOCK = 128
MOE_FC = 4


def _split3(x):
    hi = x.astype(bf16)
    r1 = x - hi.astype(f32)
    mid = r1.astype(bf16)
    lo = (r1 - mid.astype(f32)).astype(bf16)
    return hi, mid, lo


def _router_body(x_ref, g_ref, wr_ref, hb_ref, gate_t_ref, rank_t_ref, gate_c_ref, rank_c_ref, cnt_ref):
    x = x_ref[...]
    tm = x.shape[0]
    h = x * lax.rsqrt(jnp.mean(x * x, axis=-1, keepdims=True) + RMS_EPS) * g_ref[...]
    hb_ref[...] = h.astype(bf16)
    h_hi, h_mid, h_lo = _split3(h)
    w_hi, w_mid, w_lo = _split3(wr_ref[...])
    logits = (_dot_nt(w_hi, h_hi) + _dot_nt(w_hi, h_mid) + _dot_nt(w_mid, h_hi)
              + _dot_nt(w_hi, h_lo) + _dot_nt(w_mid, h_mid) + _dot_nt(w_lo, h_hi))
    eid = lax.broadcasted_iota(jnp.int32, logits.shape, 0)
    m1 = jnp.max(logits, axis=0, keepdims=True)
    i1 = jnp.min(jnp.where(logits == m1, eid, N_EXPERTS), axis=0, keepdims=True)
    sel1 = eid == i1
    rest = jnp.where(sel1, -jnp.inf, logits)
    m2 = jnp.max(rest, axis=0, keepdims=True)
    i2 = jnp.min(jnp.where(rest == m2, eid, N_EXPERTS), axis=0, keepdims=True)
    sel2 = eid == i2
    e2 = jnp.exp(m2 - m1)
    den = 1.0 + e2
    gate_t = jnp.where(sel1, 1.0 / den, 0.0) + jnp.where(sel2, e2 / den, 0.0)
    sel = jnp.where(sel1 | sel2, 1.0, 0.0)
    s_idx = lax.broadcasted_iota(jnp.int32, (tm, tm), 0)
    t_idx = lax.broadcasted_iota(jnp.int32, (tm, tm), 1)
    before = jnp.where(s_idx < t_idx, 1.0, 0.0).astype(bf16)
    rank = _dot(sel.astype(bf16), before)
    rank_t = jnp.where(sel > 0.5, rank, -1.0)
    gate_t_ref[...] = gate_t
    rank_t_ref[...] = rank_t
    gate_c_ref[...] = gate_t.T
    rank_c_ref[...] = rank_t.T
    cnt = jnp.sum(sel, axis=1, keepdims=True)
    cnt_ref[...] = jnp.broadcast_to(cnt, cnt_ref.shape).astype(jnp.int32)


def _router(x, g, wr_t):
    n = x.shape[0]
    tm = MOE_TM
    nt = n // tm
    vmem = 2 * tm * D_MODEL * 6 + 8 * tm * D_MODEL * 4 + 3 * tm * tm * 4 + 8 * MIB
    return pl.pallas_call(
        _router_body, grid=(nt,),
        in_specs=[pl.BlockSpec((tm, D_MODEL), lambda i: (i, 0)), _const_spec((1, D_MODEL)),
                  _const_spec(wr_t.shape)],
        out_specs=[pl.BlockSpec((tm, D_MODEL), lambda i: (i, 0)),
                   pl.BlockSpec((N_EXPERTS, tm), lambda i: (0, i)),
                   pl.BlockSpec((N_EXPERTS, tm), lambda i: (0, i)),
                   pl.BlockSpec((tm, N_EXPERTS), lambda i: (i, 0)),
                   pl.BlockSpec((tm, N_EXPERTS), lambda i: (i, 0)),
                   pl.BlockSpec((None, N_EXPERTS, 128), lambda i: (i, 0, 0))],
        out_shape=[jax.ShapeDtypeStruct((n, D_MODEL), bf16),
                   jax.ShapeDtypeStruct((N_EXPERTS, n), f32), jax.ShapeDtypeStruct((N_EXPERTS, n), f32),
                   jax.ShapeDtypeStruct((n, N_EXPERTS), f32), jax.ShapeDtypeStruct((n, N_EXPERTS), f32),
                   jax.ShapeDtypeStruct((nt, N_EXPERTS, 128), jnp.int32)],
        compiler_params=_params(vmem, 1), name="moe_router",
    )(x, g, wr_t)


def _moe_body(cnt_ref, hb_ref, rank_t_ref, gate_c_ref, rank_c_ref, w1_ref, w3_ref, w2_ref,
              o_ref, xg_scr, acc_scr):
    i, e, c = pl.program_id(0), pl.program_id(1), pl.program_id(2)
    tm = hb_ref.shape[0]
    n_blk = (cnt_ref[i * N_EXPERTS + e] + MOE_BLOCK - 1) // MOE_BLOCK

    @pl.when((e == 0) & (c == 0))
    def _():
        o_ref[...] = jnp.zeros(o_ref.shape, f32)

    @pl.when(c == 0)
    def _():
        rank_row = rank_t_ref[pl.ds(e, 1), :]
        slot = lax.broadcasted_iota(jnp.int32, (MOE_BLOCK, 1), 0).astype(f32)

        def gather(j, carry):
            base = (j * MOE_BLOCK).astype(f32)
            onehot = jnp.where(rank_row == slot + base, 1.0, 0.0).astype(bf16)
            r0 = pl.multiple_of(j * MOE_BLOCK, MOE_BLOCK)
            xg_scr[pl.ds(r0, MOE_BLOCK), :] = _dot(onehot, hb_ref[...]).astype(bf16)
            acc_scr[pl.ds(r0, MOE_BLOCK), :] = jnp.zeros((MOE_BLOCK, D_MODEL), f32)
            return carry

        lax.fori_loop(0, n_blk, gather, 0)

    def expert(j, carry):
        r0 = pl.multiple_of(j * MOE_BLOCK, MOE_BLOCK)
        xb = xg_scr[pl.ds(r0, MOE_BLOCK), :]
        a = _dot(xb, w1_ref[...])
        b = _dot(xb, w3_ref[...])
        t = (a * _sigmoid(a) * b).astype(bf16)
        acc_scr[pl.ds(r0, MOE_BLOCK), :] += _dot(t, w2_ref[...])
        return carry

    lax.fori_loop(0, n_blk, expert, 0)

    @pl.when(c == pl.num_programs(2) - 1)
    def _():
        lane8 = lax.broadcasted_iota(jnp.int32, (tm, N_EXPERTS), 1)
        pick = lambda ref: jnp.sum(jnp.where(lane8 == e, ref[...], 0.0), axis=1, keepdims=True)
        rank_col = pick(rank_c_ref)
        gate_col = pick(gate_c_ref)
        slot = lax.broadcasted_iota(jnp.int32, (1, MOE_BLOCK), 1).astype(f32)

        def scatter(j, carry):
            base = (j * MOE_BLOCK).astype(f32)
            onehot = jnp.where(rank_col == slot + base, 1.0, 0.0).astype(bf16)
            r0 = pl.multiple_of(j * MOE_BLOCK, MOE_BLOCK)
            yb = acc_scr[pl.ds(r0, MOE_BLOCK), :]
            hi = yb.astype(bf16)
            lo = (yb - hi.astype(f32)).astype(bf16)
            o_ref[...] += gate_col * (_dot(onehot, hi) + _dot(onehot, lo))
            return carry

        lax.fori_loop(0, n_blk, scatter, 0)


def _moe(cnt, hb, rank_t, gate_c, rank_c, w1, w3, w2):
    n = hb.shape[0]
    tm = MOE_TM
    d_ff = w1.shape[2]
    fw = d_ff // MOE_FC
    tile = lambda i, e, c, cnt: (i, 0)
    once = dict(pipeline_mode=pl.Buffered(1))
    grid_spec = pltpu.PrefetchScalarGridSpec(
        num_scalar_prefetch=1, grid=(n // tm, N_EXPERTS, MOE_FC),
        in_specs=[pl.BlockSpec((tm, D_MODEL), tile, **once),
                  pl.BlockSpec((N_EXPERTS, tm), lambda i, e, c, cnt: (0, i), **once),
                  pl.BlockSpec((tm, N_EXPERTS), tile, **once), pl.BlockSpec((tm, N_EXPERTS), tile, **once),
                  pl.BlockSpec((None, D_MODEL, fw), lambda i, e, c, cnt: (e, 0, c)),
                  pl.BlockSpec((None, D_MODEL, fw), lambda i, e, c, cnt: (e, 0, c)),
                  pl.BlockSpec((None, fw, D_MODEL), lambda i, e, c, cnt: (e, c, 0))],
        out_specs=pl.BlockSpec((tm, D_MODEL), tile),
        scratch_shapes=[pltpu.VMEM((tm, D_MODEL), bf16), pltpu.VMEM((tm, D_MODEL), f32)])
    vmem = (tm * D_MODEL * (2 + 2 * 4) + tm * D_MODEL * 6 + 2 * 3 * D_MODEL * fw * 2
            + 3 * tm * 128 * 4 + 3 * tm * D_MODEL * 4 + 4 * MIB)
    return pl.pallas_call(
        _moe_body, grid_spec=grid_spec, out_shape=jax.ShapeDtypeStruct((n, D_MODEL), f32),
        compiler_params=_params(vmem, 3), name="moe_experts",
    )(cnt, hb, rank_t, gate_c, rank_c, w1, w3, w2)


NORM_TM = 512


def _norm_body(x_ref, y_ref, g_ref, o_ref):
    x = x_ref[...] + y_ref[...]
    o_ref[...] = x * lax.rsqrt(jnp.mean(x * x, axis=-1, keepdims=True) + RMS_EPS) * g_ref[...]


def _add_norm(x, y, g):
    n = x.shape[0]
    row = pl.BlockSpec((NORM_TM, D_MODEL), lambda i: (i, 0))
    return pl.pallas_call(
        _norm_body, grid=(n // NORM_TM,), in_specs=[row, row, _const_spec((1, D_MODEL))], out_specs=row,
        out_shape=jax.ShapeDtypeStruct((n, D_MODEL), f32),
        compiler_params=_params(10 * NORM_TM * D_MODEL * 4, 1), name="final_norm",
    )(x, y, g)


SCAN_CHUNK = 64
SAMPLE_PAD = 8
SAMPLE_SEQ_PER_STEP = 8


def _pair_states(s):
    n = s.shape[0]
    s = s.reshape(n, N_PAIR, 2, HEAD_B, HEAD_B)
    z = jnp.zeros_like(s[:, :, 0])
    top = jnp.concatenate([s[:, :, 0], z], axis=-1)
    bot = jnp.concatenate([z, s[:, :, 1]], axis=-1)
    return jnp.concatenate([top, bot], axis=-2)


def _unpair_states(sp):
    n = sp.shape[0]
    even = sp[:, :, :HEAD_B, :HEAD_B]
    odd = sp[:, :, HEAD_B:, HEAD_B:]
    return jnp.stack([even, odd], axis=2).reshape(n, H_B, HEAD_B, HEAD_B)


def kernel(x_prompt, x_sample, state_wkv, state_shift, norm_mix_g, w_in, shift_mu, sgu_ln_g, sgu_ln_b, sgu_w, sgu_b, rwkv_w0, rwkv_w2, rwkv_a0, rwkv_a2, rwkv_g2, rwkv_v0, rwkv_v1, rwkv_v2, rwkv_k_k, rwkv_k_a, rwkv_r_k, rwkv_ln_g, rwkv_ln_b, w_branch_a, w_branch_b, w_out, norm_ffn_g, ffn_w1, ffn_w3, ffn_w2, moe_router, moe_w1, moe_w3, moe_w2, norm_final_g):
    bp, tp, d = x_prompt.shape
    bs, ts, _ = x_sample.shape
    depth = w_in.shape[0]
    n_p, n_s = bp * tp, bs * ts
    x = jnp.concatenate([x_prompt.reshape(n_p, d), x_sample.reshape(n_s, d)], axis=0)
    row = lambda a: a.reshape(1, -1)

    hid = jnp.arange(4 * HEAD_B) // HEAD_B
    bd = (hid[:, None] == hid[None, :]).astype(bf16)
    tpos = jnp.arange(CHUNK)
    mask_long = (tpos[None, :] <= tpos[:, None]).astype(f32)
    mask_short = ((tpos[None, :] // ts == tpos[:, None] // ts) & (tpos[None, :] <= tpos[:, None])).astype(f32)
    sgu_mask = jnp.stack([mask_long, mask_short])
    zero_shift = jnp.zeros((bp, N_SHIFT), f32)
    zero_state = jnp.zeros((bp, N_PAIR, PAIR, PAIR), f32)

    wkv_p, shift_p, wkv_s, shift_s, chunk_v = [], [], [], [], []
    vf_p = vf_s = None
    for l in range(depth):
        vparams = None
        if l > 0:
            vparams = (rwkv_v1[l - 1].astype(bf16), rwkv_v2[l - 1].astype(bf16), row(rwkv_v0[l - 1]))
        outs = _in_proj(x, row(norm_mix_g[l]), w_in[l].astype(bf16), row(sgu_ln_g[l]), row(sgu_ln_b[l]), vparams)
        u, va, ps, gates = outs[:4]
        ps_p, ps_s = ps[:n_p], ps[n_p:]
        shift_p.append(ps_p.reshape(bp, tp, N_SHIFT)[:, -1])
        shift_s.append(ps_s.reshape(bs, ts, N_SHIFT)[:, -1])
        chunk_v.append(va[n_p:].reshape(bs, ts, D_A))

        zpad = jnp.zeros((LORA_W, D_B), f32)
        wa2 = jnp.concatenate([jnp.concatenate([rwkv_w2[l], zpad], axis=1),
                               jnp.concatenate([zpad, rwkv_a2[l]], axis=1)], axis=0).astype(bf16)
        wts = (row(shift_mu[l]), wa2, row(rwkv_w0[l]), row(rwkv_a0[l]), rwkv_g2[l].astype(bf16),
               row(rwkv_k_k[l]), row(rwkv_k_a[l]), row(rwkv_r_k[l]), bd)
        first_s = jnp.pad(state_shift[l][:, None, :], ((0, 0), (0, ts - 1), (0, 0))).reshape(n_s, N_SHIFT)
        vx_p = vx_s = None
        if l > 0:
            vg = outs[4]
            vx_p, vx_s = (vg[:n_p], vf_p), (vg[n_p:], vf_s)
        pre_p = _prep(ps_p, zero_shift, 0, wts, vx_p)
        pre_s = _prep(ps_s, first_s, ts, wts, vx_s)
        if l == 0:
            vf_p, vf_s = pre_p[3], pre_s[3]

        ops_p = [a.reshape(bp, tp, D_B) for a in pre_p[:6]]
        y_p, sl_p = _scan(ops_p, zero_state, SCAN_CHUNK, SCAN_CHUNK, 1)
        ops_s = [jnp.pad(a.reshape(bs, ts, D_B), ((0, 0), (0, SAMPLE_PAD - ts), (0, 0))) for a in pre_s[:6]]
        y_s, sl_s = _scan(ops_s, _pair_states(state_wkv[l]), SAMPLE_PAD, ts, SAMPLE_SEQ_PER_STEP)
        wkv_p.append(_unpair_states(sl_p))
        wkv_s.append(_unpair_states(sl_s))
        y = jnp.concatenate([y_p.reshape(n_p, D_B), y_s[:, :ts].reshape(n_s, D_B)], axis=0)
        g = jnp.concatenate([pre_p[6], pre_s[6]], axis=0)
        rkv = jnp.concatenate([pre_p[7], pre_s[7]], axis=0)

        w_short = jnp.tile(sgu_w[l][:, :ts, :ts], (1, CHUNK // ts, CHUNK // ts))
        sgu_w2 = jnp.stack([sgu_w[l], w_short])
        b_long = jnp.repeat(sgu_b[l].T, GROUP_A, axis=1)
        b_short = jnp.tile(b_long[:ts], (CHUNK // ts, 1))
        sgu_b2 = jnp.stack([b_long, b_short])
        x = _mix_out(u, va, y, rkv, g, gates, x, sgu_w2, sgu_mask, sgu_b2, n_p,
                     row(rwkv_ln_g[l]), row(rwkv_ln_b[l]), bd,
                     w_branch_a[l].astype(bf16), w_branch_b[l].astype(bf16), w_out[l].astype(bf16))

        j = l // 2
        if l % 2 == 0:
            x = _ffn_dense(x, row(norm_ffn_g[l]), ffn_w1[j].astype(bf16), ffn_w3[j].astype(bf16),
                           ffn_w2[j].astype(bf16))
            delta = None
        else:
            hb, _, rank_t, gate_c, rank_c, cnt = _router(x, row(norm_ffn_g[l]), moe_router[j].T)
            delta = _moe(cnt[:, :, 0].reshape(-1), hb, rank_t, gate_c, rank_c,
                         moe_w1[j].astype(bf16), moe_w3[j].astype(bf16), moe_w2[j].astype(bf16))
            if l < depth - 1:
                x = x + delta
                delta = None

    yn = _add_norm(x, jnp.zeros_like(x) if delta is None else delta, row(norm_final_g))
    return (yn[:n_p].reshape(bp, tp, d), yn[n_p:].reshape(bs, ts, d),
            jnp.stack(wkv_p), jnp.stack(shift_p), jnp.stack(wkv_s), jnp.stack(shift_s),
            jnp.stack(chunk_v))
```

```python
import functools

import jax
import jax.numpy as jnp
from jax import lax
from jax.experimental import pallas as pl
from jax.experimental.pallas import tpu as pltpu

f32 = jnp.float32
bf16 = jnp.bfloat16

D_MODEL = 1024
CHUNK = 128
D_A = D_MODEL
GROUP_A = 128
H_A = D_A // GROUP_A
D_B = D_MODEL
HEAD_B = 64
H_B = D_B // HEAD_B
LORA_W = 64
LORA_A = 64
LORA_G = 128
N_SHIFT = 3 * D_B + LORA_W + LORA_A + LORA_G
N_IN = 2 * D_A + N_SHIFT + 2 * D_MODEL
N_EXPERTS = 8
RMS_EPS = 1e-6
LN_EPS = 1e-5
GN_EPS = 64e-5

PAIR = 2 * HEAD_B
N_PAIR = H_B // 2
MIB = 1024 * 1024
VMEM_CAP_V7X = 56 * MIB


def _params(vmem_bytes, n_grid):
    return pltpu.CompilerParams(
        dimension_semantics=("arbitrary",) * n_grid,
        vmem_limit_bytes=int(min(max(vmem_bytes, 16 * MIB), VMEM_CAP_V7X)),
    )


def _const_spec(shape):
    nd = len(shape)
    return pl.BlockSpec(shape, lambda *_: (0,) * nd, pipeline_mode=pl.Buffered(1))


def _gelu(x):
    return 0.5 * x * (1.0 + lax.erf(x * (2.0 ** -0.5)))


def _sigmoid(x):
    return 1.0 / (1.0 + jnp.exp(-x))


def _dot(a, b):
    return jnp.dot(a, b, preferred_element_type=f32)


def _dot_nt(a, b):
    return lax.dot_general(a, b, (((1,), (1,)), ((), ())), preferred_element_type=f32)


def _dot_tn(a, b):
    return lax.dot_general(a, b, (((0,), (0,)), ((), ())), preferred_element_type=f32)


def _head_sum(x, bd_ref):
    w = bd_ref.shape[0]
    cols = [_dot(x[:, c:c + w].astype(bf16), bd_ref[...]) for c in range(0, x.shape[1], w)]
    return jnp.concatenate(cols, axis=1)


IN_TM = 256
IN_CW = 512


def _in_proj_body(has_v, x_ref, g_ref, w_ref, lng_ref, lnb_ref, *rest):
    if has_v:
        v1_ref, v2_ref, v0_ref, u_ref, va_ref, ps_ref, gate_ref, vg_ref, h_scr = rest
    else:
        u_ref, va_ref, ps_ref, gate_ref, h_scr = rest
    x = x_ref[...]
    h = x * lax.rsqrt(jnp.mean(x * x, axis=-1, keepdims=True) + RMS_EPS) * g_ref[...]
    h_scr[...] = h.astype(bf16)

    def mm(c0, c1):
        return _dot(h_scr[...], w_ref[:, c0:c1])

    for c in range(0, D_A, IN_CW):
        u_ref[:, c:c + IN_CW] = _gelu(mm(c, c + IN_CW))
    for c in range(0, D_A, IN_CW):
        va_ref[:, c:c + IN_CW] = _gelu(mm(D_A + c, D_A + c + IN_CW))
    t = va_ref[...]
    mu = jnp.mean(t, axis=-1, keepdims=True)
    d = t - mu
    var = jnp.mean(d * d, axis=-1, keepdims=True)
    va_ref[...] = d * lax.rsqrt(var + LN_EPS) * lng_ref[...] + lnb_ref[...]
    for c in range(0, N_SHIFT, IN_CW):
        c1 = min(c + IN_CW, N_SHIFT)
        ps_ref[:, c:c1] = mm(2 * D_A + c, 2 * D_A + c1)
    base = 2 * D_A + N_SHIFT
    for c in range(0, 2 * D_MODEL, IN_CW):
        gate_ref[:, c:c + IN_CW] = _sigmoid(mm(base + c, base + c + IN_CW))
    if has_v:
        lv = _dot(h_scr[...], v1_ref[...])
        vg_ref[...] = _sigmoid(v0_ref[...] + _dot(lv.astype(bf16), v2_ref[...]))


def _in_proj(x, g, w_in, lng, lnb, vparams):
    n = x.shape[0]
    tm = IN_TM
    has_v = vparams is not None
    row = lambda w: pl.BlockSpec((tm, w), lambda i: (i, 0))
    in_specs = [row(D_MODEL), _const_spec((1, D_MODEL)), _const_spec((D_MODEL, N_IN)),
                _const_spec((1, D_A)), _const_spec((1, D_A))]
    args = [x, g, w_in, lng, lnb]
    out_shape = [jax.ShapeDtypeStruct((n, D_A), f32), jax.ShapeDtypeStruct((n, D_A), f32),
                 jax.ShapeDtypeStruct((n, N_SHIFT), f32), jax.ShapeDtypeStruct((n, 2 * D_MODEL), f32)]
    out_specs = [row(D_A), row(D_A), row(N_SHIFT), row(2 * D_MODEL)]
    if has_v:
        v1, v2, v0 = vparams
        in_specs += [_const_spec(v1.shape), _const_spec(v2.shape), _const_spec((1, D_B))]
        args += [v1, v2, v0]
        out_shape.append(jax.ShapeDtypeStruct((n, D_B), f32))
        out_specs.append(row(D_B))
    out_w = 2 * D_A + N_SHIFT + 2 * D_MODEL + (D_B if has_v else 0)
    vmem = 2 * D_MODEL * N_IN + 2 * tm * 4 * (D_MODEL + out_w) + 6 * tm * D_MODEL * 4 + 4 * MIB
    return pl.pallas_call(
        functools.partial(_in_proj_body, has_v),
        grid=(n // tm,), in_specs=in_specs, out_specs=out_specs, out_shape=out_shape,
        scratch_shapes=[pltpu.VMEM((tm, D_MODEL), bf16)],
        compiler_params=_params(vmem, 1), name="in_proj",
    )(*args)


PREP_TM = 256
N_PREP_OUT = 8


def _prep_body(n_long_tiles, tiles_per_seq, period, has_v, ps_ref, prevblk_ref, first_long_ref,
               first_short_ref, *rest):
    (mu_ref, wa2_ref, w0_ref, a0_ref, g2_ref, kk_ref, ka_ref, rk_ref, bd_ref), rest = rest[:9], rest[9:]
    if has_v:
        (vg_ref, vf_ref), rest = rest[:2], rest[2:]
    r_out, lw_out, k_out, v_out, na_out, b_out, g_out, rkv_out = rest
    i = pl.program_id(0)
    ps = ps_ref[...]
    tm = ps.shape[0]
    row = lax.broadcasted_iota(jnp.int32, (tm, 1), 0)
    is_long = i < n_long_tiles
    seq = jnp.minimum(i // tiles_per_seq, first_long_ref.shape[0] - 1)
    row0 = jnp.where(i % tiles_per_seq == 0, first_long_ref[pl.ds(seq, 1), :], prevblk_ref[7:8, :])
    starts = jnp.where(is_long, row, row % period) == 0
    init = jnp.where(is_long, row0, first_short_ref[...])
    prev = jnp.where(starts, init, pltpu.roll(ps, 1, 0))
    xs = ps + mu_ref[...] * (prev - ps)
    r = xs[:, 0:D_B]
    k = xs[:, D_B:2 * D_B]
    v = xs[:, 2 * D_B:3 * D_B]
    dwa = xs[:, 3 * D_B:3 * D_B + LORA_W + LORA_A]
    dg = xs[:, 3 * D_B + LORA_W + LORA_A:N_SHIFT]
    lane = lax.broadcasted_iota(jnp.int32, dwa.shape, 1)
    lhs = jnp.where(lane < LORA_W, jnp.tanh(dwa), dwa).astype(bf16)
    wa = _dot(lhs, wa2_ref[...])
    w_log = -jax.nn.softplus(-(w0_ref[...] + wa[:, :D_B])) - 0.5
    lw_out[...] = -jnp.exp(w_log)
    a = _sigmoid(a0_ref[...] + wa[:, D_B:])
    g_out[...] = _dot(_sigmoid(dg).astype(bf16), g2_ref[...])
    kkr = k * kk_ref[...]
    kk = kkr * lax.rsqrt(_head_sum(kkr * kkr, bd_ref) + 1e-12)
    k2 = k * (1.0 + (a - 1.0) * ka_ref[...])
    if has_v:
        v = v + (vf_ref[...] - v) * vg_ref[...]
    r_out[...] = r
    k_out[...] = k2
    v_out[...] = v
    na_out[...] = -kk
    b_out[...] = kk * a
    rkv_out[...] = _head_sum(r * k2 * rk_ref[...], bd_ref) * v


def _prep(ps, n_long, first_long, first_short, period, wts, vextra):
    n = ps.shape[0]
    tm = PREP_TM
    has_v = vextra is not None
    n_long_tiles = n_long // tm
    row = lambda w: pl.BlockSpec((tm, w), lambda i: (i, 0))
    in_specs = [row(N_SHIFT),
                pl.BlockSpec((8, N_SHIFT), lambda i: (jnp.maximum(i * (tm // 8) - 1, 0), 0)),
                _const_spec(first_long.shape),
                pl.BlockSpec((tm, N_SHIFT), lambda i: (jnp.maximum(i - n_long_tiles, 0), 0))]
    args = [ps, ps, first_long, first_short]
    in_specs += [_const_spec(w.shape) for w in wts]
    args += list(wts)
    if has_v:
        in_specs += [row(D_B), row(D_B)]
        args += list(vextra)
    vmem = 2 * tm * 4 * (3 * N_SHIFT + (N_PREP_OUT + 2) * D_B) + 16 * tm * D_B * 4 + 4 * MIB
    return pl.pallas_call(
        functools.partial(_prep_body, n_long_tiles, n_long_tiles // first_long.shape[0], period, has_v),
        grid=(n // tm,), in_specs=in_specs,
        out_specs=[row(D_B)] * N_PREP_OUT,
        out_shape=[jax.ShapeDtypeStruct((n, D_B), f32)] * N_PREP_OUT,
        compiler_params=_params(vmem, 1), name="rwkv_prep",
    )(*args)


SCAN_ROWS = 64


def _scan_body(n_seq, slen, has_s0, has_alias, r_ref, lw_ref, k_ref, v_ref, na_ref, b_ref, tri_ref,
               tot_ref, *rest):
    if has_s0:
        s0_ref, rest = rest[0], rest[1:]
    if has_alias:
        rest = rest[1:]
    y_ref, sl_ref, s_scr = rest
    c = pl.program_id(1)

    @pl.when(c == 0)
    def _():
        if has_s0:
            zero = jnp.zeros((HEAD_B, HEAD_B), f32)

            def pack(g, carry):
                for p in range(N_PAIR):
                    top = jnp.concatenate([s0_ref[g, 2 * p], zero], axis=1)
                    bot = jnp.concatenate([zero, s0_ref[g, 2 * p + 1]], axis=1)
                    s_scr[p, pl.ds(pl.multiple_of(g * PAIR, PAIR), PAIR), :] = jnp.concatenate([top, bot], axis=0)
                return carry

            lax.fori_loop(0, n_seq, pack, 0)
        else:
            s_scr[...] = jnp.zeros(s_scr.shape, f32)

    lw = lw_ref[...]
    cum = jnp.dot(tri_ref[...], lw, precision=lax.Precision.HIGHEST, preferred_element_type=f32)
    tot = jnp.dot(tot_ref[...], lw, precision=lax.Precision.HIGHEST, preferred_element_type=f32)
    g_inv = jnp.exp(-cum)
    g_tail = jnp.exp(tot - cum)
    at = na_ref[...] * jnp.exp(cum - lw)
    rt = r_ref[...] * jnp.exp(cum)
    bt = b_ref[...] * g_inv
    kt = k_ref[...] * g_inv
    bh = b_ref[...] * g_tail
    kh = k_ref[...] * g_tail
    g_last = jnp.exp(tot)
    vv = v_ref[...]

    n_st = 2 * SCAN_ROWS
    def causal(width, inclusive):
        ri = lax.broadcasted_iota(jnp.int32, (n_st, width), 0)
        ci = lax.broadcasted_iota(jnp.int32, (n_st, width), 1) % n_st
        same = (ri // SCAN_ROWS == ci // SCAN_ROWS) & ((ri % SCAN_ROWS) // slen == (ci % SCAN_ROWS) // slen)
        t_r, t_c = ri % SCAN_ROWS, ci % SCAN_ROWS
        return same & ((t_c <= t_r) if inclusive else (t_c < t_r))

    incl2 = causal(2 * n_st, True)
    strict = causal(n_st, False)
    lane_e = lax.broadcasted_iota(jnp.int32, (1, PAIR), 1) < HEAD_B
    row_seq = ((lax.broadcasted_iota(jnp.int32, (2 * n_st, 1), 0) % SCAN_ROWS) // slen)

    def stack(z):
        return jnp.concatenate([jnp.where(lane_e, z, 0.0), jnp.where(lane_e, 0.0, z)], axis=0).astype(bf16)

    def pick_seq(big):
        if n_seq == 1:
            return big
        acc = jnp.where(row_seq == 0, big[:, :PAIR], 0.0)
        for g in range(1, n_seq):
            acc = acc + jnp.where(row_seq == g, big[:, g * PAIR:(g + 1) * PAIR], 0.0)
        return acc

    def spread_seq(z):
        if n_seq == 1:
            return z
        return jnp.concatenate([jnp.where(row_seq == g, z, jnp.zeros_like(z)) for g in range(n_seq)], axis=1)

    sls = [slice(p * PAIR, (p + 1) * PAIR) for p in range(N_PAIR)]
    pairs = range(N_PAIR)
    ar = [jnp.concatenate([stack(at[:, s]), stack(rt[:, s])], axis=0) for s in sls]
    bk = [jnp.concatenate([bt[:, s], bt[:, s], kt[:, s], kt[:, s]], axis=0).astype(bf16) for s in sls]
    pm = [_dot_nt(ar[p], bk[p]) for p in pairs]
    vb = [stack(vv[:, s]) for s in sls]
    qs = [pick_seq(_dot_nt(ar[p], s_scr[p].astype(bf16))) for p in pairs]
    x = [qs[p][:n_st] + _dot(jnp.where(strict, pm[p][:n_st, n_st:], 0.0).astype(bf16), vb[p]) for p in pairs]
    m = [jnp.where(strict, pm[p][:n_st, :n_st], 0.0).astype(bf16) for p in pairs]
    span = 1
    while span < slen:
        x = [x[p] + _dot(m[p], x[p].astype(bf16)) for p in pairs]
        span *= 2
        if span < slen:
            m = [_dot(m[p], m[p]).astype(bf16) for p in pairs]
    uv = [jnp.concatenate([x[p].astype(bf16), vb[p]], axis=0) for p in pairs]
    for p in pairs:
        yb = qs[p][n_st:] + _dot(jnp.where(incl2, pm[p][n_st:], 0.0).astype(bf16), uv[p])
        y_ref[:, sls[p]] = yb[:SCAN_ROWS] + yb[SCAN_ROWS:]
    for p in pairs:
        bkh = jnp.concatenate([stack(bh[:, sls[p]]), stack(kh[:, sls[p]])], axis=0)
        upd = _dot_tn(spread_seq(uv[p]), bkh)
        for g in range(n_seq):
            rows = slice(g * PAIR, (g + 1) * PAIR)
            s_scr[p, rows, :] = (s_scr[p, rows, :] * g_last[g * slen:g * slen + 1, sls[p]] + upd[rows])

    @pl.when(c == pl.num_programs(1) - 1)
    def _():
        def unpack(g, carry):
            for p in pairs:
                tile = s_scr[p, pl.ds(pl.multiple_of(g * PAIR, PAIR), PAIR), :]
                sl_ref[g, 2 * p] = tile[:HEAD_B, :HEAD_B]
                sl_ref[g, 2 * p + 1] = tile[HEAD_B:, HEAD_B:]
            return carry

        lax.fori_loop(0, n_seq, unpack, 0)


def _scan(ops, y_prev, s0, n_seq_total, slen, row0):
    n = ops[0].shape[0]
    long_seq = slen >= SCAN_ROWS
    if long_seq:
        n_seq, clen, n_chunk, n_outer = 1, SCAN_ROWS, slen // SCAN_ROWS, n_seq_total
    else:
        n_seq, clen, n_chunk, n_outer = SCAN_ROWS // slen, slen, 1, n_seq_total * slen // SCAN_ROWS
    blk0 = row0 // SCAN_ROWS
    tok = pl.BlockSpec((SCAN_ROWS, D_B), lambda i, c: (blk0 + i * n_chunk + c, 0))
    st = pl.BlockSpec((n_seq, H_B, HEAD_B, HEAD_B), lambda i, c: (i, 0, 0, 0))
    sq = pl.BlockSpec((SCAN_ROWS, SCAN_ROWS), lambda i, c: (0, 0))
    t = jnp.arange(SCAN_ROWS)
    same_seq = t[:, None] // clen == t[None, :] // clen
    tri = (same_seq & (t[None, :] <= t[:, None])).astype(f32)
    in_specs = [tok] * 6 + [sq, sq]
    args = list(ops) + [tri, same_seq.astype(f32)]
    aliases = {}
    if s0 is not None:
        in_specs.append(st)
        args.append(s0)
    if y_prev is not None:
        aliases = {len(args): 0}
        in_specs.append(pl.BlockSpec(memory_space=pl.ANY))
        args.append(y_prev)
    state_pad = 2 * n_seq * H_B * HEAD_B * PAIR * 4
    vmem = (2 * 7 * SCAN_ROWS * D_B * 4 + 2 * state_pad * (2 if s0 is not None else 1)
            + N_PAIR * n_seq * PAIR * PAIR * 4 + 40 * SCAN_ROWS * D_B * 4 + 8 * n_seq * PAIR * PAIR * 4 * 4
            + 6 * MIB)
    return pl.pallas_call(
        functools.partial(_scan_body, n_seq, clen, s0 is not None, y_prev is not None),
        grid=(n_outer, n_chunk), in_specs=in_specs, out_specs=[tok, st],
        out_shape=[jax.ShapeDtypeStruct((n, D_B), f32),
                   jax.ShapeDtypeStruct((n_seq_total, H_B, HEAD_B, HEAD_B), f32)],
        scratch_shapes=[pltpu.VMEM((N_PAIR, n_seq * PAIR, PAIR), f32)],
        input_output_aliases=aliases,
        compiler_params=_params(vmem, 2), name="wkv_scan",
    )(*args)


MIX_TM = 256


def _mix_body(u_ref, va_ref, y_ref, rkv_ref, g_ref, gate_ref, x_ref, sw_ref, sm_ref, sb_ref,
              lng_ref, lnb_ref, bd_ref, wa_ref, wb_ref, wo_ref, o_ref, ya_scr):
    tm = u_ref.shape[0]
    for r0 in range(0, tm, CHUNK):
        for grp in range(H_A):
            cs = slice(grp * GROUP_A, (grp + 1) * GROUP_A)
            w = jnp.where(sm_ref[...] > 0.5, sw_ref[grp], 0.0).astype(bf16)
            mixed = _dot(w, va_ref[r0:r0 + CHUNK, cs].astype(bf16)) + sb_ref[:, cs]
            ya_scr[r0:r0 + CHUNK, cs] = (u_ref[r0:r0 + CHUNK, cs] * mixed).astype(bf16)
    y = y_ref[...]
    mu = _head_sum(y, bd_ref) * (1.0 / HEAD_B)
    d = y - mu
    var = _head_sum(d * d, bd_ref) * (1.0 / HEAD_B)
    yn = d * lax.rsqrt(var + GN_EPS) * lng_ref[...] + lnb_ref[...] + rkv_ref[...]
    yb = (yn * g_ref[...]).astype(bf16)
    merged = (gate_ref[:, :D_MODEL] * _dot(ya_scr[...], wa_ref[...])
              + gate_ref[:, D_MODEL:] * _dot(yb, wb_ref[...]))
    o_ref[...] = x_ref[...] + _dot(merged.astype(bf16), wo_ref[...])


def _mix_out(u, va, y, rkv, g, gates, x, sgu_w2, sgu_m2, sgu_b2, n_long, lng, lnb, bd, wa, wb, wo):
    n = x.shape[0]
    tm = MIX_TM
    row = lambda w: pl.BlockSpec((tm, w), lambda i: (i, 0))
    kind = lambda i: jnp.where(i * tm >= n_long, 1, 0)
    in_specs = [row(D_A), row(D_A), row(D_B), row(D_B), row(D_B), row(2 * D_MODEL), row(D_MODEL),
                pl.BlockSpec((None, H_A, CHUNK, CHUNK), lambda i: (kind(i), 0, 0, 0)),
                pl.BlockSpec((None, CHUNK, CHUNK), lambda i: (kind(i), 0, 0)),
                pl.BlockSpec((None, CHUNK, D_A), lambda i: (kind(i), 0, 0)),
                _const_spec((1, D_B)), _const_spec((1, D_B)), _const_spec(bd.shape),
                _const_spec(wa.shape), _const_spec(wb.shape), _const_spec(wo.shape)]
    vmem = 2 * tm * 4 * 9 * D_MODEL + 3 * 2 * D_MODEL * D_MODEL + 10 * tm * D_MODEL * 4 + 6 * MIB
    return pl.pallas_call(
        _mix_body, grid=(n // tm,), in_specs=in_specs, out_specs=row(D_MODEL),
        out_shape=jax.ShapeDtypeStruct((n, D_MODEL), f32),
        scratch_shapes=[pltpu.VMEM((tm, D_A), bf16)],
        compiler_params=_params(vmem, 1), name="mix_out",
    )(u, va, y, rkv, g, gates, x, sgu_w2, sgu_m2, sgu_b2, lng, lnb, bd, wa, wb, wo)


FFN_TM = 512
FFN_CW = 256


def _ffn_body(x_ref, g_ref, w1_ref, w3_ref, w2_ref, o_ref, h_scr):
    x = x_ref[...]
    h = x * lax.rsqrt(jnp.mean(x * x, axis=-1, keepdims=True) + RMS_EPS) * g_ref[...]
    h_scr[...] = h.astype(bf16)
    o_ref[...] = x
    d_ff = w1_ref.shape[1]
    for c in range(0, d_ff, FFN_CW):
        a = _dot(h_scr[...], w1_ref[:, c:c + FFN_CW])
        b = _dot(h_scr[...], w3_ref[:, c:c + FFN_CW])
        t = (a * _sigmoid(a) * b).astype(bf16)
        o_ref[...] += _dot(t, w2_ref[c:c + FFN_CW, :])


def _ffn_dense(x, g, w1, w3, w2):
    n = x.shape[0]
    tm = FFN_TM
    row = pl.BlockSpec((tm, D_MODEL), lambda i: (i, 0))
    vmem = 3 * 2 * D_MODEL * w1.shape[1] + 4 * tm * D_MODEL * 4 + 8 * tm * D_MODEL * 4 + 4 * MIB
    return pl.pallas_call(
        _ffn_body, grid=(n // tm,),
        in_specs=[row, _const_spec((1, D_MODEL)), _const_spec(w1.shape), _const_spec(w3.shape),
                  _const_spec(w2.shape)],
        out_specs=row, out_shape=jax.ShapeDtypeStruct((n, D_MODEL), f32),
        scratch_shapes=[pltpu.VMEM((tm, D_MODEL), bf16)],
        compiler_params=_params(vmem, 1), name="ffn_dense",
    )(x, g, w1, w3, w2)


MOE_TM = 1536
MOE_BLOCK = 256
MOE_FC = 4


def _split3(x):
    hi = x.astype(bf16)
    r1 = x - hi.astype(f32)
    mid = r1.astype(bf16)
    lo = (r1 - mid.astype(f32)).astype(bf16)
    return hi, mid, lo


def _router_body(x_ref, g_ref, wr_ref, hb_ref, gate_t_ref, rank_t_ref, rank_c_ref, cnt_ref):
    x = x_ref[...]
    tm = x.shape[0]
    h = x * lax.rsqrt(jnp.mean(x * x, axis=-1, keepdims=True) + RMS_EPS) * g_ref[...]
    hb_ref[...] = h.astype(bf16)
    h_hi, h_mid, h_lo = _split3(h)
    w_hi, w_mid, w_lo = _split3(wr_ref[...])
    logits = (_dot_nt(w_hi, h_hi) + _dot_nt(w_hi, h_mid) + _dot_nt(w_mid, h_hi)
              + _dot_nt(w_hi, h_lo) + _dot_nt(w_mid, h_mid) + _dot_nt(w_lo, h_hi))
    eid = lax.broadcasted_iota(jnp.int32, logits.shape, 0)
    m1 = jnp.max(logits, axis=0, keepdims=True)
    i1 = jnp.min(jnp.where(logits == m1, eid, N_EXPERTS), axis=0, keepdims=True)
    sel1 = eid == i1
    rest = jnp.where(sel1, -jnp.inf, logits)
    m2 = jnp.max(rest, axis=0, keepdims=True)
    i2 = jnp.min(jnp.where(rest == m2, eid, N_EXPERTS), axis=0, keepdims=True)
    sel2 = eid == i2
    e2 = jnp.exp(m2 - m1)
    den = 1.0 + e2
    gate_t = jnp.where(sel1, 1.0 / den, 0.0) + jnp.where(sel2, e2 / den, 0.0)
    sel = jnp.where(sel1 | sel2, 1.0, 0.0)
    s_idx = lax.broadcasted_iota(jnp.int32, (tm, tm), 0)
    t_idx = lax.broadcasted_iota(jnp.int32, (tm, tm), 1)
    before = jnp.where(s_idx < t_idx, 1.0, 0.0).astype(bf16)
    rank = _dot(sel.astype(bf16), before)
    rank_t = jnp.where(sel > 0.5, rank, -1.0)
    gate_t_ref[...] = gate_t
    rank_t_ref[...] = rank_t
    rank_c_ref[...] = rank_t.T
    cnt = jnp.sum(sel, axis=1, keepdims=True)
    cnt_ref[...] = jnp.broadcast_to(cnt, cnt_ref.shape).astype(jnp.int32)


def _router(x, g, wr_t):
    n = x.shape[0]
    tm = MOE_TM
    nt = n // tm
    vmem = 2 * tm * D_MODEL * 6 + 8 * tm * D_MODEL * 4 + 3 * tm * tm * 4 + 8 * MIB
    return pl.pallas_call(
        _router_body, grid=(nt,),
        in_specs=[pl.BlockSpec((tm, D_MODEL), lambda i: (i, 0)), _const_spec((1, D_MODEL)),
                  _const_spec(wr_t.shape)],
        out_specs=[pl.BlockSpec((tm, D_MODEL), lambda i: (i, 0)),
                   pl.BlockSpec((N_EXPERTS, tm), lambda i: (0, i)),
                   pl.BlockSpec((N_EXPERTS, tm), lambda i: (0, i)),
                   pl.BlockSpec((tm, N_EXPERTS), lambda i: (i, 0)),
                   pl.BlockSpec((None, N_EXPERTS, 128), lambda i: (i, 0, 0))],
        out_shape=[jax.ShapeDtypeStruct((n, D_MODEL), bf16),
                   jax.ShapeDtypeStruct((N_EXPERTS, n), f32), jax.ShapeDtypeStruct((N_EXPERTS, n), f32),
                   jax.ShapeDtypeStruct((n, N_EXPERTS), f32),
                   jax.ShapeDtypeStruct((nt, N_EXPERTS, 128), jnp.int32)],
        compiler_params=_params(vmem, 1), name="moe_router",
    )(x, g, wr_t)


def _moe_body(cnt_ref, hb_ref, rank_t_ref, gate_t_ref, rank_c_ref, w1_ref, w3_ref, w2_ref,
              o_ref, xg_scr, acc_scr):
    i, e, c = pl.program_id(0), pl.program_id(1), pl.program_id(2)
    tm = hb_ref.shape[0]
    n_blk = (cnt_ref[i * N_EXPERTS + e] + MOE_BLOCK - 1) // MOE_BLOCK

    @pl.when((e == 0) & (c == 0))
    def _():
        o_ref[...] = jnp.zeros(o_ref.shape, f32)

    @pl.when(c == 0)
    def _():
        rank_row = rank_t_ref[pl.ds(e, 1), :]
        slot = lax.broadcasted_iota(jnp.int32, (MOE_BLOCK, 1), 0).astype(f32)

        def gather(j, carry):
            base = (j * MOE_BLOCK).astype(f32)
            onehot = jnp.where(rank_row == slot + base, 1.0, 0.0).astype(bf16)
            r0 = pl.multiple_of(j * MOE_BLOCK, MOE_BLOCK)
            xg_scr[pl.ds(r0, MOE_BLOCK), :] = _dot(onehot, hb_ref[...]).astype(bf16)
            acc_scr[pl.ds(r0, MOE_BLOCK), :] = jnp.zeros((MOE_BLOCK, D_MODEL), f32)
            return carry

        lax.fori_loop(0, n_blk, gather, 0)

    def expert(j, carry):
        r0 = pl.multiple_of(j * MOE_BLOCK, MOE_BLOCK)
        xb = xg_scr[pl.ds(r0, MOE_BLOCK), :]
        a = _dot(xb, w1_ref[...])
        b = _dot(xb, w3_ref[...])
        t = (a * _sigmoid(a) * b).astype(bf16)
        acc_scr[pl.ds(r0, MOE_BLOCK), :] += _dot(t, w2_ref[...])
        return carry

    lax.fori_loop(0, n_blk, expert, 0)

    @pl.when(c == pl.num_programs(2) - 1)
    def _():
        lane8 = lax.broadcasted_iota(jnp.int32, (tm, N_EXPERTS), 1)
        rank_col = jnp.sum(jnp.where(lane8 == e, rank_c_ref[...], 0.0), axis=1, keepdims=True)
        rank_row = rank_t_ref[pl.ds(e, 1), :]
        gate_row = gate_t_ref[pl.ds(e, 1), :]
        slot_col = lax.broadcasted_iota(jnp.int32, (1, MOE_BLOCK), 1).astype(f32)
        slot_row = lax.broadcasted_iota(jnp.int32, (MOE_BLOCK, 1), 0).astype(f32)

        def scatter(j, carry):
            base = (j * MOE_BLOCK).astype(f32)
            gate_blk = jnp.sum(jnp.where(rank_row == slot_row + base, gate_row, 0.0), axis=1, keepdims=True)
            onehot = jnp.where(rank_col == slot_col + base, 1.0, 0.0).astype(bf16)
            r0 = pl.multiple_of(j * MOE_BLOCK, MOE_BLOCK)
            scaled = (acc_scr[pl.ds(r0, MOE_BLOCK), :] * gate_blk).astype(bf16)
            o_ref[...] += _dot(onehot, scaled)
            return carry

        lax.fori_loop(0, n_blk, scatter, 0)


def _moe(cnt, hb, rank_t, gate_t, rank_c, w1, w3, w2):
    n = hb.shape[0]
    tm = MOE_TM
    d_ff = w1.shape[2]
    fw = d_ff // MOE_FC
    tile = lambda i, e, c, cnt: (i, 0)
    once = dict(pipeline_mode=pl.Buffered(1))
    grid_spec = pltpu.PrefetchScalarGridSpec(
        num_scalar_prefetch=1, grid=(n // tm, N_EXPERTS, MOE_FC),
        in_specs=[pl.BlockSpec((tm, D_MODEL), tile, **once),
                  pl.BlockSpec((N_EXPERTS, tm), lambda i, e, c, cnt: (0, i), **once),
                  pl.BlockSpec((N_EXPERTS, tm), lambda i, e, c, cnt: (0, i), **once),
                  pl.BlockSpec((tm, N_EXPERTS), tile, **once),
                  pl.BlockSpec((None, D_MODEL, fw), lambda i, e, c, cnt: (e, 0, c)),
                  pl.BlockSpec((None, D_MODEL, fw), lambda i, e, c, cnt: (e, 0, c)),
                  pl.BlockSpec((None, fw, D_MODEL), lambda i, e, c, cnt: (e, c, 0))],
        out_specs=pl.BlockSpec((tm, D_MODEL), tile),
        scratch_shapes=[pltpu.VMEM((tm, D_MODEL), bf16), pltpu.VMEM((tm, D_MODEL), f32)])
    vmem = (tm * D_MODEL * (2 + 2 * 4) + tm * D_MODEL * 6 + 2 * 3 * D_MODEL * fw * 2
            + 3 * tm * 128 * 4 + 3 * tm * D_MODEL * 4 + 4 * MIB)
    return pl.pallas_call(
        _moe_body, grid_spec=grid_spec, out_shape=jax.ShapeDtypeStruct((n, D_MODEL), f32),
        compiler_params=_params(vmem, 3), name="moe_experts",
    )(cnt, hb, rank_t, gate_t, rank_c, w1, w3, w2)


NORM_TM = 512


def _norm_body(n_long_tiles, x_ref, y_ref, g_ref, o_long_ref, o_short_ref):
    x = x_ref[...] + y_ref[...]
    val = x * lax.rsqrt(jnp.mean(x * x, axis=-1, keepdims=True) + RMS_EPS) * g_ref[...]
    i = pl.program_id(0)

    @pl.when(i < n_long_tiles)
    def _():
        o_long_ref[...] = val

    @pl.when(i >= n_long_tiles)
    def _():
        o_short_ref[...] = val


def _add_norm(x, y, g, n_long):
    n = x.shape[0]
    tm = NORM_TM
    nl = n_long // tm
    row = pl.BlockSpec((tm, D_MODEL), lambda i: (i, 0))
    return pl.pallas_call(
        functools.partial(_norm_body, nl), grid=(n // tm,),
        in_specs=[row, row, _const_spec((1, D_MODEL))],
        out_specs=[pl.BlockSpec((tm, D_MODEL), lambda i: (jnp.minimum(i, nl - 1), 0)),
                   pl.BlockSpec((tm, D_MODEL), lambda i: (jnp.maximum(i - nl, 0), 0))],
        out_shape=[jax.ShapeDtypeStruct((n_long, D_MODEL), f32),
                   jax.ShapeDtypeStruct((n - n_long, D_MODEL), f32)],
        compiler_params=_params(12 * tm * D_MODEL * 4, 1), name="final_norm",
    )(x, y, g)


def kernel(x_prompt, x_sample, state_wkv, state_shift, norm_mix_g, w_in, shift_mu, sgu_ln_g, sgu_ln_b, sgu_w, sgu_b, rwkv_w0, rwkv_w2, rwkv_a0, rwkv_a2, rwkv_g2, rwkv_v0, rwkv_v1, rwkv_v2, rwkv_k_k, rwkv_k_a, rwkv_r_k, rwkv_ln_g, rwkv_ln_b, w_branch_a, w_branch_b, w_out, norm_ffn_g, ffn_w1, ffn_w3, ffn_w2, moe_router, moe_w1, moe_w3, moe_w2, norm_final_g):
    bp, tp, d = x_prompt.shape
    bs, ts, _ = x_sample.shape
    depth = w_in.shape[0]
    n_p, n_s = bp * tp, bs * ts
    x = jnp.concatenate([x_prompt.reshape(n_p, d), x_sample.reshape(n_s, d)], axis=0)
    row = lambda a: a.reshape(1, -1)

    hid = jnp.arange(4 * HEAD_B) // HEAD_B
    bd = (hid[:, None] == hid[None, :]).astype(bf16)
    tpos = jnp.arange(CHUNK)
    mask_long = (tpos[None, :] <= tpos[:, None]).astype(f32)
    mask_short = ((tpos[None, :] // ts == tpos[:, None] // ts) & (tpos[None, :] <= tpos[:, None])).astype(f32)
    sgu_mask = jnp.stack([mask_long, mask_short])
    zero_shift = jnp.zeros((bp, N_SHIFT), f32)

    wkv_p, shift_p, wkv_s, shift_s, chunk_v = [], [], [], [], []
    v_first = None
    for l in range(depth):
        vparams = None
        if l > 0:
            vparams = (rwkv_v1[l - 1].astype(bf16), rwkv_v2[l - 1].astype(bf16), row(rwkv_v0[l - 1]))
        outs = _in_proj(x, row(norm_mix_g[l]), w_in[l].astype(bf16), row(sgu_ln_g[l]), row(sgu_ln_b[l]), vparams)
        u, va, ps, gates = outs[:4]
        shift_p.append(ps[tp - 1:n_p:tp])
        shift_s.append(ps[n_p + ts - 1::ts])
        chunk_v.append(va[n_p:].reshape(bs, ts, D_A))

        zpad = jnp.zeros((LORA_W, D_B), f32)
        wa2 = jnp.concatenate([jnp.concatenate([rwkv_w2[l], zpad], axis=1),
                               jnp.concatenate([zpad, rwkv_a2[l]], axis=1)], axis=0).astype(bf16)
        wts = (row(shift_mu[l]), wa2, row(rwkv_w0[l]), row(rwkv_a0[l]), rwkv_g2[l].astype(bf16),
               row(rwkv_k_k[l]), row(rwkv_k_a[l]), row(rwkv_r_k[l]), bd)
        first_s = jnp.pad(state_shift[l][:, None, :], ((0, 0), (0, ts - 1), (0, 0))).reshape(n_s, N_SHIFT)
        pre = _prep(ps, n_p, zero_shift, first_s, ts, wts, (outs[4], v_first) if l > 0 else None)
        if l == 0:
            v_first = pre[3]
        g, rkv = pre[6], pre[7]
        y, sl_p = _scan(pre[:6], None, None, bp, tp, 0)
        y, sl_s = _scan(pre[:6], y, state_wkv[l], bs, ts, n_p)
        wkv_p.append(sl_p)
        wkv_s.append(sl_s)

        w_short = jnp.tile(sgu_w[l][:, :ts, :ts], (1, CHUNK // ts, CHUNK // ts))
        sgu_w2 = jnp.stack([sgu_w[l], w_short])
        b_long = jnp.repeat(sgu_b[l].T, GROUP_A, axis=1)
        b_short = jnp.tile(b_long[:ts], (CHUNK // ts, 1))
        sgu_b2 = jnp.stack([b_long, b_short])
        x = _mix_out(u, va, y, rkv, g, gates, x, sgu_w2, sgu_mask, sgu_b2, n_p,
                     row(rwkv_ln_g[l]), row(rwkv_ln_b[l]), bd,
                     w_branch_a[l].astype(bf16), w_branch_b[l].astype(bf16), w_out[l].astype(bf16))

        j = l // 2
        if l % 2 == 0:
            x = _ffn_dense(x, row(norm_ffn_g[l]), ffn_w1[j].astype(bf16), ffn_w3[j].astype(bf16),
                           ffn_w2[j].astype(bf16))
            delta = None
        else:
            hb, gate_t, rank_t, rank_c, cnt = _router(x, row(norm_ffn_g[l]), moe_router[j].T)
            delta = _moe(cnt[:, :, 0].reshape(-1), hb, rank_t, gate_t, rank_c,
                         moe_w1[j].astype(bf16), moe_w3[j].astype(bf16), moe_w2[j].astype(bf16))
            if l < depth - 1:
                x = x + delta
                delta = None

    yn_p, yn_s = _add_norm(x, jnp.zeros_like(x) if delta is None else delta, row(norm_final_g), n_p)
    return (yn_p.reshape(bp, tp, d), yn_s.reshape(bs, ts, d),
            jnp.stack(wkv_p), jnp.stack(shift_p), jnp.stack(wkv_s), jnp.stack(shift_s),
            jnp.stack(chunk_v))
```

```python
import functools

import jax
import jax.numpy as jnp
from jax import lax
from jax.experimental import pallas as pl
from jax.experimental.pallas import tpu as pltpu

f32 = jnp.float32
bf16 = jnp.bfloat16

D_MODEL = 1024
CHUNK = 128
D_A = D_MODEL
GROUP_A = 128
H_A = D_A // GROUP_A
D_B = D_MODEL
HEAD_B = 64
H_B = D_B // HEAD_B
LORA_W = 64
LORA_A = 64
LORA_G = 128
N_SHIFT = 3 * D_B + LORA_W + LORA_A + LORA_G
N_IN = 2 * D_A + N_SHIFT + 2 * D_MODEL
N_EXPERTS = 8
RMS_EPS = 1e-6
LN_EPS = 1e-5
GN_EPS = 64e-5

PAIR = 2 * HEAD_B
N_PAIR = H_B // 2
MIB = 1024 * 1024
VMEM_CAP_V7X = 56 * MIB


def _params(vmem_bytes, n_grid):
    return pltpu.CompilerParams(
        dimension_semantics=("arbitrary",) * n_grid,
        vmem_limit_bytes=int(min(max(vmem_bytes, 16 * MIB), VMEM_CAP_V7X)),
    )


def _const_spec(shape):
    nd = len(shape)
    return pl.BlockSpec(shape, lambda *_: (0,) * nd, pipeline_mode=pl.Buffered(1))


def _gelu(x):
    return 0.5 * x * (1.0 + lax.erf(x * (2.0 ** -0.5)))


def _sigmoid(x):
    return 1.0 / (1.0 + jnp.exp(-x))


def _dot(a, b):
    return jnp.dot(a, b, preferred_element_type=f32)


def _dot_nt(a, b):
    return lax.dot_general(a, b, (((1,), (1,)), ((), ())), preferred_element_type=f32)


def _dot_tn(a, b):
    return lax.dot_general(a, b, (((0,), (0,)), ((), ())), preferred_element_type=f32)


def _head_sum(x, bd_ref):
    w = bd_ref.shape[0]
    cols = [_dot(x[:, c:c + w].astype(bf16), bd_ref[...]) for c in range(0, x.shape[1], w)]
    return jnp.concatenate(cols, axis=1)


IN_TM = 256
IN_CW = 512


def _in_proj_body(n_long_tiles, has_v, x_ref, g_ref, w_ref, lng_ref, lnb_ref, *rest):
    if has_v:
        v1_ref, v2_ref, v0_ref, u_ref, va_ref, vs_ref, ps_ref, tail_ref, gate_ref, vg_ref, h_scr, t_scr = rest
    else:
        u_ref, va_ref, vs_ref, ps_ref, tail_ref, gate_ref, h_scr, t_scr = rest
    x = x_ref[...]
    tm = x.shape[0]
    h = x * lax.rsqrt(jnp.mean(x * x, axis=-1, keepdims=True) + RMS_EPS) * g_ref[...]
    h_scr[...] = h.astype(bf16)

    def mm(c0, c1):
        return _dot(h_scr[...], w_ref[:, c0:c1])

    for c in range(0, D_A, IN_CW):
        u_ref[:, c:c + IN_CW] = _gelu(mm(c, c + IN_CW)).astype(bf16)
    for c in range(0, D_A, IN_CW):
        t_scr[:, c:c + IN_CW] = _gelu(mm(D_A + c, D_A + c + IN_CW))
    t = t_scr[...]
    mu = jnp.mean(t, axis=-1, keepdims=True)
    d = t - mu
    var = jnp.mean(d * d, axis=-1, keepdims=True)
    va = d * lax.rsqrt(var + LN_EPS) * lng_ref[...] + lnb_ref[...]
    va_ref[...] = va.astype(bf16)

    @pl.when(pl.program_id(0) >= n_long_tiles)
    def _():
        vs_ref[...] = va

    for c in range(0, N_SHIFT, IN_CW):
        c1 = min(c + IN_CW, N_SHIFT)
        ps_ref[:, c:c1] = mm(2 * D_A + c, 2 * D_A + c1)
    tail_ref[...] = ps_ref[tm - 8:tm, :]
    base = 2 * D_A + N_SHIFT
    for c in range(0, 2 * D_MODEL, IN_CW):
        gate_ref[:, c:c + IN_CW] = _sigmoid(mm(base + c, base + c + IN_CW)).astype(bf16)
    if has_v:
        lv = _dot(h_scr[...], v1_ref[...])
        vg_ref[...] = _sigmoid(v0_ref[...] + _dot(lv.astype(bf16), v2_ref[...])).astype(bf16)


def _in_proj(x, n_long, g, w_in, lng, lnb, vparams):
    n = x.shape[0]
    tm = IN_TM
    nl = n_long // tm
    has_v = vparams is not None
    row = lambda w: pl.BlockSpec((tm, w), lambda i: (i, 0))
    in_specs = [row(D_MODEL), _const_spec((1, D_MODEL)), _const_spec((D_MODEL, N_IN)),
                _const_spec((1, D_A)), _const_spec((1, D_A))]
    args = [x, g, w_in, lng, lnb]
    out_shape = [jax.ShapeDtypeStruct((n, D_A), bf16), jax.ShapeDtypeStruct((n, D_A), bf16),
                 jax.ShapeDtypeStruct((n - n_long, D_A), f32),
                 jax.ShapeDtypeStruct((n, N_SHIFT), f32), jax.ShapeDtypeStruct((n // tm, 8, N_SHIFT), f32),
                 jax.ShapeDtypeStruct((n, 2 * D_MODEL), bf16)]
    out_specs = [row(D_A), row(D_A), pl.BlockSpec((tm, D_A), lambda i: (jnp.maximum(i - nl, 0), 0)),
                 row(N_SHIFT), pl.BlockSpec((None, 8, N_SHIFT), lambda i: (i, 0, 0)), row(2 * D_MODEL)]
    if has_v:
        v1, v2, v0 = vparams
        in_specs += [_const_spec(v1.shape), _const_spec(v2.shape), _const_spec((1, D_B))]
        args += [v1, v2, v0]
        out_shape.append(jax.ShapeDtypeStruct((n, D_B), bf16))
        out_specs.append(row(D_B))
    out_bytes = 2 * (2 * D_A + 2 * D_MODEL + (D_B if has_v else 0)) + 4 * (N_SHIFT + D_A)
    vmem = 2 * D_MODEL * N_IN + 2 * tm * (4 * D_MODEL + out_bytes) + 8 * tm * D_MODEL * 4 + 4 * MIB
    return pl.pallas_call(
        functools.partial(_in_proj_body, nl, has_v),
        grid=(n // tm,), in_specs=in_specs, out_specs=out_specs, out_shape=out_shape,
        scratch_shapes=[pltpu.VMEM((tm, D_MODEL), bf16), pltpu.VMEM((tm, D_A), f32)],
        compiler_params=_params(vmem, 1), name="in_proj",
    )(*args)


PREP_TM = 256
N_PREP_OUT = 8


def _prep_body(n_long_tiles, tiles_per_seq, period, has_v, ps_ref, prevblk_ref, first_long_ref,
               first_short_ref, *rest):
    (mu_ref, wa2_ref, w0_ref, a0_ref, g2_ref, kk_ref, ka_ref, rk_ref, bd_ref), rest = rest[:9], rest[9:]
    if has_v:
        (vg_ref, vf_ref), rest = rest[:2], rest[2:]
    r_out, lw_out, k_out, v_out, na_out, b_out, g_out, rkv_out = rest
    i = pl.program_id(0)
    ps = ps_ref[...]
    tm = ps.shape[0]
    row = lax.broadcasted_iota(jnp.int32, (tm, 1), 0)
    is_long = i < n_long_tiles
    seq = jnp.minimum(i // tiles_per_seq, first_long_ref.shape[0] - 1)
    row0 = jnp.where(i % tiles_per_seq == 0, first_long_ref[pl.ds(seq, 1), :], prevblk_ref[7:8, :])
    starts = jnp.where(is_long, row, row % period) == 0
    init = jnp.where(is_long, row0, first_short_ref[...])
    prev = jnp.where(starts, init, pltpu.roll(ps, 1, 0))
    xs = ps + mu_ref[...] * (prev - ps)
    r = xs[:, 0:D_B]
    k = xs[:, D_B:2 * D_B]
    v = xs[:, 2 * D_B:3 * D_B]
    dwa = xs[:, 3 * D_B:3 * D_B + LORA_W + LORA_A]
    dg = xs[:, 3 * D_B + LORA_W + LORA_A:N_SHIFT]
    lane = lax.broadcasted_iota(jnp.int32, dwa.shape, 1)
    lhs = jnp.where(lane < LORA_W, jnp.tanh(dwa), dwa).astype(bf16)
    wa = _dot(lhs, wa2_ref[...])
    w_log = -jax.nn.softplus(-(w0_ref[...] + wa[:, :D_B])) - 0.5
    lw_out[...] = -jnp.exp(w_log)
    a = _sigmoid(a0_ref[...] + wa[:, D_B:])
    g_out[...] = _dot(_sigmoid(dg).astype(bf16), g2_ref[...]).astype(bf16)
    kkr = k * kk_ref[...]
    kk = kkr * lax.rsqrt(_head_sum(kkr * kkr, bd_ref) + 1e-12)
    k2 = k * (1.0 + (a - 1.0) * ka_ref[...])
    if has_v:
        v = v + (vf_ref[...].astype(f32) - v) * vg_ref[...].astype(f32)
    r_out[...] = r.astype(bf16)
    k_out[...] = k2.astype(bf16)
    v_out[...] = v.astype(bf16)
    na_out[...] = (-kk).astype(bf16)
    b_out[...] = (kk * a).astype(bf16)
    rkv_out[...] = (_head_sum(r * k2 * rk_ref[...], bd_ref) * v).astype(bf16)


def _prep(ps, n_long, first_long, first_short, period, wts, vextra):
    n = ps.shape[0]
    tm = PREP_TM
    has_v = vextra is not None
    n_long_tiles = n_long // tm
    row = lambda w: pl.BlockSpec((tm, w), lambda i: (i, 0))
    in_specs = [row(N_SHIFT),
                pl.BlockSpec((8, N_SHIFT), lambda i: (jnp.maximum(i * (tm // 8) - 1, 0), 0)),
                _const_spec(first_long.shape),
                pl.BlockSpec((tm, N_SHIFT), lambda i: (jnp.maximum(i - n_long_tiles, 0), 0))]
    args = [ps, ps, first_long, first_short]
    in_specs += [_const_spec(w.shape) for w in wts]
    args += list(wts)
    if has_v:
        in_specs += [row(D_B), row(D_B)]
        args += list(vextra)
    vmem = 2 * tm * 4 * (3 * N_SHIFT + (N_PREP_OUT + 2) * D_B) + 16 * tm * D_B * 4 + 4 * MIB
    return pl.pallas_call(
        functools.partial(_prep_body, n_long_tiles, n_long_tiles // first_long.shape[0], period, has_v),
        grid=(n // tm,), in_specs=in_specs,
        out_specs=[row(D_B)] * N_PREP_OUT,
        out_shape=[jax.ShapeDtypeStruct((n, D_B), f32 if j == 1 else bf16) for j in range(N_PREP_OUT)],
        compiler_params=_params(vmem, 1), name="rwkv_prep",
    )(*args)


SCAN_ROWS = 64


def _scan_body(n_seq, slen, has_s0, n_alias, r_ref, lw_ref, k_ref, v_ref, na_ref, b_ref, tri_ref,
               tot_ref, *rest):
    if has_s0:
        s0_ref, rest = rest[0], rest[1:]
    y_ref, sl_ref, s_scr = rest[n_alias:]
    c = pl.program_id(1)

    @pl.when(c == 0)
    def _():
        if has_s0:
            zero = jnp.zeros((HEAD_B, HEAD_B), f32)

            def pack(g, carry):
                for p in range(N_PAIR):
                    top = jnp.concatenate([s0_ref[g, 2 * p], zero], axis=1)
                    bot = jnp.concatenate([zero, s0_ref[g, 2 * p + 1]], axis=1)
                    s_scr[p, pl.ds(pl.multiple_of(g * PAIR, PAIR), PAIR), :] = jnp.concatenate([top, bot], axis=0)
                return carry

            lax.fori_loop(0, n_seq, pack, 0)
        else:
            s_scr[...] = jnp.zeros(s_scr.shape, f32)

    lw = lw_ref[...]
    cum = jnp.dot(tri_ref[...], lw, precision=lax.Precision.HIGHEST, preferred_element_type=f32)
    tot = jnp.dot(tot_ref[...], lw, precision=lax.Precision.HIGHEST, preferred_element_type=f32)
    g_inv = jnp.exp(-cum)
    g_tail = jnp.exp(tot - cum)
    kk, bb = k_ref[...].astype(f32), b_ref[...].astype(f32)
    at = na_ref[...].astype(f32) * jnp.exp(cum - lw)
    rt = r_ref[...].astype(f32) * jnp.exp(cum)
    bt = bb * g_inv
    kt = kk * g_inv
    bh = bb * g_tail
    kh = kk * g_tail
    g_last = jnp.exp(tot)
    vv = v_ref[...]

    n_st = 2 * SCAN_ROWS
    def causal(width, inclusive):
        ri = lax.broadcasted_iota(jnp.int32, (n_st, width), 0)
        ci = lax.broadcasted_iota(jnp.int32, (n_st, width), 1) % n_st
        same = (ri // SCAN_ROWS == ci // SCAN_ROWS) & ((ri % SCAN_ROWS) // slen == (ci % SCAN_ROWS) // slen)
        t_r, t_c = ri % SCAN_ROWS, ci % SCAN_ROWS
        return same & ((t_c <= t_r) if inclusive else (t_c < t_r))

    incl2 = causal(2 * n_st, True)
    strict = causal(n_st, False)
    lane_e = lax.broadcasted_iota(jnp.int32, (1, PAIR), 1) < HEAD_B
    row_seq = ((lax.broadcasted_iota(jnp.int32, (2 * n_st, 1), 0) % SCAN_ROWS) // slen)

    def stack(z):
        return jnp.concatenate([jnp.where(lane_e, z, 0.0), jnp.where(lane_e, 0.0, z)], axis=0).astype(bf16)

    def pick_seq(big):
        if n_seq == 1:
            return big
        acc = jnp.where(row_seq == 0, big[:, :PAIR], 0.0)
        for g in range(1, n_seq):
            acc = acc + jnp.where(row_seq == g, big[:, g * PAIR:(g + 1) * PAIR], 0.0)
        return acc

    def spread_seq(z):
        if n_seq == 1:
            return z
        return jnp.concatenate([jnp.where(row_seq == g, z, jnp.zeros_like(z)) for g in range(n_seq)], axis=1)

    sls = [slice(p * PAIR, (p + 1) * PAIR) for p in range(N_PAIR)]
    pairs = range(N_PAIR)
    ar = [jnp.concatenate([stack(at[:, s]), stack(rt[:, s])], axis=0) for s in sls]
    bk = [jnp.concatenate([bt[:, s], bt[:, s], kt[:, s], kt[:, s]], axis=0).astype(bf16) for s in sls]
    pm = [_dot_nt(ar[p], bk[p]) for p in pairs]
    vb = [stack(vv[:, s]) for s in sls]
    qs = [pick_seq(_dot_nt(ar[p], s_scr[p].astype(bf16))) for p in pairs]
    x = [qs[p][:n_st] + _dot(jnp.where(strict, pm[p][:n_st, n_st:], 0.0).astype(bf16), vb[p]) for p in pairs]
    m = [jnp.where(strict, pm[p][:n_st, :n_st], 0.0).astype(bf16) for p in pairs]
    span = 1
    while span < slen:
        x = [x[p] + _dot(m[p], x[p].astype(bf16)) for p in pairs]
        span *= 2
        if span < slen:
            m = [_dot(m[p], m[p]).astype(bf16) for p in pairs]
    uv = [jnp.concatenate([x[p].astype(bf16), vb[p]], axis=0) for p in pairs]
    for p in pairs:
        yb = qs[p][n_st:] + _dot(jnp.where(incl2, pm[p][n_st:], 0.0).astype(bf16), uv[p])
        y_ref[:, sls[p]] = yb[:SCAN_ROWS] + yb[SCAN_ROWS:]
    for p in pairs:
        bkh = jnp.concatenate([stack(bh[:, sls[p]]), stack(kh[:, sls[p]])], axis=0)
        upd = _dot_tn(spread_seq(uv[p]), bkh)
        for g in range(n_seq):
            rows = slice(g * PAIR, (g + 1) * PAIR)
            s_scr[p, rows, :] = (s_scr[p, rows, :] * g_last[g * slen:g * slen + 1, sls[p]] + upd[rows])

    @pl.when(c == pl.num_programs(1) - 1)
    def _():
        def unpack(g, carry):
            for p in pairs:
                tile = s_scr[p, pl.ds(pl.multiple_of(g * PAIR, PAIR), PAIR), :]
                sl_ref[g, 2 * p] = tile[:HEAD_B, :HEAD_B]
                sl_ref[g, 2 * p + 1] = tile[HEAD_B:, HEAD_B:]
            return carry

        lax.fori_loop(0, n_seq, unpack, 0)


def _scan(ops, y_prev, s0, s_prev, layer, depth, n_seq_total, slen, row0):
    n = ops[0].shape[0]
    long_seq = slen >= SCAN_ROWS
    if long_seq:
        n_seq, clen, n_chunk, n_outer = 1, SCAN_ROWS, slen // SCAN_ROWS, n_seq_total
    else:
        n_seq, clen, n_chunk, n_outer = SCAN_ROWS // slen, slen, 1, n_seq_total * slen // SCAN_ROWS
    blk0 = row0 // SCAN_ROWS
    tok = pl.BlockSpec((SCAN_ROWS, D_B), lambda i, c: (blk0 + i * n_chunk + c, 0))
    st = pl.BlockSpec((None, n_seq, H_B, HEAD_B, HEAD_B), lambda i, c: (layer, i, 0, 0, 0))
    sq = pl.BlockSpec((SCAN_ROWS, SCAN_ROWS), lambda i, c: (0, 0))
    t = jnp.arange(SCAN_ROWS)
    same_seq = t[:, None] // clen == t[None, :] // clen
    tri = (same_seq & (t[None, :] <= t[:, None])).astype(f32)
    in_specs = [tok] * 6 + [sq, sq]
    args = list(ops) + [tri, same_seq.astype(f32)]
    aliases = {}
    if s0 is not None:
        in_specs.append(st)
        args.append(s0)
    for out_idx, prev in enumerate((y_prev, s_prev)):
        if prev is not None:
            aliases[len(args)] = out_idx
            in_specs.append(pl.BlockSpec(memory_space=pl.ANY))
            args.append(prev)
    state_pad = 2 * n_seq * H_B * HEAD_B * PAIR * 4
    vmem = (2 * 7 * SCAN_ROWS * D_B * 4 + 2 * state_pad * (2 if s0 is not None else 1)
            + N_PAIR * n_seq * PAIR * PAIR * 4 + 40 * SCAN_ROWS * D_B * 4 + 8 * n_seq * PAIR * PAIR * 4 * 4
            + 6 * MIB)
    return pl.pallas_call(
        functools.partial(_scan_body, n_seq, clen, s0 is not None, len(aliases)),
        grid=(n_outer, n_chunk), in_specs=in_specs, out_specs=[tok, st],
        out_shape=[jax.ShapeDtypeStruct((n, D_B), f32),
                   jax.ShapeDtypeStruct((depth, n_seq_total, H_B, HEAD_B, HEAD_B), f32)],
        scratch_shapes=[pltpu.VMEM((N_PAIR, n_seq * PAIR, PAIR), f32)],
        input_output_aliases=aliases,
        compiler_params=_params(vmem, 2), name="wkv_scan",
    )(*args)


MIX_TM = 256


def _mix_body(u_ref, va_ref, y_ref, rkv_ref, g_ref, gate_ref, x_ref, sw_ref, sm_ref, sb_ref,
              lng_ref, lnb_ref, bd_ref, wa_ref, wb_ref, wo_ref, o_ref, ya_scr):
    tm = u_ref.shape[0]
    for r0 in range(0, tm, CHUNK):
        for grp in range(H_A):
            cs = slice(grp * GROUP_A, (grp + 1) * GROUP_A)
            w = jnp.where(sm_ref[...] > 0.5, sw_ref[grp], 0.0).astype(bf16)
            mixed = _dot(w, va_ref[r0:r0 + CHUNK, cs].astype(bf16)) + sb_ref[:, cs]
            ya_scr[r0:r0 + CHUNK, cs] = (u_ref[r0:r0 + CHUNK, cs] * mixed).astype(bf16)
    y = y_ref[...]
    mu = _head_sum(y, bd_ref) * (1.0 / HEAD_B)
    d = y - mu
    var = _head_sum(d * d, bd_ref) * (1.0 / HEAD_B)
    yn = d * lax.rsqrt(var + GN_EPS) * lng_ref[...] + lnb_ref[...] + rkv_ref[...]
    yb = (yn * g_ref[...]).astype(bf16)
    merged = (gate_ref[:, :D_MODEL] * _dot(ya_scr[...], wa_ref[...])
              + gate_ref[:, D_MODEL:] * _dot(yb, wb_ref[...]))
    o_ref[...] = x_ref[...] + _dot(merged.astype(bf16), wo_ref[...])


def _mix_out(u, va, y, rkv, g, gates, x, sgu_w2, sgu_m2, sgu_b2, n_long, lng, lnb, bd, wa, wb, wo):
    n = x.shape[0]
    tm = MIX_TM
    row = lambda w: pl.BlockSpec((tm, w), lambda i: (i, 0))
    kind = lambda i: jnp.where(i * tm >= n_long, 1, 0)
    in_specs = [row(D_A), row(D_A), row(D_B), row(D_B), row(D_B), row(2 * D_MODEL), row(D_MODEL),
                pl.BlockSpec((None, H_A, CHUNK, CHUNK), lambda i: (kind(i), 0, 0, 0)),
                pl.BlockSpec((None, CHUNK, CHUNK), lambda i: (kind(i), 0, 0)),
                pl.BlockSpec((None, CHUNK, D_A), lambda i: (kind(i), 0, 0)),
                _const_spec((1, D_B)), _const_spec((1, D_B)), _const_spec(bd.shape),
                _const_spec(wa.shape), _const_spec(wb.shape), _const_spec(wo.shape)]
    vmem = 2 * tm * 4 * 9 * D_MODEL + 3 * 2 * D_MODEL * D_MODEL + 10 * tm * D_MODEL * 4 + 6 * MIB
    return pl.pallas_call(
        _mix_body, grid=(n // tm,), in_specs=in_specs, out_specs=row(D_MODEL),
        out_shape=jax.ShapeDtypeStruct((n, D_MODEL), f32),
        scratch_shapes=[pltpu.VMEM((tm, D_A), bf16)],
        compiler_params=_params(vmem, 1), name="mix_out",
    )(u, va, y, rkv, g, gates, x, sgu_w2, sgu_m2, sgu_b2, lng, lnb, bd, wa, wb, wo)


FFN_TM = 512
FFN_CW = 256


def _ffn_body(x_ref, g_ref, w1_ref, w3_ref, w2_ref, o_ref, h_scr):
    x = x_ref[...]
    h = x * lax.rsqrt(jnp.mean(x * x, axis=-1, keepdims=True) + RMS_EPS) * g_ref[...]
    h_scr[...] = h.astype(bf16)
    o_ref[...] = x
    d_ff = w1_ref.shape[1]
    for c in range(0, d_ff, FFN_CW):
        a = _dot(h_scr[...], w1_ref[:, c:c + FFN_CW])
        b = _dot(h_scr[...], w3_ref[:, c:c + FFN_CW])
        t = (a * _sigmoid(a) * b).astype(bf16)
        o_ref[...] += _dot(t, w2_ref[c:c + FFN_CW, :])


def _ffn_dense(x, g, w1, w3, w2):
    n = x.shape[0]
    tm = FFN_TM
    row = pl.BlockSpec((tm, D_MODEL), lambda i: (i, 0))
    vmem = 3 * 2 * D_MODEL * w1.shape[1] + 4 * tm * D_MODEL * 4 + 8 * tm * D_MODEL * 4 + 4 * MIB
    return pl.pallas_call(
        _ffn_body, grid=(n // tm,),
        in_specs=[row, _const_spec((1, D_MODEL)), _const_spec(w1.shape), _const_spec(w3.shape),
                  _const_spec(w2.shape)],
        out_specs=row, out_shape=jax.ShapeDtypeStruct((n, D_MODEL), f32),
        scratch_shapes=[pltpu.VMEM((tm, D_MODEL), bf16)],
        compiler_params=_params(vmem, 1), name="ffn_dense",
    )(x, g, w1, w3, w2)


MOE_TM = 1408
MOE_BLOCK = 256
MOE_TAIL = 128
MOE_FC = 4


def _split3(x):
    hi = x.astype(bf16)
    r1 = x - hi.astype(f32)
    mid = r1.astype(bf16)
    lo = (r1 - mid.astype(f32)).astype(bf16)
    return hi, mid, lo


def _router_body(x_ref, g_ref, wr_ref, hb_ref, gate_t_ref, rank_t_ref, rank_c_ref, cnt_ref):
    x = x_ref[...]
    tm = x.shape[0]
    h = x * lax.rsqrt(jnp.mean(x * x, axis=-1, keepdims=True) + RMS_EPS) * g_ref[...]
    hb_ref[...] = h.astype(bf16)
    h_hi, h_mid, h_lo = _split3(h)
    w_hi, w_mid, w_lo = _split3(wr_ref[...])
    logits = (_dot_nt(w_hi, h_hi) + _dot_nt(w_hi, h_mid) + _dot_nt(w_mid, h_hi)
              + _dot_nt(w_hi, h_lo) + _dot_nt(w_mid, h_mid) + _dot_nt(w_lo, h_hi))
    eid = lax.broadcasted_iota(jnp.int32, logits.shape, 0)
    m1 = jnp.max(logits, axis=0, keepdims=True)
    i1 = jnp.min(jnp.where(logits == m1, eid, N_EXPERTS), axis=0, keepdims=True)
    sel1 = eid == i1
    rest = jnp.where(sel1, -jnp.inf, logits)
    m2 = jnp.max(rest, axis=0, keepdims=True)
    i2 = jnp.min(jnp.where(rest == m2, eid, N_EXPERTS), axis=0, keepdims=True)
    sel2 = eid == i2
    e2 = jnp.exp(m2 - m1)
    den = 1.0 + e2
    gate_t = jnp.where(sel1, 1.0 / den, 0.0) + jnp.where(sel2, e2 / den, 0.0)
    sel = jnp.where(sel1 | sel2, 1.0, 0.0)
    s_idx = lax.broadcasted_iota(jnp.int32, (tm, tm), 0)
    t_idx = lax.broadcasted_iota(jnp.int32, (tm, tm), 1)
    before = jnp.where(s_idx < t_idx, 1.0, 0.0).astype(bf16)
    rank = _dot(sel.astype(bf16), before)
    rank_t = jnp.where(sel > 0.5, rank, -1.0)
    gate_t_ref[...] = gate_t
    rank_t_ref[...] = rank_t
    rank_c_ref[...] = rank_t.T
    cnt = jnp.sum(sel, axis=1, keepdims=True)
    cnt_ref[...] = jnp.broadcast_to(cnt, cnt_ref.shape).astype(jnp.int32)


def _router(x, g, wr_t):
    n = x.shape[0]
    tm = MOE_TM
    nt = n // tm
    vmem = 2 * tm * D_MODEL * 6 + 8 * tm * D_MODEL * 4 + 3 * tm * tm * 4 + 8 * MIB
    return pl.pallas_call(
        _router_body, grid=(nt,),
        in_specs=[pl.BlockSpec((tm, D_MODEL), lambda i: (i, 0)), _const_spec((1, D_MODEL)),
                  _const_spec(wr_t.shape)],
        out_specs=[pl.BlockSpec((tm, D_MODEL), lambda i: (i, 0)),
                   pl.BlockSpec((N_EXPERTS, tm), lambda i: (0, i)),
                   pl.BlockSpec((N_EXPERTS, tm), lambda i: (0, i)),
                   pl.BlockSpec((tm, N_EXPERTS), lambda i: (i, 0)),
                   pl.BlockSpec((None, N_EXPERTS, 128), lambda i: (i, 0, 0))],
        out_shape=[jax.ShapeDtypeStruct((n, D_MODEL), bf16),
                   jax.ShapeDtypeStruct((N_EXPERTS, n), f32), jax.ShapeDtypeStruct((N_EXPERTS, n), f32),
                   jax.ShapeDtypeStruct((n, N_EXPERTS), f32),
                   jax.ShapeDtypeStruct((nt, N_EXPERTS, 128), jnp.int32)],
        compiler_params=_params(vmem, 1), name="moe_router",
    )(x, g, wr_t)


def _moe_body(cnt_ref, hb_ref, rank_t_ref, gate_t_ref, rank_c_ref, w1_ref, w3_ref, w2_ref,
              o_ref, xg_scr, acc_scr):
    i, e, c = pl.program_id(0), pl.program_id(1), pl.program_id(2)
    tm = hb_ref.shape[0]
    cnt = cnt_ref[i * N_EXPERTS + e]
    n_full = cnt // MOE_BLOCK
    rem = cnt - n_full * MOE_BLOCK
    n_big = n_full + jnp.where(rem > MOE_TAIL, 1, 0)
    short_tail = (rem > 0) & (rem <= MOE_TAIL)

    def for_blocks(fn):
        def body(j, carry):
            fn(pl.multiple_of(j * MOE_BLOCK, MOE_BLOCK), MOE_BLOCK)
            return carry

        lax.fori_loop(0, n_big, body, 0)

        @pl.when(short_tail)
        def _():
            fn(pl.multiple_of(n_full * MOE_BLOCK, MOE_TAIL), MOE_TAIL)

    def slots(r0, rows, axis):
        shape = (rows, 1) if axis == 0 else (1, rows)
        return (lax.broadcasted_iota(jnp.int32, shape, axis) + r0).astype(f32)

    @pl.when((e == 0) & (c == 0))
    def _():
        o_ref[...] = jnp.zeros(o_ref.shape, f32)

    @pl.when(c == 0)
    def _():
        rank_row = rank_t_ref[pl.ds(e, 1), :]

        def gather(r0, rows):
            onehot = jnp.where(rank_row == slots(r0, rows, 0), 1.0, 0.0).astype(bf16)
            xg_scr[pl.ds(r0, rows), :] = _dot(onehot, hb_ref[...]).astype(bf16)
            acc_scr[pl.ds(r0, rows), :] = jnp.zeros((rows, D_MODEL), f32)

        for_blocks(gather)

    def expert(r0, rows):
        xb = xg_scr[pl.ds(r0, rows), :]
        a = _dot(xb, w1_ref[...])
        b = _dot(xb, w3_ref[...])
        t = (a * _sigmoid(a) * b).astype(bf16)
        acc_scr[pl.ds(r0, rows), :] += _dot(t, w2_ref[...])

    for_blocks(expert)

    @pl.when(c == pl.num_programs(2) - 1)
    def _():
        lane8 = lax.broadcasted_iota(jnp.int32, (tm, N_EXPERTS), 1)
        rank_col = jnp.sum(jnp.where(lane8 == e, rank_c_ref[...], 0.0), axis=1, keepdims=True)
        rank_row = rank_t_ref[pl.ds(e, 1), :]
        gate_row = gate_t_ref[pl.ds(e, 1), :]

        def scatter(r0, rows):
            gate_blk = jnp.sum(jnp.where(rank_row == slots(r0, rows, 0), gate_row, 0.0), axis=1, keepdims=True)
            onehot = jnp.where(rank_col == slots(r0, rows, 1), 1.0, 0.0).astype(bf16)
            scaled = (acc_scr[pl.ds(r0, rows), :] * gate_blk).astype(bf16)
            o_ref[...] += _dot(onehot, scaled)

        for_blocks(scatter)


def _moe(cnt, hb, rank_t, gate_t, rank_c, w1, w3, w2):
    n = hb.shape[0]
    tm = MOE_TM
    d_ff = w1.shape[2]
    fw = d_ff // MOE_FC
    tile = lambda i, e, c, cnt: (i, 0)
    once = dict(pipeline_mode=pl.Buffered(1))
    grid_spec = pltpu.PrefetchScalarGridSpec(
        num_scalar_prefetch=1, grid=(n // tm, N_EXPERTS, MOE_FC),
        in_specs=[pl.BlockSpec((tm, D_MODEL), tile, **once),
                  pl.BlockSpec((N_EXPERTS, tm), lambda i, e, c, cnt: (0, i), **once),
                  pl.BlockSpec((N_EXPERTS, tm), lambda i, e, c, cnt: (0, i), **once),
                  pl.BlockSpec((tm, N_EXPERTS), tile, **once),
                  pl.BlockSpec((None, D_MODEL, fw), lambda i, e, c, cnt: (e, 0, c)),
                  pl.BlockSpec((None, D_MODEL, fw), lambda i, e, c, cnt: (e, 0, c)),
                  pl.BlockSpec((None, fw, D_MODEL), lambda i, e, c, cnt: (e, c, 0))],
        out_specs=pl.BlockSpec((tm, D_MODEL), tile),
        scratch_shapes=[pltpu.VMEM((tm, D_MODEL), bf16), pltpu.VMEM((tm, D_MODEL), f32)])
    vmem = (tm * D_MODEL * (2 + 2 * 4) + tm * D_MODEL * 6 + 2 * 3 * D_MODEL * fw * 2
            + 3 * tm * 128 * 4 + 3 * tm * D_MODEL * 4 + 4 * MIB)
    return pl.pallas_call(
        _moe_body, grid_spec=grid_spec, out_shape=jax.ShapeDtypeStruct((n, D_MODEL), f32),
        compiler_params=_params(vmem, 3), name="moe_experts",
    )(cnt, hb, rank_t, gate_t, rank_c, w1, w3, w2)


NORM_TM = 512


def _norm_body(n_long_tiles, x_ref, y_ref, g_ref, o_long_ref, o_short_ref):
    x = x_ref[...] + y_ref[...]
    val = x * lax.rsqrt(jnp.mean(x * x, axis=-1, keepdims=True) + RMS_EPS) * g_ref[...]
    i = pl.program_id(0)

    @pl.when(i < n_long_tiles)
    def _():
        o_long_ref[...] = val

    @pl.when(i >= n_long_tiles)
    def _():
        o_short_ref[...] = val


def _add_norm(x, y, g, n_long):
    n = x.shape[0]
    tm = NORM_TM
    nl = n_long // tm
    row = pl.BlockSpec((tm, D_MODEL), lambda i: (i, 0))
    return pl.pallas_call(
        functools.partial(_norm_body, nl), grid=(n // tm,),
        in_specs=[row, row, _const_spec((1, D_MODEL))],
        out_specs=[pl.BlockSpec((tm, D_MODEL), lambda i: (jnp.minimum(i, nl - 1), 0)),
                   pl.BlockSpec((tm, D_MODEL), lambda i: (jnp.maximum(i - nl, 0), 0))],
        out_shape=[jax.ShapeDtypeStruct((n_long, D_MODEL), f32),
                   jax.ShapeDtypeStruct((n - n_long, D_MODEL), f32)],
        compiler_params=_params(12 * tm * D_MODEL * 4, 1), name="final_norm",
    )(x, y, g)


def kernel(x_prompt, x_sample, state_wkv, state_shift, norm_mix_g, w_in, shift_mu, sgu_ln_g, sgu_ln_b, sgu_w, sgu_b, rwkv_w0, rwkv_w2, rwkv_a0, rwkv_a2, rwkv_g2, rwkv_v0, rwkv_v1, rwkv_v2, rwkv_k_k, rwkv_k_a, rwkv_r_k, rwkv_ln_g, rwkv_ln_b, w_branch_a, w_branch_b, w_out, norm_ffn_g, ffn_w1, ffn_w3, ffn_w2, moe_router, moe_w1, moe_w3, moe_w2, norm_final_g):
    bp, tp, d = x_prompt.shape
    bs, ts, _ = x_sample.shape
    depth = w_in.shape[0]
    n_p, n_s = bp * tp, bs * ts
    x = jnp.concatenate([x_prompt.reshape(n_p, d), x_sample.reshape(n_s, d)], axis=0)
    row = lambda a: a.reshape(1, -1)

    hid = jnp.arange(4 * HEAD_B) // HEAD_B
    bd = (hid[:, None] == hid[None, :]).astype(bf16)
    tpos = jnp.arange(CHUNK)
    mask_long = (tpos[None, :] <= tpos[:, None]).astype(f32)
    mask_short = ((tpos[None, :] // ts == tpos[:, None] // ts) & (tpos[None, :] <= tpos[:, None])).astype(f32)
    sgu_mask = jnp.stack([mask_long, mask_short])
    zero_shift = jnp.zeros((bp, N_SHIFT), f32)

    shift_p, shift_s, chunk_v = [], [], []
    wkv_p = wkv_s = v_first = None
    for l in range(depth):
        vparams = None
        if l > 0:
            vparams = (rwkv_v1[l - 1].astype(bf16), rwkv_v2[l - 1].astype(bf16), row(rwkv_v0[l - 1]))
        outs = _in_proj(x, n_p, row(norm_mix_g[l]), w_in[l].astype(bf16), row(sgu_ln_g[l]), row(sgu_ln_b[l]),
                        vparams)
        u, va, va_s, ps, tails, gates = outs[:6]
        tiles_per_seq = tp // IN_TM
        shift_p.append(tails[tiles_per_seq - 1:bp * tiles_per_seq:tiles_per_seq, 7])
        shift_s.append(ps[n_p + ts - 1::ts])
        chunk_v.append(va_s.reshape(bs, ts, D_A))

        zpad = jnp.zeros((LORA_W, D_B), f32)
        wa2 = jnp.concatenate([jnp.concatenate([rwkv_w2[l], zpad], axis=1),
                               jnp.concatenate([zpad, rwkv_a2[l]], axis=1)], axis=0).astype(bf16)
        wts = (row(shift_mu[l]), wa2, row(rwkv_w0[l]), row(rwkv_a0[l]), rwkv_g2[l].astype(bf16),
               row(rwkv_k_k[l]), row(rwkv_k_a[l]), row(rwkv_r_k[l]), bd)
        first_s = jnp.pad(state_shift[l][:, None, :], ((0, 0), (0, ts - 1), (0, 0))).reshape(n_s, N_SHIFT)
        pre = _prep(ps, n_p, zero_shift, first_s, ts, wts, (outs[6], v_first) if l > 0 else None)
        if l == 0:
            v_first = pre[3]
        g, rkv = pre[6], pre[7]
        y, wkv_p = _scan(pre[:6], None, None, wkv_p, l, depth, bp, tp, 0)
        y, wkv_s = _scan(pre[:6], y, state_wkv, wkv_s, l, depth, bs, ts, n_p)

        w_short = jnp.tile(sgu_w[l][:, :ts, :ts], (1, CHUNK // ts, CHUNK // ts))
        sgu_w2 = jnp.stack([sgu_w[l], w_short])
        b_long = jnp.repeat(sgu_b[l].T, GROUP_A, axis=1)
        b_short = jnp.tile(b_long[:ts], (CHUNK // ts, 1))
        sgu_b2 = jnp.stack([b_long, b_short])
        x = _mix_out(u, va, y, rkv, g, gates, x, sgu_w2, sgu_mask, sgu_b2, n_p,
                     row(rwkv_ln_g[l]), row(rwkv_ln_b[l]), bd,
                     w_branch_a[l].astype(bf16), w_branch_b[l].astype(bf16), w_out[l].astype(bf16))

        j = l // 2
        if l % 2 == 0:
            x = _ffn_dense(x, row(norm_ffn_g[l]), ffn_w1[j].astype(bf16), ffn_w3[j].astype(bf16),
                           ffn_w2[j].astype(bf16))
            delta = None
        else:
            hb, gate_t, rank_t, rank_c, cnt = _router(x, row(norm_ffn_g[l]), moe_router[j].T)
            delta = _moe(cnt[:, :, 0].reshape(-1), hb, rank_t, gate_t, rank_c,
                         moe_w1[j].astype(bf16), moe_w3[j].astype(bf16), moe_w2[j].astype(bf16))
            if l < depth - 1:
                x = x + delta
                delta = None

    yn_p, yn_s = _add_norm(x, jnp.zeros_like(x) if delta is None else delta, row(norm_final_g), n_p)
    return (yn_p.reshape(bp, tp, d), yn_s.reshape(bs, ts, d),
            wkv_p, jnp.stack(shift_p), wkv_s, jnp.stack(shift_s),
            jnp.stack(chunk_v))
```

```python
import functools

import jax
import jax.numpy as jnp
from jax import lax
from jax.experimental import pallas as pl
from jax.experimental.pallas import tpu as pltpu

f32 = jnp.float32
bf16 = jnp.bfloat16

D_MODEL = 1024
CHUNK = 128
D_A = D_MODEL
GROUP_A = 128
H_A = D_A // GROUP_A
D_B = D_MODEL
HEAD_B = 64
H_B = D_B // HEAD_B
LORA_W = 64
LORA_A = 64
LORA_G = 128
N_SHIFT = 3 * D_B + LORA_W + LORA_A + LORA_G
N_IN = 2 * D_A + N_SHIFT + 2 * D_MODEL
N_EXPERTS = 8
RMS_EPS = 1e-6
LN_EPS = 1e-5
GN_EPS = 64e-5

PAIR = 2 * HEAD_B
N_PAIR = H_B // 2
MIB = 1024 * 1024
VMEM_CAP_V7X = 56 * MIB


def _params(vmem_bytes, n_grid):
    return pltpu.CompilerParams(
        dimension_semantics=("arbitrary",) * n_grid,
        vmem_limit_bytes=int(min(max(vmem_bytes, 16 * MIB), VMEM_CAP_V7X)),
    )


def _const_spec(shape):
    nd = len(shape)
    return pl.BlockSpec(shape, lambda *_: (0,) * nd, pipeline_mode=pl.Buffered(1))


def _gelu(x):
    return 0.5 * x * (1.0 + lax.erf(x * (2.0 ** -0.5)))


def _sigmoid(x):
    return 1.0 / (1.0 + jnp.exp(-x))


def _dot(a, b):
    return jnp.dot(a, b, preferred_element_type=f32)


def _dot_nt(a, b):
    return lax.dot_general(a, b, (((1,), (1,)), ((), ())), preferred_element_type=f32)


def _dot_tn(a, b):
    return lax.dot_general(a, b, (((0,), (0,)), ((), ())), preferred_element_type=f32)


def _head_sum(x, bd_ref):
    w = bd_ref.shape[0]
    cols = [_dot(x[:, c:c + w].astype(bf16), bd_ref[...]) for c in range(0, x.shape[1], w)]
    return jnp.concatenate(cols, axis=1)


IN_TM = 512
IN_CW = 512


def _in_proj_body(n_long_tiles, has_v, x_ref, g_ref, w_ref, lng_ref, lnb_ref, *rest):
    if has_v:
        v1_ref, v2_ref, v0_ref, u_ref, va_ref, vs_ref, ps_ref, tail_ref, gate_ref, vg_ref, h_scr, t_scr = rest
    else:
        u_ref, va_ref, vs_ref, ps_ref, tail_ref, gate_ref, h_scr, t_scr = rest
    x = x_ref[...]
    tm = x.shape[0]
    h = x * lax.rsqrt(jnp.mean(x * x, axis=-1, keepdims=True) + RMS_EPS) * g_ref[...]
    h_scr[...] = h.astype(bf16)

    def mm(c0, c1):
        return _dot(h_scr[...], w_ref[:, c0:c1])

    for c in range(0, D_A, IN_CW):
        u_ref[:, c:c + IN_CW] = _gelu(mm(c, c + IN_CW)).astype(bf16)
    for c in range(0, D_A, IN_CW):
        t_scr[:, c:c + IN_CW] = _gelu(mm(D_A + c, D_A + c + IN_CW))
    t = t_scr[...]
    mu = jnp.mean(t, axis=-1, keepdims=True)
    d = t - mu
    var = jnp.mean(d * d, axis=-1, keepdims=True)
    va = d * lax.rsqrt(var + LN_EPS) * lng_ref[...] + lnb_ref[...]
    va_ref[...] = va.astype(bf16)

    @pl.when(pl.program_id(0) >= n_long_tiles)
    def _():
        vs_ref[...] = va

    for c in range(0, N_SHIFT, IN_CW):
        c1 = min(c + IN_CW, N_SHIFT)
        ps_ref[:, c:c1] = mm(2 * D_A + c, 2 * D_A + c1)
    tail_ref[...] = ps_ref[tm - 8:tm, :]
    base = 2 * D_A + N_SHIFT
    for c in range(0, 2 * D_MODEL, IN_CW):
        gate_ref[:, c:c + IN_CW] = _sigmoid(mm(base + c, base + c + IN_CW)).astype(bf16)
    if has_v:
        lv = _dot(h_scr[...], v1_ref[...])
        vg_ref[...] = _sigmoid(v0_ref[...] + _dot(lv.astype(bf16), v2_ref[...])).astype(bf16)


def _in_proj(x, n_long, g, w_in, lng, lnb, vparams):
    n = x.shape[0]
    tm = IN_TM
    nl = n_long // tm
    has_v = vparams is not None
    row = lambda w: pl.BlockSpec((tm, w), lambda i: (i, 0))
    in_specs = [row(D_MODEL), _const_spec((1, D_MODEL)), _const_spec((D_MODEL, N_IN)),
                _const_spec((1, D_A)), _const_spec((1, D_A))]
    args = [x, g, w_in, lng, lnb]
    out_shape = [jax.ShapeDtypeStruct((n, D_A), bf16), jax.ShapeDtypeStruct((n, D_A), bf16),
                 jax.ShapeDtypeStruct((n - n_long, D_A), f32),
                 jax.ShapeDtypeStruct((n, N_SHIFT), f32), jax.ShapeDtypeStruct((n // tm, 8, N_SHIFT), f32),
                 jax.ShapeDtypeStruct((n, 2 * D_MODEL), bf16)]
    out_specs = [row(D_A), row(D_A), pl.BlockSpec((tm, D_A), lambda i: (jnp.maximum(i - nl, 0), 0)),
                 row(N_SHIFT), pl.BlockSpec((None, 8, N_SHIFT), lambda i: (i, 0, 0)), row(2 * D_MODEL)]
    if has_v:
        v1, v2, v0 = vparams
        in_specs += [_const_spec(v1.shape), _const_spec(v2.shape), _const_spec((1, D_B))]
        args += [v1, v2, v0]
        out_shape.append(jax.ShapeDtypeStruct((n, D_B), bf16))
        out_specs.append(row(D_B))
    out_bytes = 2 * (2 * D_A + 2 * D_MODEL + (D_B if has_v else 0)) + 4 * (N_SHIFT + D_A)
    vmem = 2 * D_MODEL * N_IN + 2 * tm * (4 * D_MODEL + out_bytes) + 8 * tm * D_MODEL * 4 + 4 * MIB
    return pl.pallas_call(
        functools.partial(_in_proj_body, nl, has_v),
        grid=(n // tm,), in_specs=in_specs, out_specs=out_specs, out_shape=out_shape,
        scratch_shapes=[pltpu.VMEM((tm, D_MODEL), bf16), pltpu.VMEM((tm, D_A), f32)],
        compiler_params=_params(vmem, 1), name="in_proj",
    )(*args)


N_MIX_W = 11
SCAN_ROWS = 64


N_OPS_BF, N_OPS_F32 = 7, 3


def _scan_outer(*args):
    for parity in (0, 1):
        @pl.when(pl.program_id(0) % 2 == parity)
        def _(parity=parity):
            _scan_body(parity, *args)


def _scan_body(parity, n_seq, slen, long_seq, n_chunk, n_blocks, has_v, has_s0, emit_v, n_alias,
               ps_ref, prevblk_ref, first_ref, *rest):
    if has_v:
        (vg_ref, vf_ref), rest = rest[:2], rest[2:]
    (mu_ref, wa2_ref, w0_ref, a0_ref, g2_ref, kk_ref, ka_ref, rk_ref, bd_ref, lng_ref, lnb_ref), rest = (
        rest[:N_MIX_W], rest[N_MIX_W:])
    (tri_ref, tot_ref), rest = rest[:2], rest[2:]
    if has_s0:
        s0_ref, rest = rest[0], rest[1:]
    rest = rest[n_alias:]
    if emit_v:
        yb_ref, vo_ref, sl_ref, s_scr, opb_scr, opf_scr = rest
    else:
        yb_ref, sl_ref, s_scr, opb_scr, opf_scr = rest
    j = pl.program_id(0)
    blk_prep = jnp.minimum(j, n_blocks - 1)
    blk_run = jnp.maximum(j - 1, 0)
    c = blk_run % n_chunk

    if parity == 0:
        @pl.when(j == 0)
        def _():
            opb_scr[1] = jnp.zeros(opb_scr.shape[1:], bf16)
            opf_scr[1] = jnp.zeros(opf_scr.shape[1:], f32)

    @pl.when(c == 0)
    def _():
        if has_s0:
            zero = jnp.zeros((HEAD_B, HEAD_B), f32)

            def pack(g, carry):
                for p in range(N_PAIR):
                    top = jnp.concatenate([s0_ref[g, 2 * p], zero], axis=1)
                    bot = jnp.concatenate([zero, s0_ref[g, 2 * p + 1]], axis=1)
                    s_scr[p, pl.ds(pl.multiple_of(g * PAIR, PAIR), PAIR), :] = jnp.concatenate([top, bot], axis=0)
                return carry

            lax.fori_loop(0, n_seq, pack, 0)
        else:
            s_scr[...] = jnp.zeros(s_scr.shape, f32)

    ps = ps_ref[...]
    row = lax.broadcasted_iota(jnp.int32, (SCAN_ROWS, 1), 0)
    if long_seq:
        init = jnp.where(blk_prep % n_chunk == 0, first_ref[pl.ds(blk_prep // n_chunk, 1), :],
                         prevblk_ref[7:8, :])
        starts = row == 0
    else:
        init = first_ref[...]
        starts = row % slen == 0
    prev = jnp.where(starts, init, pltpu.roll(ps, 1, 0))
    xs = ps + mu_ref[...] * (prev - ps)
    rr = xs[:, 0:D_B]
    k_raw = xs[:, D_B:2 * D_B]
    v_new = xs[:, 2 * D_B:3 * D_B]
    dwa = xs[:, 3 * D_B:3 * D_B + LORA_W + LORA_A]
    dg = xs[:, 3 * D_B + LORA_W + LORA_A:N_SHIFT]
    lane = lax.broadcasted_iota(jnp.int32, dwa.shape, 1)
    lhs = jnp.where(lane < LORA_W, jnp.tanh(dwa), dwa).astype(bf16)
    wa = _dot(lhs, wa2_ref[...])
    g_new = _dot(_sigmoid(dg).astype(bf16), g2_ref[...])
    kkr = k_raw * kk_ref[...]
    kk_norm2 = _head_sum(kkr * kkr, bd_ref)

    at, rt, bt, kt, bh, kh, vv = [opb_scr[1 - parity, k] for k in range(N_OPS_BF)]
    g_last, gg, bonus = [opf_scr[1 - parity, k] for k in range(N_OPS_F32)]

    n_st = 2 * SCAN_ROWS
    def causal(width, inclusive):
        ri = lax.broadcasted_iota(jnp.int32, (n_st, width), 0)
        ci = lax.broadcasted_iota(jnp.int32, (n_st, width), 1) % n_st
        same = (ri // SCAN_ROWS == ci // SCAN_ROWS) & ((ri % SCAN_ROWS) // slen == (ci % SCAN_ROWS) // slen)
        t_r, t_c = ri % SCAN_ROWS, ci % SCAN_ROWS
        return same & ((t_c <= t_r) if inclusive else (t_c < t_r))

    incl2 = causal(2 * n_st, True)
    strict = causal(n_st, False)
    lane_e = lax.broadcasted_iota(jnp.int32, (1, PAIR), 1) < HEAD_B
    row_seq = ((lax.broadcasted_iota(jnp.int32, (2 * n_st, 1), 0) % SCAN_ROWS) // slen)

    def stack(z):
        return jnp.concatenate([jnp.where(lane_e, z, 0.0), jnp.where(lane_e, 0.0, z)], axis=0).astype(bf16)

    def pick_seq(big):
        if n_seq == 1:
            return big
        acc = jnp.where(row_seq == 0, big[:, :PAIR], 0.0)
        for g in range(1, n_seq):
            acc = acc + jnp.where(row_seq == g, big[:, g * PAIR:(g + 1) * PAIR], 0.0)
        return acc

    def spread_seq(z):
        if n_seq == 1:
            return z
        return jnp.concatenate([jnp.where(row_seq == g, z, jnp.zeros_like(z)) for g in range(n_seq)], axis=1)

    sls = [slice(p * PAIR, (p + 1) * PAIR) for p in range(N_PAIR)]
    pairs = range(N_PAIR)
    ar = [jnp.concatenate([stack(at[:, s]), stack(rt[:, s])], axis=0) for s in sls]
    bk = [jnp.concatenate([bt[:, s], bt[:, s], kt[:, s], kt[:, s]], axis=0).astype(bf16) for s in sls]
    pm = [_dot_nt(ar[p], bk[p]) for p in pairs]
    vb = [stack(vv[:, s]) for s in sls]
    qs = [pick_seq(_dot_nt(ar[p], s_scr[p].astype(bf16))) for p in pairs]
    x = [qs[p][:n_st] + _dot(jnp.where(strict, pm[p][:n_st, n_st:], 0.0).astype(bf16), vb[p]) for p in pairs]
    m = [jnp.where(strict, pm[p][:n_st, :n_st], 0.0).astype(bf16) for p in pairs]

    w_log = -jax.nn.softplus(-(w0_ref[...] + wa[:, :D_B])) - 0.5
    lw = -jnp.exp(w_log)
    a_new = _sigmoid(a0_ref[...] + wa[:, D_B:])
    k_new = k_raw * (1.0 + (a_new - 1.0) * ka_ref[...])
    if has_v:
        v_new = v_new + (vf_ref[...].astype(f32) - v_new) * vg_ref[...].astype(f32)
    if emit_v:
        vo_ref[...] = v_new.astype(bf16)
    cum = jnp.dot(tri_ref[...], lw, precision=lax.Precision.HIGHEST, preferred_element_type=f32)
    tot = jnp.dot(tot_ref[...], lw, precision=lax.Precision.HIGHEST, preferred_element_type=f32)
    rk_sum = _head_sum(rr * k_new * rk_ref[...], bd_ref)

    span = 1
    while span < slen:
        x = [x[p] + _dot(m[p], x[p].astype(bf16)) for p in pairs]
        span *= 2
        if span < slen:
            m = [_dot(m[p], m[p]).astype(bf16) for p in pairs]
    uv = [jnp.concatenate([x[p].astype(bf16), vb[p]], axis=0) for p in pairs]
    ys = []
    for p in pairs:
        yp = qs[p][n_st:] + _dot(jnp.where(incl2, pm[p][n_st:], 0.0).astype(bf16), uv[p])
        ys.append(yp[:SCAN_ROWS] + yp[SCAN_ROWS:])
    y = jnp.concatenate(ys, axis=1)
    mean = _head_sum(y, bd_ref) * (1.0 / HEAD_B)
    d = y - mean
    var = _head_sum(d * d, bd_ref) * (1.0 / HEAD_B)
    yn = d * lax.rsqrt(var + GN_EPS) * lng_ref[...] + lnb_ref[...] + bonus
    yb_ref[...] = (yn * gg).astype(bf16)
    for p in pairs:
        bkh = jnp.concatenate([stack(bh[:, sls[p]]), stack(kh[:, sls[p]])], axis=0)
        upd = _dot_tn(spread_seq(uv[p]), bkh)
        for g in range(n_seq):
            rows = slice(g * PAIR, (g + 1) * PAIR)
            s_scr[p, rows, :] = (s_scr[p, rows, :] * g_last[g * slen:g * slen + 1, sls[p]] + upd[rows])

    kk_new = kkr * lax.rsqrt(kk_norm2 + 1e-12)
    b_new = kk_new * a_new
    g_inv = jnp.exp(-cum)
    g_tail = jnp.exp(tot - cum)
    folded = (-kk_new * jnp.exp(cum - lw), rr * jnp.exp(cum), b_new * g_inv, k_new * g_inv,
              b_new * g_tail, k_new * g_tail, v_new)
    for idx, val in enumerate(folded):
        opb_scr[parity, idx] = val.astype(bf16)
    for idx, val in enumerate((jnp.exp(tot), g_new, rk_sum * v_new)):
        opf_scr[parity, idx] = val

    @pl.when(c == n_chunk - 1)
    def _():
        def unpack(g, carry):
            for p in pairs:
                tile = s_scr[p, pl.ds(pl.multiple_of(g * PAIR, PAIR), PAIR), :]
                sl_ref[g, 2 * p] = tile[:HEAD_B, :HEAD_B]
                sl_ref[g, 2 * p + 1] = tile[HEAD_B:, HEAD_B:]
            return carry

        lax.fori_loop(0, n_seq, unpack, 0)


def _time_mix(ps, first, vextra, wts, s0, prevs, emit_v, layer, depth, n_seq_total, slen, row0):
    n = ps.shape[0]
    long_seq = slen >= SCAN_ROWS
    if long_seq:
        n_seq, clen, n_chunk, n_outer = 1, SCAN_ROWS, slen // SCAN_ROWS, n_seq_total
    else:
        n_seq, clen, n_chunk, n_outer = SCAN_ROWS // slen, slen, 1, n_seq_total * slen // SCAN_ROWS
    blk0 = row0 // SCAN_ROWS
    n_blocks = n_outer * n_chunk
    prep = lambda j: jnp.minimum(j, n_blocks - 1)
    run = lambda j: jnp.maximum(j - 1, 0)
    tok_prep = lambda w: pl.BlockSpec((SCAN_ROWS, w), lambda j: (blk0 + prep(j), 0))
    tok_run = lambda w: pl.BlockSpec((SCAN_ROWS, w), lambda j: (blk0 + run(j), 0))
    st = pl.BlockSpec((None, n_seq, H_B, HEAD_B, HEAD_B), lambda j: (layer, run(j) // n_chunk, 0, 0, 0))
    sq = pl.BlockSpec((SCAN_ROWS, SCAN_ROWS), lambda j: (0, 0))
    t = jnp.arange(SCAN_ROWS)
    same_seq = t[:, None] // clen == t[None, :] // clen
    tri = (same_seq & (t[None, :] <= t[:, None])).astype(f32)
    in_specs = [tok_prep(N_SHIFT),
                pl.BlockSpec((8, N_SHIFT), lambda j: (jnp.maximum((blk0 + prep(j)) * (SCAN_ROWS // 8) - 1, 0), 0)),
                _const_spec(first.shape) if long_seq else pl.BlockSpec((SCAN_ROWS, N_SHIFT), lambda j: (prep(j), 0))]
    args = [ps, ps, first]
    if vextra is not None:
        in_specs += [tok_prep(D_B), tok_prep(D_B)]
        args += list(vextra)
    in_specs += [_const_spec(w.shape) for w in wts] + [sq, sq]
    args += list(wts) + [tri, same_seq.astype(f32)]
    if s0 is not None:
        in_specs.append(st)
        args.append(s0)
    aliases = {}
    for out_idx, prev in enumerate(prevs):
        if prev is not None:
            aliases[len(args)] = out_idx
            in_specs.append(pl.BlockSpec(memory_space=pl.ANY))
            args.append(prev)
    out_specs = [tok_run(D_B)] + ([tok_prep(D_B)] if emit_v else []) + [st]
    out_shape = ([jax.ShapeDtypeStruct((n, D_B), bf16)] * (2 if emit_v else 1)
                 + [jax.ShapeDtypeStruct((depth, n_seq_total, H_B, HEAD_B, HEAD_B), f32)])
    state_pad = 2 * n_seq * H_B * HEAD_B * PAIR * 4
    vmem = (2 * SCAN_ROWS * (2 * N_SHIFT * 4 + 6 * D_B * 2) + 2 * state_pad * (2 if s0 is not None else 1)
            + N_PAIR * n_seq * PAIR * PAIR * 4 + 64 * SCAN_ROWS * D_B * 4 + 8 * n_seq * PAIR * PAIR * 4 * 4
            + 8 * MIB)
    return pl.pallas_call(
        functools.partial(_scan_outer, n_seq, clen, long_seq, n_chunk, n_blocks, vextra is not None,
                          s0 is not None, emit_v, len(aliases)),
        grid=(n_blocks + 1,), in_specs=in_specs, out_specs=out_specs, out_shape=out_shape,
        scratch_shapes=[pltpu.VMEM((N_PAIR, n_seq * PAIR, PAIR), f32),
                        pltpu.VMEM((2, N_OPS_BF, SCAN_ROWS, D_B), bf16),
                        pltpu.VMEM((2, N_OPS_F32, SCAN_ROWS, D_B), f32)],
        input_output_aliases=aliases,
        compiler_params=_params(vmem, 1), name="time_mix",
    )(*args)


MIX_TM = 256


def _mix_body(u_ref, va_ref, yb_ref, gate_ref, x_ref, sw_ref, sm_ref, sb_ref, wa_ref, wb_ref, wo_ref,
              o_ref, ya_scr):
    tm = u_ref.shape[0]
    for r0 in range(0, tm, CHUNK):
        for grp in range(H_A):
            cs = slice(grp * GROUP_A, (grp + 1) * GROUP_A)
            w = jnp.where(sm_ref[...] > 0.5, sw_ref[grp], 0.0).astype(bf16)
            mixed = _dot(w, va_ref[r0:r0 + CHUNK, cs].astype(bf16)) + sb_ref[:, cs]
            ya_scr[r0:r0 + CHUNK, cs] = (u_ref[r0:r0 + CHUNK, cs] * mixed).astype(bf16)
    merged = (gate_ref[:, :D_MODEL] * _dot(ya_scr[...], wa_ref[...])
              + gate_ref[:, D_MODEL:] * _dot(yb_ref[...], wb_ref[...]))
    o_ref[...] = x_ref[...] + _dot(merged.astype(bf16), wo_ref[...])


def _mix_out(u, va, yb, gates, x, sgu_w2, sgu_m2, sgu_b2, n_long, wa, wb, wo):
    n = x.shape[0]
    tm = MIX_TM
    row = lambda w: pl.BlockSpec((tm, w), lambda i: (i, 0))
    kind = lambda i: jnp.where(i * tm >= n_long, 1, 0)
    in_specs = [row(D_A), row(D_A), row(D_B), row(2 * D_MODEL), row(D_MODEL),
                pl.BlockSpec((None, H_A, CHUNK, CHUNK), lambda i: (kind(i), 0, 0, 0)),
                pl.BlockSpec((None, CHUNK, CHUNK), lambda i: (kind(i), 0, 0)),
                pl.BlockSpec((None, CHUNK, D_A), lambda i: (kind(i), 0, 0)),
                _const_spec(wa.shape), _const_spec(wb.shape), _const_spec(wo.shape)]
    vmem = 2 * tm * (2 * 5 + 4 * 2) * D_MODEL + 3 * 2 * D_MODEL * D_MODEL + 10 * tm * D_MODEL * 4 + 6 * MIB
    return pl.pallas_call(
        _mix_body, grid=(n // tm,), in_specs=in_specs, out_specs=row(D_MODEL),
        out_shape=jax.ShapeDtypeStruct((n, D_MODEL), f32),
        scratch_shapes=[pltpu.VMEM((tm, D_A), bf16)],
        compiler_params=_params(vmem, 1), name="mix_out",
    )(u, va, yb, gates, x, sgu_w2, sgu_m2, sgu_b2, wa, wb, wo)


FFN_TM = 512
FFN_CW = 256


def _ffn_body(x_ref, g_ref, w1_ref, w3_ref, w2_ref, o_ref, h_scr):
    x = x_ref[...]
    h = x * lax.rsqrt(jnp.mean(x * x, axis=-1, keepdims=True) + RMS_EPS) * g_ref[...]
    h_scr[...] = h.astype(bf16)
    o_ref[...] = x
    d_ff = w1_ref.shape[1]
    for c in range(0, d_ff, FFN_CW):
        a = _dot(h_scr[...], w1_ref[:, c:c + FFN_CW])
        b = _dot(h_scr[...], w3_ref[:, c:c + FFN_CW])
        t = (a * _sigmoid(a) * b).astype(bf16)
        o_ref[...] += _dot(t, w2_ref[c:c + FFN_CW, :])


def _ffn_dense(x, g, w1, w3, w2):
    n = x.shape[0]
    tm = FFN_TM
    row = pl.BlockSpec((tm, D_MODEL), lambda i: (i, 0))
    vmem = 3 * 2 * D_MODEL * w1.shape[1] + 4 * tm * D_MODEL * 4 + 8 * tm * D_MODEL * 4 + 4 * MIB
    return pl.pallas_call(
        _ffn_body, grid=(n // tm,),
        in_specs=[row, _const_spec((1, D_MODEL)), _const_spec(w1.shape), _const_spec(w3.shape),
                  _const_spec(w2.shape)],
        out_specs=row, out_shape=jax.ShapeDtypeStruct((n, D_MODEL), f32),
        scratch_shapes=[pltpu.VMEM((tm, D_MODEL), bf16)],
        compiler_params=_params(vmem, 1), name="ffn_dense",
    )(x, g, w1, w3, w2)


MOE_TM = 1408
MOE_BLOCK = 256
MOE_TAIL = 128
MOE_ONE_SHOT = 3
MOE_FC = 2


def _split3(x):
    hi = x.astype(bf16)
    r1 = x - hi.astype(f32)
    mid = r1.astype(bf16)
    lo = (r1 - mid.astype(f32)).astype(bf16)
    return hi, mid, lo


def _router_body(x_ref, g_ref, wr_ref, hb_ref, gate_t_ref, rank_t_ref, rank_c_ref, cnt_ref):
    x = x_ref[...]
    tm = x.shape[0]
    h = x * lax.rsqrt(jnp.mean(x * x, axis=-1, keepdims=True) + RMS_EPS) * g_ref[...]
    hb_ref[...] = h.astype(bf16)
    h_hi, h_mid, h_lo = _split3(h)
    w_hi, w_mid, w_lo = _split3(wr_ref[...])
    logits = (_dot_nt(w_hi, h_hi) + _dot_nt(w_hi, h_mid) + _dot_nt(w_mid, h_hi)
              + _dot_nt(w_hi, h_lo) + _dot_nt(w_mid, h_mid) + _dot_nt(w_lo, h_hi))
    eid = lax.broadcasted_iota(jnp.int32, logits.shape, 0)
    m1 = jnp.max(logits, axis=0, keepdims=True)
    i1 = jnp.min(jnp.where(logits == m1, eid, N_EXPERTS), axis=0, keepdims=True)
    sel1 = eid == i1
    rest = jnp.where(sel1, -jnp.inf, logits)
    m2 = jnp.max(rest, axis=0, keepdims=True)
    i2 = jnp.min(jnp.where(rest == m2, eid, N_EXPERTS), axis=0, keepdims=True)
    sel2 = eid == i2
    e2 = jnp.exp(m2 - m1)
    den = 1.0 + e2
    gate_t = jnp.where(sel1, 1.0 / den, 0.0) + jnp.where(sel2, e2 / den, 0.0)
    sel = jnp.where(sel1 | sel2, 1.0, 0.0)
    s_idx = lax.broadcasted_iota(jnp.int32, (tm, tm), 0)
    t_idx = lax.broadcasted_iota(jnp.int32, (tm, tm), 1)
    before = jnp.where(s_idx < t_idx, 1.0, 0.0).astype(bf16)
    rank = _dot(sel.astype(bf16), before)
    rank_t = jnp.where(sel > 0.5, rank, -1.0)
    gate_t_ref[...] = gate_t
    rank_t_ref[...] = rank_t
    rank_c_ref[...] = rank_t.T
    cnt = jnp.sum(sel, axis=1, keepdims=True)
    cnt_ref[...] = jnp.broadcast_to(cnt, cnt_ref.shape).astype(jnp.int32)


def _router(x, g, wr_t):
    n = x.shape[0]
    tm = MOE_TM
    nt = n // tm
    vmem = 2 * tm * D_MODEL * 6 + 8 * tm * D_MODEL * 4 + 3 * tm * tm * 4 + 8 * MIB
    return pl.pallas_call(
        _router_body, grid=(nt,),
        in_specs=[pl.BlockSpec((tm, D_MODEL), lambda i: (i, 0)), _const_spec((1, D_MODEL)),
                  _const_spec(wr_t.shape)],
        out_specs=[pl.BlockSpec((tm, D_MODEL), lambda i: (i, 0)),
                   pl.BlockSpec((N_EXPERTS, tm), lambda i: (0, i)),
                   pl.BlockSpec((N_EXPERTS, tm), lambda i: (0, i)),
                   pl.BlockSpec((tm, N_EXPERTS), lambda i: (i, 0)),
                   pl.BlockSpec((None, N_EXPERTS, 128), lambda i: (i, 0, 0))],
        out_shape=[jax.ShapeDtypeStruct((n, D_MODEL), bf16),
                   jax.ShapeDtypeStruct((N_EXPERTS, n), f32), jax.ShapeDtypeStruct((N_EXPERTS, n), f32),
                   jax.ShapeDtypeStruct((n, N_EXPERTS), f32),
                   jax.ShapeDtypeStruct((nt, N_EXPERTS, 128), jnp.int32)],
        compiler_params=_params(vmem, 1), name="moe_router",
    )(x, g, wr_t)


def _moe_body(cnt_ref, hb_ref, rank_t_ref, gate_t_ref, rank_c_ref, w1_ref, w3_ref, w2_ref,
              o_ref, xg_scr, acc_scr):
    i, e, c = pl.program_id(0), pl.program_id(1), pl.program_id(2)
    tm = hb_ref.shape[0]
    cnt = cnt_ref[i * N_EXPERTS + e]
    n_full = cnt // MOE_BLOCK
    rem = cnt - n_full * MOE_BLOCK
    n_big = n_full + jnp.where(rem > MOE_TAIL, 1, 0)
    short_tail = (rem > 0) & (rem <= MOE_TAIL)

    n_tail = (cnt + MOE_TAIL - 1) // MOE_TAIL

    def for_blocks(fn):
        for k in range(1, MOE_ONE_SHOT + 1):
            @pl.when(n_tail == k)
            def _(k=k):
                fn(0, k * MOE_TAIL)

        @pl.when(n_tail > MOE_ONE_SHOT)
        def _():
            def body(j, carry):
                fn(pl.multiple_of(j * MOE_BLOCK, MOE_BLOCK), MOE_BLOCK)
                return carry

            lax.fori_loop(0, n_big, body, 0)

            @pl.when(short_tail)
            def _():
                fn(pl.multiple_of(n_full * MOE_BLOCK, MOE_TAIL), MOE_TAIL)

    def slots(r0, rows, axis):
        shape = (rows, 1) if axis == 0 else (1, rows)
        return (lax.broadcasted_iota(jnp.int32, shape, axis) + r0).astype(f32)

    @pl.when((e == 0) & (c == 0))
    def _():
        o_ref[...] = jnp.zeros(o_ref.shape, f32)

    @pl.when(c == 0)
    def _():
        rank_row = rank_t_ref[pl.ds(e, 1), :]

        def gather(r0, rows):
            onehot = jnp.where(rank_row == slots(r0, rows, 0), 1.0, 0.0).astype(bf16)
            xg_scr[pl.ds(r0, rows), :] = _dot(onehot, hb_ref[...]).astype(bf16)
            acc_scr[pl.ds(r0, rows), :] = jnp.zeros((rows, D_MODEL), f32)

        for_blocks(gather)

    def expert(r0, rows):
        xb = xg_scr[pl.ds(r0, rows), :]
        a = _dot(xb, w1_ref[...])
        b = _dot(xb, w3_ref[...])
        t = (a * _sigmoid(a) * b).astype(bf16)
        acc_scr[pl.ds(r0, rows), :] += _dot(t, w2_ref[...])

    for_blocks(expert)

    @pl.when(c == pl.num_programs(2) - 1)
    def _():
        lane8 = lax.broadcasted_iota(jnp.int32, (tm, N_EXPERTS), 1)
        rank_col = jnp.sum(jnp.where(lane8 == e, rank_c_ref[...], 0.0), axis=1, keepdims=True)
        rank_row = rank_t_ref[pl.ds(e, 1), :]
        gate_row = gate_t_ref[pl.ds(e, 1), :]

        def scatter(r0, rows):
            gate_blk = jnp.sum(jnp.where(rank_row == slots(r0, rows, 0), gate_row, 0.0), axis=1, keepdims=True)
            onehot = jnp.where(rank_col == slots(r0, rows, 1), 1.0, 0.0).astype(bf16)
            scaled = (acc_scr[pl.ds(r0, rows), :] * gate_blk).astype(bf16)
            o_ref[...] += _dot(onehot, scaled)

        for_blocks(scatter)


def _moe(cnt, hb, rank_t, gate_t, rank_c, w1, w3, w2):
    n = hb.shape[0]
    tm = MOE_TM
    fw = w2.shape[1] // MOE_FC
    tile = lambda i, e, c, cnt: (i, 0)
    once = dict(pipeline_mode=pl.Buffered(1))
    grid_spec = pltpu.PrefetchScalarGridSpec(
        num_scalar_prefetch=1, grid=(n // tm, N_EXPERTS, MOE_FC),
        in_specs=[pl.BlockSpec((tm, D_MODEL), tile, **once),
                  pl.BlockSpec((N_EXPERTS, tm), lambda i, e, c, cnt: (0, i), **once),
                  pl.BlockSpec((N_EXPERTS, tm), lambda i, e, c, cnt: (0, i), **once),
                  pl.BlockSpec((tm, N_EXPERTS), tile, **once),
                  pl.BlockSpec((None, D_MODEL, fw), lambda i, e, c, cnt: (e, 0, c)),
                  pl.BlockSpec((None, D_MODEL, fw), lambda i, e, c, cnt: (e, 0, c)),
                  pl.BlockSpec((None, fw, D_MODEL), lambda i, e, c, cnt: (e, c, 0))],
        out_specs=pl.BlockSpec((tm, D_MODEL), tile),
        scratch_shapes=[pltpu.VMEM((tm, D_MODEL), bf16), pltpu.VMEM((tm, D_MODEL), f32)])
    vmem = (tm * D_MODEL * (2 + 2 * 4) + tm * D_MODEL * 6 + 2 * 3 * D_MODEL * fw * 2
            + 3 * tm * 128 * 4 + 3 * tm * D_MODEL * 4 + 4 * MIB)
    return pl.pallas_call(
        _moe_body, grid_spec=grid_spec, out_shape=jax.ShapeDtypeStruct((n, D_MODEL), f32),
        compiler_params=_params(vmem, 3), name="moe_experts",
    )(cnt, hb, rank_t, gate_t, rank_c, w1, w3, w2)


NORM_TM = 512


def _norm_body(n_long_tiles, x_ref, y_ref, g_ref, o_long_ref, o_short_ref):
    x = x_ref[...] + y_ref[...]
    val = x * lax.rsqrt(jnp.mean(x * x, axis=-1, keepdims=True) + RMS_EPS) * g_ref[...]
    i = pl.program_id(0)

    @pl.when(i < n_long_tiles)
    def _():
        o_long_ref[...] = val

    @pl.when(i >= n_long_tiles)
    def _():
        o_short_ref[...] = val


def _add_norm(x, y, g, n_long):
    n = x.shape[0]
    tm = NORM_TM
    nl = n_long // tm
    row = pl.BlockSpec((tm, D_MODEL), lambda i: (i, 0))
    return pl.pallas_call(
        functools.partial(_norm_body, nl), grid=(n // tm,),
        in_specs=[row, row, _const_spec((1, D_MODEL))],
        out_specs=[pl.BlockSpec((tm, D_MODEL), lambda i: (jnp.minimum(i, nl - 1), 0)),
                   pl.BlockSpec((tm, D_MODEL), lambda i: (jnp.maximum(i - nl, 0), 0))],
        out_shape=[jax.ShapeDtypeStruct((n_long, D_MODEL), f32),
                   jax.ShapeDtypeStruct((n - n_long, D_MODEL), f32)],
        compiler_params=_params(12 * tm * D_MODEL * 4, 1), name="final_norm",
    )(x, y, g)


def kernel(x_prompt, x_sample, state_wkv, state_shift, norm_mix_g, w_in, shift_mu, sgu_ln_g, sgu_ln_b, sgu_w, sgu_b, rwkv_w0, rwkv_w2, rwkv_a0, rwkv_a2, rwkv_g2, rwkv_v0, rwkv_v1, rwkv_v2, rwkv_k_k, rwkv_k_a, rwkv_r_k, rwkv_ln_g, rwkv_ln_b, w_branch_a, w_branch_b, w_out, norm_ffn_g, ffn_w1, ffn_w3, ffn_w2, moe_router, moe_w1, moe_w3, moe_w2, norm_final_g):
    bp, tp, d = x_prompt.shape
    bs, ts, _ = x_sample.shape
    depth = w_in.shape[0]
    n_p, n_s = bp * tp, bs * ts
    x = jnp.concatenate([x_prompt.reshape(n_p, d), x_sample.reshape(n_s, d)], axis=0)
    row = lambda a: a.reshape(1, -1)

    hid = jnp.arange(4 * HEAD_B) // HEAD_B
    bd = (hid[:, None] == hid[None, :]).astype(bf16)
    tpos = jnp.arange(CHUNK)
    mask_long = (tpos[None, :] <= tpos[:, None]).astype(f32)
    mask_short = ((tpos[None, :] // ts == tpos[:, None] // ts) & (tpos[None, :] <= tpos[:, None])).astype(f32)
    sgu_mask = jnp.stack([mask_long, mask_short])
    zero_shift = jnp.zeros((bp, N_SHIFT), f32)

    shift_p, shift_s, chunk_v = [], [], []
    wkv_p = wkv_s = v_first = None
    for l in range(depth):
        vparams = None
        if l > 0:
            vparams = (rwkv_v1[l - 1].astype(bf16), rwkv_v2[l - 1].astype(bf16), row(rwkv_v0[l - 1]))
        outs = _in_proj(x, n_p, row(norm_mix_g[l]), w_in[l].astype(bf16), row(sgu_ln_g[l]), row(sgu_ln_b[l]),
                        vparams)
        u, va, va_s, ps, tails, gates = outs[:6]
        tiles_per_seq = tp // IN_TM
        shift_p.append(tails[tiles_per_seq - 1:bp * tiles_per_seq:tiles_per_seq, 7])
        shift_s.append(ps[n_p + ts - 1::ts])
        chunk_v.append(va_s.reshape(bs, ts, D_A))

        zpad = jnp.zeros((LORA_W, D_B), f32)
        wa2 = jnp.concatenate([jnp.concatenate([rwkv_w2[l], zpad], axis=1),
                               jnp.concatenate([zpad, rwkv_a2[l]], axis=1)], axis=0).astype(bf16)
        wts = (row(shift_mu[l]), wa2, row(rwkv_w0[l]), row(rwkv_a0[l]), rwkv_g2[l].astype(bf16),
               row(rwkv_k_k[l]), row(rwkv_k_a[l]), row(rwkv_r_k[l]), bd, row(rwkv_ln_g[l]), row(rwkv_ln_b[l]))
        first_s = jnp.pad(state_shift[l][:, None, :], ((0, 0), (0, ts - 1), (0, 0))).reshape(n_s, N_SHIFT)
        vextra = (outs[6], v_first) if l > 0 else None
        if l == 0:
            yb, v_first, wkv_p = _time_mix(ps, zero_shift, vextra, wts, None, (None, None, wkv_p), True,
                                           l, depth, bp, tp, 0)
            yb, v_first, wkv_s = _time_mix(ps, first_s, vextra, wts, state_wkv, (yb, v_first, wkv_s), True,
                                           l, depth, bs, ts, n_p)
        else:
            yb, wkv_p = _time_mix(ps, zero_shift, vextra, wts, None, (None, wkv_p), False, l, depth, bp, tp, 0)
            yb, wkv_s = _time_mix(ps, first_s, vextra, wts, state_wkv, (yb, wkv_s), False, l, depth, bs, ts, n_p)

        w_short = jnp.tile(sgu_w[l][:, :ts, :ts], (1, CHUNK // ts, CHUNK // ts))
        sgu_w2 = jnp.stack([sgu_w[l], w_short])
        b_long = jnp.repeat(sgu_b[l].T, GROUP_A, axis=1)
        b_short = jnp.tile(b_long[:ts], (CHUNK // ts, 1))
        sgu_b2 = jnp.stack([b_long, b_short])
        x = _mix_out(u, va, yb, gates, x, sgu_w2, sgu_mask, sgu_b2, n_p,
                     w_branch_a[l].astype(bf16), w_branch_b[l].astype(bf16), w_out[l].astype(bf16))

        j = l // 2
        if l % 2 == 0:
            x = _ffn_dense(x, row(norm_ffn_g[l]), ffn_w1[j].astype(bf16), ffn_w3[j].astype(bf16),
                           ffn_w2[j].astype(bf16))
            delta = None
        else:
            hb, gate_t, rank_t, rank_c, cnt = _router(x, row(norm_ffn_g[l]), moe_router[j].T)
            delta = _moe(cnt[:, :, 0].reshape(-1), hb, rank_t, gate_t, rank_c,
                         moe_w1[j].astype(bf16), moe_w3[j].astype(bf16), moe_w2[j].astype(bf16))
            if l < depth - 1:
                x = x + delta
                delta = None

    yn_p, yn_s = _add_norm(x, jnp.zeros_like(x) if delta is None else delta, row(norm_final_g), n_p)
    return (yn_p.reshape(bp, tp, d), yn_s.reshape(bs, ts, d),
            wkv_p, jnp.stack(shift_p), wkv_s, jnp.stack(shift_s),
            jnp.stack(chunk_v))
```

```python
import functools

import jax
import jax.numpy as jnp
from jax import lax
from jax.experimental import pallas as pl
from jax.experimental.pallas import tpu as pltpu

f32 = jnp.float32
bf16 = jnp.bfloat16

D_MODEL = 1024
CHUNK = 128
D_A = D_MODEL
GROUP_A = 128
H_A = D_A // GROUP_A
D_B = D_MODEL
HEAD_B = 64
H_B = D_B // HEAD_B
LORA_W = 64
LORA_A = 64
LORA_G = 128
N_SHIFT = 3 * D_B + LORA_W + LORA_A + LORA_G
N_IN = 2 * D_A + N_SHIFT + 2 * D_MODEL
N_EXPERTS = 8
RMS_EPS = 1e-6
LN_EPS = 1e-5
GN_EPS = 64e-5

PAIR = 2 * HEAD_B
N_PAIR = H_B // 2
MIB = 1024 * 1024
VMEM_CAP_V7X = 56 * MIB


def _params(vmem_bytes, n_grid):
    return pltpu.CompilerParams(
        dimension_semantics=("arbitrary",) * n_grid,
        vmem_limit_bytes=int(min(max(vmem_bytes, 16 * MIB), VMEM_CAP_V7X)),
    )


def _const_spec(shape):
    nd = len(shape)
    return pl.BlockSpec(shape, lambda *_: (0,) * nd, pipeline_mode=pl.Buffered(1))


def _gelu(x):
    return 0.5 * x * (1.0 + lax.erf(x * (2.0 ** -0.5)))


def _sigmoid(x):
    return 1.0 / (1.0 + jnp.exp(-x))


def _dot(a, b):
    return jnp.dot(a, b, preferred_element_type=f32)


def _dot_nt(a, b):
    return lax.dot_general(a, b, (((1,), (1,)), ((), ())), preferred_element_type=f32)


def _dot_tn(a, b):
    return lax.dot_general(a, b, (((0,), (0,)), ((), ())), preferred_element_type=f32)


def _head_sum(x, bd_ref):
    w = bd_ref.shape[0]
    cols = [_dot(x[:, c:c + w].astype(bf16), bd_ref[...]) for c in range(0, x.shape[1], w)]
    return jnp.concatenate(cols, axis=1)


IN_TM = 512
IN_CW = 512


def _in_proj_body(n_long_tiles, has_v, x_ref, g_ref, w_ref, lng_ref, lnb_ref, *rest):
    if has_v:
        v1_ref, v2_ref, v0_ref, u_ref, va_ref, vs_ref, ps_ref, tail_ref, gate_ref, vg_ref, h_scr, t_scr = rest
    else:
        u_ref, va_ref, vs_ref, ps_ref, tail_ref, gate_ref, h_scr, t_scr = rest
    x = x_ref[...]
    tm = x.shape[0]
    h = x * lax.rsqrt(jnp.mean(x * x, axis=-1, keepdims=True) + RMS_EPS) * g_ref[...]
    h_scr[...] = h.astype(bf16)

    def mm(c0, c1):
        return _dot(h_scr[...], w_ref[:, c0:c1])

    for c in range(0, D_A, IN_CW):
        u_ref[:, c:c + IN_CW] = _gelu(mm(c, c + IN_CW)).astype(bf16)
    for c in range(0, D_A, IN_CW):
        t_scr[:, c:c + IN_CW] = _gelu(mm(D_A + c, D_A + c + IN_CW))
    t = t_scr[...]
    mu = jnp.mean(t, axis=-1, keepdims=True)
    d = t - mu
    var = jnp.mean(d * d, axis=-1, keepdims=True)
    va = d * lax.rsqrt(var + LN_EPS) * lng_ref[...] + lnb_ref[...]
    va_ref[...] = va.astype(bf16)

    @pl.when(pl.program_id(0) >= n_long_tiles)
    def _():
        vs_ref[...] = va

    for c in range(0, N_SHIFT, IN_CW):
        c1 = min(c + IN_CW, N_SHIFT)
        ps_ref[:, c:c1] = mm(2 * D_A + c, 2 * D_A + c1)
    tail_ref[...] = ps_ref[tm - 8:tm, :]
    base = 2 * D_A + N_SHIFT
    for c in range(0, 2 * D_MODEL, IN_CW):
        gate_ref[:, c:c + IN_CW] = _sigmoid(mm(base + c, base + c + IN_CW)).astype(bf16)
    if has_v:
        lv = _dot(h_scr[...], v1_ref[...])
        vg_ref[...] = _sigmoid(v0_ref[...] + _dot(lv.astype(bf16), v2_ref[...])).astype(bf16)


def _in_proj(x, n_long, g, w_in, lng, lnb, vparams):
    n = x.shape[0]
    tm = IN_TM
    nl = n_long // tm
    has_v = vparams is not None
    row = lambda w: pl.BlockSpec((tm, w), lambda i: (i, 0))
    in_specs = [row(D_MODEL), _const_spec((1, D_MODEL)), _const_spec((D_MODEL, N_IN)),
                _const_spec((1, D_A)), _const_spec((1, D_A))]
    args = [x, g, w_in, lng, lnb]
    out_shape = [jax.ShapeDtypeStruct((n, D_A), bf16), jax.ShapeDtypeStruct((n, D_A), bf16),
                 jax.ShapeDtypeStruct((n - n_long, D_A), f32),
                 jax.ShapeDtypeStruct((n, N_SHIFT), f32), jax.ShapeDtypeStruct((n // tm, 8, N_SHIFT), f32),
                 jax.ShapeDtypeStruct((n, 2 * D_MODEL), bf16)]
    out_specs = [row(D_A), row(D_A), pl.BlockSpec((tm, D_A), lambda i: (jnp.maximum(i - nl, 0), 0)),
                 row(N_SHIFT), pl.BlockSpec((None, 8, N_SHIFT), lambda i: (i, 0, 0)), row(2 * D_MODEL)]
    if has_v:
        v1, v2, v0 = vparams
        in_specs += [_const_spec(v1.shape), _const_spec(v2.shape), _const_spec((1, D_B))]
        args += [v1, v2, v0]
        out_shape.append(jax.ShapeDtypeStruct((n, D_B), bf16))
        out_specs.append(row(D_B))
    out_bytes = 2 * (2 * D_A + 2 * D_MODEL + (D_B if has_v else 0)) + 4 * (N_SHIFT + D_A)
    vmem = 2 * D_MODEL * N_IN + 2 * tm * (4 * D_MODEL + out_bytes) + 8 * tm * D_MODEL * 4 + 4 * MIB
    return pl.pallas_call(
        functools.partial(_in_proj_body, nl, has_v),
        grid=(n // tm,), in_specs=in_specs, out_specs=out_specs, out_shape=out_shape,
        scratch_shapes=[pltpu.VMEM((tm, D_MODEL), bf16), pltpu.VMEM((tm, D_A), f32)],
        compiler_params=_params(vmem, 1), name="in_proj",
    )(*args)


N_MIX_W = 11
SCAN_ROWS = 64


N_OPS_BF, N_OPS_F32 = 7, 3


def _scan_outer(*args):
    for parity in (0, 1):
        @pl.when(pl.program_id(0) % 2 == parity)
        def _(parity=parity):
            _scan_body(parity, *args)


def _scan_body(parity, n_seq, slen, long_seq, n_chunk, n_blocks, has_v, has_s0, emit_v, n_alias, n_cast,
               ps_ref, prevblk_ref, first_ref, *rest):
    if has_v:
        (vg_ref, vf_ref), rest = rest[:2], rest[2:]
    (mu_ref, wa2_ref, w0_ref, a0_ref, g2_ref, kk_ref, ka_ref, rk_ref, bd_ref, lng_ref, lnb_ref), rest = (
        rest[:N_MIX_W], rest[N_MIX_W:])
    (tri_ref, tot_ref), rest = rest[:2], rest[2:]
    if has_s0:
        s0_ref, rest = rest[0], rest[1:]
    cast_src, rest = rest[:n_cast], rest[n_cast:]
    rest = rest[n_alias:]
    if emit_v:
        yb_ref, vo_ref, sl_ref, *rest = rest
    else:
        yb_ref, sl_ref, *rest = rest
    cast_dst, (s_scr, opb_scr, opf_scr) = rest[:n_cast], rest[n_cast:]
    for src, dst in zip(cast_src, cast_dst):
        dst[...] = src[...].astype(bf16)
    j = pl.program_id(0)
    blk_prep = jnp.minimum(j, n_blocks - 1)
    blk_run = jnp.maximum(j - 1, 0)
    c = blk_run % n_chunk

    if parity == 0:
        @pl.when(j == 0)
        def _():
            opb_scr[1] = jnp.zeros(opb_scr.shape[1:], bf16)
            opf_scr[1] = jnp.zeros(opf_scr.shape[1:], f32)

    @pl.when(c == 0)
    def _():
        if has_s0:
            zero = jnp.zeros((HEAD_B, HEAD_B), f32)

            def pack(g, carry):
                for p in range(N_PAIR):
                    top = jnp.concatenate([s0_ref[g, 2 * p], zero], axis=1)
                    bot = jnp.concatenate([zero, s0_ref[g, 2 * p + 1]], axis=1)
                    s_scr[p, pl.ds(pl.multiple_of(g * PAIR, PAIR), PAIR), :] = jnp.concatenate([top, bot], axis=0)
                return carry

            lax.fori_loop(0, n_seq, pack, 0)
        else:
            s_scr[...] = jnp.zeros(s_scr.shape, f32)

    ps = ps_ref[...]
    row = lax.broadcasted_iota(jnp.int32, (SCAN_ROWS, 1), 0)
    if long_seq:
        init = jnp.where(blk_prep % n_chunk == 0, first_ref[pl.ds(blk_prep // n_chunk, 1), :],
                         prevblk_ref[7:8, :])
        starts = row == 0
    else:
        init = first_ref[...]
        starts = row % slen == 0
    prev = jnp.where(starts, init, pltpu.roll(ps, 1, 0))
    xs = ps + mu_ref[...] * (prev - ps)
    rr = xs[:, 0:D_B]
    k_raw = xs[:, D_B:2 * D_B]
    v_new = xs[:, 2 * D_B:3 * D_B]
    dwa = xs[:, 3 * D_B:3 * D_B + LORA_W + LORA_A]
    dg = xs[:, 3 * D_B + LORA_W + LORA_A:N_SHIFT]
    lane = lax.broadcasted_iota(jnp.int32, dwa.shape, 1)
    lhs = jnp.where(lane < LORA_W, jnp.tanh(dwa), dwa).astype(bf16)
    wa = _dot(lhs, wa2_ref[...])
    g_new = _dot(_sigmoid(dg).astype(bf16), g2_ref[...])
    kkr = k_raw * kk_ref[...]
    kk_norm2 = _head_sum(kkr * kkr, bd_ref)

    at, rt, bt, kt, bh, kh, vv = [opb_scr[1 - parity, k] for k in range(N_OPS_BF)]
    g_last, gg, bonus = [opf_scr[1 - parity, k] for k in range(N_OPS_F32)]

    n_st = 2 * SCAN_ROWS
    def causal(width, inclusive):
        ri = lax.broadcasted_iota(jnp.int32, (n_st, width), 0)
        ci = lax.broadcasted_iota(jnp.int32, (n_st, width), 1) % n_st
        same = (ri // SCAN_ROWS == ci // SCAN_ROWS) & ((ri % SCAN_ROWS) // slen == (ci % SCAN_ROWS) // slen)
        t_r, t_c = ri % SCAN_ROWS, ci % SCAN_ROWS
        return same & ((t_c <= t_r) if inclusive else (t_c < t_r))

    incl2 = causal(2 * n_st, True)
    strict = causal(n_st, False)
    lane_e = lax.broadcasted_iota(jnp.int32, (1, PAIR), 1) < HEAD_B
    row_seq = ((lax.broadcasted_iota(jnp.int32, (2 * n_st, 1), 0) % SCAN_ROWS) // slen)

    def stack(z):
        return jnp.concatenate([jnp.where(lane_e, z, 0.0), jnp.where(lane_e, 0.0, z)], axis=0).astype(bf16)

    def pick_seq(big):
        if n_seq == 1:
            return big
        acc = jnp.where(row_seq == 0, big[:, :PAIR], 0.0)
        for g in range(1, n_seq):
            acc = acc + jnp.where(row_seq == g, big[:, g * PAIR:(g + 1) * PAIR], 0.0)
        return acc

    def spread_seq(z):
        if n_seq == 1:
            return z
        return jnp.concatenate([jnp.where(row_seq == g, z, jnp.zeros_like(z)) for g in range(n_seq)], axis=1)

    sls = [slice(p * PAIR, (p + 1) * PAIR) for p in range(N_PAIR)]
    pairs = range(N_PAIR)
    ar = [jnp.concatenate([stack(at[:, s]), stack(rt[:, s])], axis=0) for s in sls]
    bk = [jnp.concatenate([bt[:, s], bt[:, s], kt[:, s], kt[:, s]], axis=0).astype(bf16) for s in sls]
    pm = [_dot_nt(ar[p], bk[p]) for p in pairs]
    vb = [stack(vv[:, s]) for s in sls]
    qs = [pick_seq(_dot_nt(ar[p], s_scr[p].astype(bf16))) for p in pairs]
    x = [qs[p][:n_st] + _dot(jnp.where(strict, pm[p][:n_st, n_st:], 0.0).astype(bf16), vb[p]) for p in pairs]
    m = [jnp.where(strict, pm[p][:n_st, :n_st], 0.0).astype(bf16) for p in pairs]

    w_log = -jax.nn.softplus(-(w0_ref[...] + wa[:, :D_B])) - 0.5
    lw = -jnp.exp(w_log)
    a_new = _sigmoid(a0_ref[...] + wa[:, D_B:])
    k_new = k_raw * (1.0 + (a_new - 1.0) * ka_ref[...])
    if has_v:
        v_new = v_new + (vf_ref[...].astype(f32) - v_new) * vg_ref[...].astype(f32)
    if emit_v:
        vo_ref[...] = v_new.astype(bf16)
    cum = jnp.dot(tri_ref[...], lw, precision=lax.Precision.HIGHEST, preferred_element_type=f32)
    tot = jnp.dot(tot_ref[...], lw, precision=lax.Precision.HIGHEST, preferred_element_type=f32)
    rk_sum = _head_sum(rr * k_new * rk_ref[...], bd_ref)

    span = 1
    while span < slen:
        span *= 2
        if span < slen:
            z = [_dot(m[p], jnp.concatenate([x[p].astype(bf16), m[p]], axis=1)) for p in pairs]
            x = [x[p] + z[p][:, :n_st] for p in pairs]
            m = [z[p][:, n_st:].astype(bf16) for p in pairs]
        else:
            x = [x[p] + _dot(m[p], x[p].astype(bf16)) for p in pairs]
    uv = [jnp.concatenate([x[p].astype(bf16), vb[p]], axis=0) for p in pairs]
    ys = []
    for p in pairs:
        yp = qs[p][n_st:] + _dot(jnp.where(incl2, pm[p][n_st:], 0.0).astype(bf16), uv[p])
        ys.append(yp[:SCAN_ROWS] + yp[SCAN_ROWS:])
    y = jnp.concatenate(ys, axis=1)
    mean = _head_sum(y, bd_ref) * (1.0 / HEAD_B)
    d = y - mean
    var = _head_sum(d * d, bd_ref) * (1.0 / HEAD_B)
    yn = d * lax.rsqrt(var + GN_EPS) * lng_ref[...] + lnb_ref[...] + bonus
    yb_ref[...] = (yn * gg).astype(bf16)
    for p in pairs:
        bkh = jnp.concatenate([stack(bh[:, sls[p]]), stack(kh[:, sls[p]])], axis=0)
        upd = _dot_tn(spread_seq(uv[p]), bkh)
        for g in range(n_seq):
            rows = slice(g * PAIR, (g + 1) * PAIR)
            s_scr[p, rows, :] = (s_scr[p, rows, :] * g_last[g * slen:g * slen + 1, sls[p]] + upd[rows])

    kk_new = kkr * lax.rsqrt(kk_norm2 + 1e-12)
    b_new = kk_new * a_new
    g_inv = jnp.exp(-cum)
    g_tail = jnp.exp(tot - cum)
    folded = (-kk_new * jnp.exp(cum - lw), rr * jnp.exp(cum), b_new * g_inv, k_new * g_inv,
              b_new * g_tail, k_new * g_tail, v_new)
    for idx, val in enumerate(folded):
        opb_scr[parity, idx] = val.astype(bf16)
    for idx, val in enumerate((jnp.exp(tot), g_new, rk_sum * v_new)):
        opf_scr[parity, idx] = val

    @pl.when(c == n_chunk - 1)
    def _():
        def unpack(g, carry):
            for p in pairs:
                tile = s_scr[p, pl.ds(pl.multiple_of(g * PAIR, PAIR), PAIR), :]
                sl_ref[g, 2 * p] = tile[:HEAD_B, :HEAD_B]
                sl_ref[g, 2 * p + 1] = tile[HEAD_B:, HEAD_B:]
            return carry

        lax.fori_loop(0, n_seq, unpack, 0)


def _time_mix(ps, first, vextra, wts, s0, prevs, emit_v, layer, depth, n_seq_total, slen, row0, casts=()):
    n = ps.shape[0]
    long_seq = slen >= SCAN_ROWS
    if long_seq:
        n_seq, clen, n_chunk, n_outer = 1, SCAN_ROWS, slen // SCAN_ROWS, n_seq_total
    else:
        n_seq, clen, n_chunk, n_outer = SCAN_ROWS // slen, slen, 1, n_seq_total * slen // SCAN_ROWS
    blk0 = row0 // SCAN_ROWS
    n_blocks = n_outer * n_chunk
    prep = lambda j: jnp.minimum(j, n_blocks - 1)
    run = lambda j: jnp.maximum(j - 1, 0)
    tok_prep = lambda w: pl.BlockSpec((SCAN_ROWS, w), lambda j: (blk0 + prep(j), 0))
    tok_run = lambda w: pl.BlockSpec((SCAN_ROWS, w), lambda j: (blk0 + run(j), 0))
    st = pl.BlockSpec((None, n_seq, H_B, HEAD_B, HEAD_B), lambda j: (layer, run(j) // n_chunk, 0, 0, 0))
    sq = pl.BlockSpec((SCAN_ROWS, SCAN_ROWS), lambda j: (0, 0))
    t = jnp.arange(SCAN_ROWS)
    same_seq = t[:, None] // clen == t[None, :] // clen
    tri = (same_seq & (t[None, :] <= t[:, None])).astype(f32)
    in_specs = [tok_prep(N_SHIFT),
                pl.BlockSpec((8, N_SHIFT), lambda j: (jnp.maximum((blk0 + prep(j)) * (SCAN_ROWS // 8) - 1, 0), 0)),
                _const_spec(first.shape) if long_seq else pl.BlockSpec((SCAN_ROWS, N_SHIFT), lambda j: (prep(j), 0))]
    args = [ps, ps, first]
    if vextra is not None:
        in_specs += [tok_prep(D_B), tok_prep(D_B)]
        args += list(vextra)
    in_specs += [_const_spec(w.shape) for w in wts] + [sq, sq]
    args += list(wts) + [tri, same_seq.astype(f32)]
    if s0 is not None:
        in_specs.append(st)
        args.append(s0)
    cast_specs = [pl.BlockSpec((a.shape[0] // n_blocks, a.shape[1]), lambda j: (prep(j), 0)) for a in casts]
    in_specs += cast_specs
    args += list(casts)
    aliases = {}
    for out_idx, prev in enumerate(prevs):
        if prev is not None:
            aliases[len(args)] = out_idx
            in_specs.append(pl.BlockSpec(memory_space=pl.ANY))
            args.append(prev)
    out_specs = [tok_run(D_B)] + ([tok_prep(D_B)] if emit_v else []) + [st] + cast_specs
    out_shape = ([jax.ShapeDtypeStruct((n, D_B), bf16)] * (2 if emit_v else 1)
                 + [jax.ShapeDtypeStruct((depth, n_seq_total, H_B, HEAD_B, HEAD_B), f32)]
                 + [jax.ShapeDtypeStruct(a.shape, bf16) for a in casts])
    state_pad = 2 * n_seq * H_B * HEAD_B * PAIR * 4
    vmem = (2 * SCAN_ROWS * (2 * N_SHIFT * 4 + 6 * D_B * 2) + 2 * state_pad * (2 if s0 is not None else 1)
            + N_PAIR * n_seq * PAIR * PAIR * 4 + 64 * SCAN_ROWS * D_B * 4 + 8 * n_seq * PAIR * PAIR * 4 * 4
            + 8 * MIB)
    return pl.pallas_call(
        functools.partial(_scan_outer, n_seq, clen, long_seq, n_chunk, n_blocks, vextra is not None,
                          s0 is not None, emit_v, len(aliases), len(casts)),
        grid=(n_blocks + 1,), in_specs=in_specs, out_specs=out_specs, out_shape=out_shape,
        scratch_shapes=[pltpu.VMEM((N_PAIR, n_seq * PAIR, PAIR), f32),
                        pltpu.VMEM((2, N_OPS_BF, SCAN_ROWS, D_B), bf16),
                        pltpu.VMEM((2, N_OPS_F32, SCAN_ROWS, D_B), f32)],
        input_output_aliases=aliases,
        compiler_params=_params(vmem, 1), name="time_mix",
    )(*args)


MIX_TM = 256


def _mix_body(u_ref, va_ref, yb_ref, gate_ref, x_ref, sw_ref, sm_ref, sb_ref, wa_ref, wb_ref, wo_ref,
              o_ref, ya_scr):
    tm = u_ref.shape[0]
    for r0 in range(0, tm, CHUNK):
        for grp in range(H_A):
            cs = slice(grp * GROUP_A, (grp + 1) * GROUP_A)
            w = jnp.where(sm_ref[...] > 0.5, sw_ref[grp], 0.0).astype(bf16)
            mixed = _dot(w, va_ref[r0:r0 + CHUNK, cs].astype(bf16)) + sb_ref[:, cs]
            ya_scr[r0:r0 + CHUNK, cs] = (u_ref[r0:r0 + CHUNK, cs] * mixed).astype(bf16)
    merged = (gate_ref[:, :D_MODEL] * _dot(ya_scr[...], wa_ref[...])
              + gate_ref[:, D_MODEL:] * _dot(yb_ref[...], wb_ref[...]))
    o_ref[...] = x_ref[...] + _dot(merged.astype(bf16), wo_ref[...])


def _mix_out(u, va, yb, gates, x, sgu_w2, sgu_m2, sgu_b2, n_long, wa, wb, wo):
    n = x.shape[0]
    tm = MIX_TM
    row = lambda w: pl.BlockSpec((tm, w), lambda i: (i, 0))
    kind = lambda i: jnp.where(i * tm >= n_long, 1, 0)
    in_specs = [row(D_A), row(D_A), row(D_B), row(2 * D_MODEL), row(D_MODEL),
                pl.BlockSpec((None, H_A, CHUNK, CHUNK), lambda i: (kind(i), 0, 0, 0)),
                pl.BlockSpec((None, CHUNK, CHUNK), lambda i: (kind(i), 0, 0)),
                pl.BlockSpec((None, CHUNK, D_A), lambda i: (kind(i), 0, 0)),
                _const_spec(wa.shape), _const_spec(wb.shape), _const_spec(wo.shape)]
    vmem = 2 * tm * (2 * 5 + 4 * 2) * D_MODEL + 3 * 2 * D_MODEL * D_MODEL + 10 * tm * D_MODEL * 4 + 6 * MIB
    return pl.pallas_call(
        _mix_body, grid=(n // tm,), in_specs=in_specs, out_specs=row(D_MODEL),
        out_shape=jax.ShapeDtypeStruct((n, D_MODEL), f32),
        scratch_shapes=[pltpu.VMEM((tm, D_A), bf16)],
        compiler_params=_params(vmem, 1), name="mix_out",
    )(u, va, yb, gates, x, sgu_w2, sgu_m2, sgu_b2, wa, wb, wo)


FFN_TM = 512
FFN_CW = 256


def _ffn_body(x_ref, g_ref, w1_ref, w3_ref, w2_ref, o_ref, h_scr):
    x = x_ref[...]
    h = x * lax.rsqrt(jnp.mean(x * x, axis=-1, keepdims=True) + RMS_EPS) * g_ref[...]
    h_scr[...] = h.astype(bf16)
    o_ref[...] = x
    d_ff = w1_ref.shape[1]
    for c in range(0, d_ff, FFN_CW):
        a = _dot(h_scr[...], w1_ref[:, c:c + FFN_CW])
        b = _dot(h_scr[...], w3_ref[:, c:c + FFN_CW])
        t = (a * _sigmoid(a) * b).astype(bf16)
        o_ref[...] += _dot(t, w2_ref[c:c + FFN_CW, :])


def _ffn_dense(x, g, w1, w3, w2):
    n = x.shape[0]
    tm = FFN_TM
    row = pl.BlockSpec((tm, D_MODEL), lambda i: (i, 0))
    vmem = 3 * 2 * D_MODEL * w1.shape[1] + 4 * tm * D_MODEL * 4 + 8 * tm * D_MODEL * 4 + 4 * MIB
    return pl.pallas_call(
        _ffn_body, grid=(n // tm,),
        in_specs=[row, _const_spec((1, D_MODEL)), _const_spec(w1.shape), _const_spec(w3.shape),
                  _const_spec(w2.shape)],
        out_specs=row, out_shape=jax.ShapeDtypeStruct((n, D_MODEL), f32),
        scratch_shapes=[pltpu.VMEM((tm, D_MODEL), bf16)],
        compiler_params=_params(vmem, 1), name="ffn_dense",
    )(x, g, w1, w3, w2)


MOE_TM = 1408
MOE_BLOCK = 256
MOE_TAIL = 128
MOE_ONE_SHOT = 3
MOE_FC = 2


def _split3(x):
    hi = x.astype(bf16)
    r1 = x - hi.astype(f32)
    mid = r1.astype(bf16)
    lo = (r1 - mid.astype(f32)).astype(bf16)
    return hi, mid, lo


def _router_body(x_ref, g_ref, wr_ref, hb_ref, gate_t_ref, rank_t_ref, rank_c_ref, cnt_ref):
    x = x_ref[...]
    tm = x.shape[0]
    h = x * lax.rsqrt(jnp.mean(x * x, axis=-1, keepdims=True) + RMS_EPS) * g_ref[...]
    hb_ref[...] = h.astype(bf16)
    h_hi, h_mid, h_lo = _split3(h)
    w_hi, w_mid, w_lo = _split3(wr_ref[...])
    logits = (_dot_nt(w_hi, h_hi) + _dot_nt(w_hi, h_mid) + _dot_nt(w_mid, h_hi)
              + _dot_nt(w_hi, h_lo) + _dot_nt(w_mid, h_mid) + _dot_nt(w_lo, h_hi))
    eid = lax.broadcasted_iota(jnp.int32, logits.shape, 0)
    m1 = jnp.max(logits, axis=0, keepdims=True)
    i1 = jnp.min(jnp.where(logits == m1, eid, N_EXPERTS), axis=0, keepdims=True)
    sel1 = eid == i1
    rest = jnp.where(sel1, -jnp.inf, logits)
    m2 = jnp.max(rest, axis=0, keepdims=True)
    i2 = jnp.min(jnp.where(rest == m2, eid, N_EXPERTS), axis=0, keepdims=True)
    sel2 = eid == i2
    e2 = jnp.exp(m2 - m1)
    den = 1.0 + e2
    gate_t = jnp.where(sel1, 1.0 / den, 0.0) + jnp.where(sel2, e2 / den, 0.0)
    sel = jnp.where(sel1 | sel2, 1.0, 0.0)
    s_idx = lax.broadcasted_iota(jnp.int32, (tm, tm), 0)
    t_idx = lax.broadcasted_iota(jnp.int32, (tm, tm), 1)
    before = jnp.where(s_idx < t_idx, 1.0, 0.0).astype(bf16)
    rank = _dot(sel.astype(bf16), before)
    rank_t = jnp.where(sel > 0.5, rank, -1.0)
    gate_t_ref[...] = gate_t
    rank_t_ref[...] = rank_t
    rank_c_ref[...] = rank_t.T
    cnt = jnp.sum(sel, axis=1, keepdims=True)
    cnt_ref[...] = jnp.broadcast_to(cnt, cnt_ref.shape).astype(jnp.int32)


def _router(x, g, wr_t):
    n = x.shape[0]
    tm = MOE_TM
    nt = n // tm
    vmem = 2 * tm * D_MODEL * 6 + 8 * tm * D_MODEL * 4 + 3 * tm * tm * 4 + 8 * MIB
    return pl.pallas_call(
        _router_body, grid=(nt,),
        in_specs=[pl.BlockSpec((tm, D_MODEL), lambda i: (i, 0)), _const_spec((1, D_MODEL)),
                  _const_spec(wr_t.shape)],
        out_specs=[pl.BlockSpec((tm, D_MODEL), lambda i: (i, 0)),
                   pl.BlockSpec((N_EXPERTS, tm), lambda i: (0, i)),
                   pl.BlockSpec((N_EXPERTS, tm), lambda i: (0, i)),
                   pl.BlockSpec((tm, N_EXPERTS), lambda i: (i, 0)),
                   pl.BlockSpec((None, N_EXPERTS, 128), lambda i: (i, 0, 0))],
        out_shape=[jax.ShapeDtypeStruct((n, D_MODEL), bf16),
                   jax.ShapeDtypeStruct((N_EXPERTS, n), f32), jax.ShapeDtypeStruct((N_EXPERTS, n), f32),
                   jax.ShapeDtypeStruct((n, N_EXPERTS), f32),
                   jax.ShapeDtypeStruct((nt, N_EXPERTS, 128), jnp.int32)],
        compiler_params=_params(vmem, 1), name="moe_router",
    )(x, g, wr_t)


def _moe_body(cnt_ref, hb_ref, rank_t_ref, gate_t_ref, rank_c_ref, w1_ref, w3_ref, w2_ref,
              o_ref, xg_scr, acc_scr):
    i, e, c = pl.program_id(0), pl.program_id(1), pl.program_id(2)
    tm = hb_ref.shape[0]
    cnt = cnt_ref[i * N_EXPERTS + e]
    n_full = cnt // MOE_BLOCK
    rem = cnt - n_full * MOE_BLOCK
    n_big = n_full + jnp.where(rem > MOE_TAIL, 1, 0)
    short_tail = (rem > 0) & (rem <= MOE_TAIL)

    n_tail = (cnt + MOE_TAIL - 1) // MOE_TAIL

    def for_blocks(fn):
        for k in range(1, MOE_ONE_SHOT + 1):
            @pl.when(n_tail == k)
            def _(k=k):
                fn(0, k * MOE_TAIL)

        @pl.when(n_tail > MOE_ONE_SHOT)
        def _():
            def body(j, carry):
                fn(pl.multiple_of(j * MOE_BLOCK, MOE_BLOCK), MOE_BLOCK)
                return carry

            lax.fori_loop(0, n_big, body, 0)

            @pl.when(short_tail)
            def _():
                fn(pl.multiple_of(n_full * MOE_BLOCK, MOE_TAIL), MOE_TAIL)

    def slots(r0, rows, axis):
        shape = (rows, 1) if axis == 0 else (1, rows)
        return (lax.broadcasted_iota(jnp.int32, shape, axis) + r0).astype(f32)

    @pl.when((e == 0) & (c == 0))
    def _():
        o_ref[...] = jnp.zeros(o_ref.shape, f32)

    @pl.when(c == 0)
    def _():
        rank_row = rank_t_ref[pl.ds(e, 1), :]

        def gather(r0, rows):
            onehot = jnp.where(rank_row == slots(r0, rows, 0), 1.0, 0.0).astype(bf16)
            xg_scr[pl.ds(r0, rows), :] = _dot(onehot, hb_ref[...]).astype(bf16)
            acc_scr[pl.ds(r0, rows), :] = jnp.zeros((rows, D_MODEL), f32)

        for_blocks(gather)

    def expert(r0, rows):
        xb = xg_scr[pl.ds(r0, rows), :]
        a = _dot(xb, w1_ref[...])
        b = _dot(xb, w3_ref[...])
        t = (a * _sigmoid(a) * b).astype(bf16)
        acc_scr[pl.ds(r0, rows), :] += _dot(t, w2_ref[...])

    for_blocks(expert)

    @pl.when(c == pl.num_programs(2) - 1)
    def _():
        lane8 = lax.broadcasted_iota(jnp.int32, (tm, N_EXPERTS), 1)
        rank_col = jnp.sum(jnp.where(lane8 == e, rank_c_ref[...], 0.0), axis=1, keepdims=True)
        rank_row = rank_t_ref[pl.ds(e, 1), :]
        gate_row = gate_t_ref[pl.ds(e, 1), :]

        def scatter(r0, rows):
            gate_blk = jnp.sum(jnp.where(rank_row == slots(r0, rows, 0), gate_row, 0.0), axis=1, keepdims=True)
            onehot = jnp.where(rank_col == slots(r0, rows, 1), 1.0, 0.0).astype(bf16)
            scaled = (acc_scr[pl.ds(r0, rows), :] * gate_blk).astype(bf16)
            o_ref[...] += _dot(onehot, scaled)

        for_blocks(scatter)


def _moe(cnt, hb, rank_t, gate_t, rank_c, w1, w3, w2):
    n = hb.shape[0]
    tm = MOE_TM
    fw = w2.shape[1] // MOE_FC
    tile = lambda i, e, c, cnt: (i, 0)
    once = dict(pipeline_mode=pl.Buffered(1))
    grid_spec = pltpu.PrefetchScalarGridSpec(
        num_scalar_prefetch=1, grid=(n // tm, N_EXPERTS, MOE_FC),
        in_specs=[pl.BlockSpec((tm, D_MODEL), tile),
                  pl.BlockSpec((N_EXPERTS, tm), lambda i, e, c, cnt: (0, i), **once),
                  pl.BlockSpec((N_EXPERTS, tm), lambda i, e, c, cnt: (0, i), **once),
                  pl.BlockSpec((tm, N_EXPERTS), tile, **once),
                  pl.BlockSpec((None, D_MODEL, fw), lambda i, e, c, cnt: (e, 0, c)),
                  pl.BlockSpec((None, D_MODEL, fw), lambda i, e, c, cnt: (e, 0, c)),
                  pl.BlockSpec((None, fw, D_MODEL), lambda i, e, c, cnt: (e, c, 0))],
        out_specs=pl.BlockSpec((tm, D_MODEL), tile),
        scratch_shapes=[pltpu.VMEM((tm, D_MODEL), bf16), pltpu.VMEM((tm, D_MODEL), f32)])
    vmem = (tm * D_MODEL * (2 + 2 * 4) + tm * D_MODEL * 6 + 2 * 3 * D_MODEL * fw * 2
            + 3 * tm * 128 * 4 + 3 * tm * D_MODEL * 4 + 4 * MIB)
    return pl.pallas_call(
        _moe_body, grid_spec=grid_spec, out_shape=jax.ShapeDtypeStruct((n, D_MODEL), f32),
        compiler_params=_params(vmem, 3), name="moe_experts",
    )(cnt, hb, rank_t, gate_t, rank_c, w1, w3, w2)


NORM_TM = 512


def _norm_body(n_long_tiles, x_ref, y_ref, g_ref, o_long_ref, o_short_ref):
    x = x_ref[...] + y_ref[...]
    val = x * lax.rsqrt(jnp.mean(x * x, axis=-1, keepdims=True) + RMS_EPS) * g_ref[...]
    i = pl.program_id(0)

    @pl.when(i < n_long_tiles)
    def _():
        o_long_ref[...] = val

    @pl.when(i >= n_long_tiles)
    def _():
        o_short_ref[...] = val


def _add_norm(x, y, g, n_long):
    n = x.shape[0]
    tm = NORM_TM
    nl = n_long // tm
    row = pl.BlockSpec((tm, D_MODEL), lambda i: (i, 0))
    return pl.pallas_call(
        functools.partial(_norm_body, nl), grid=(n // tm,),
        in_specs=[row, row, _const_spec((1, D_MODEL))],
        out_specs=[pl.BlockSpec((tm, D_MODEL), lambda i: (jnp.minimum(i, nl - 1), 0)),
                   pl.BlockSpec((tm, D_MODEL), lambda i: (jnp.maximum(i - nl, 0), 0))],
        out_shape=[jax.ShapeDtypeStruct((n_long, D_MODEL), f32),
                   jax.ShapeDtypeStruct((n - n_long, D_MODEL), f32)],
        compiler_params=_params(12 * tm * D_MODEL * 4, 1), name="final_norm",
    )(x, y, g)


def kernel(x_prompt, x_sample, state_wkv, state_shift, norm_mix_g, w_in, shift_mu, sgu_ln_g, sgu_ln_b, sgu_w, sgu_b, rwkv_w0, rwkv_w2, rwkv_a0, rwkv_a2, rwkv_g2, rwkv_v0, rwkv_v1, rwkv_v2, rwkv_k_k, rwkv_k_a, rwkv_r_k, rwkv_ln_g, rwkv_ln_b, w_branch_a, w_branch_b, w_out, norm_ffn_g, ffn_w1, ffn_w3, ffn_w2, moe_router, moe_w1, moe_w3, moe_w2, norm_final_g):
    bp, tp, d = x_prompt.shape
    bs, ts, _ = x_sample.shape
    depth = w_in.shape[0]
    n_p, n_s = bp * tp, bs * ts
    x = jnp.concatenate([x_prompt.reshape(n_p, d), x_sample.reshape(n_s, d)], axis=0)
    row = lambda a: a.reshape(1, -1)

    hid = jnp.arange(4 * HEAD_B) // HEAD_B
    bd = (hid[:, None] == hid[None, :]).astype(bf16)
    tpos = jnp.arange(CHUNK)
    mask_long = (tpos[None, :] <= tpos[:, None]).astype(f32)
    mask_short = ((tpos[None, :] // ts == tpos[:, None] // ts) & (tpos[None, :] <= tpos[:, None])).astype(f32)
    sgu_mask = jnp.stack([mask_long, mask_short])
    zero_shift = jnp.zeros((bp, N_SHIFT), f32)

    shift_p, shift_s, chunk_v = [], [], []
    wkv_p = wkv_s = v_first = None
    for l in range(depth):
        vparams = None
        if l > 0:
            vparams = (rwkv_v1[l - 1].astype(bf16), rwkv_v2[l - 1].astype(bf16), row(rwkv_v0[l - 1]))
        outs = _in_proj(x, n_p, row(norm_mix_g[l]), w_in[l].astype(bf16), row(sgu_ln_g[l]), row(sgu_ln_b[l]),
                        vparams)
        u, va, va_s, ps, tails, gates = outs[:6]
        tiles_per_seq = tp // IN_TM
        shift_p.append(tails[tiles_per_seq - 1:bp * tiles_per_seq:tiles_per_seq, 7])
        shift_s.append(ps[n_p + ts - 1::ts])
        chunk_v.append(va_s.reshape(bs, ts, D_A))

        zpad = jnp.zeros((LORA_W, D_B), f32)
        wa2 = jnp.concatenate([jnp.concatenate([rwkv_w2[l], zpad], axis=1),
                               jnp.concatenate([zpad, rwkv_a2[l]], axis=1)], axis=0).astype(bf16)
        wts = (row(shift_mu[l]), wa2, row(rwkv_w0[l]), row(rwkv_a0[l]), rwkv_g2[l].astype(bf16),
               row(rwkv_k_k[l]), row(rwkv_k_a[l]), row(rwkv_r_k[l]), bd, row(rwkv_ln_g[l]), row(rwkv_ln_b[l]))
        first_s = jnp.pad(state_shift[l][:, None, :], ((0, 0), (0, ts - 1), (0, 0))).reshape(n_s, N_SHIFT)
        vextra = (outs[6], v_first) if l > 0 else None
        if l == 0:
            n_blk = n_p // SCAN_ROWS
            moe_2d = [w.reshape(-1, w.shape[-1]) for w in (moe_w1, moe_w3, moe_w2)]
            ride = all(w.shape[0] % (16 * n_blk) == 0 for w in moe_2d)
            yb, v_first, wkv_p, *moe_bf = _time_mix(ps, zero_shift, vextra, wts, None, (None, None, wkv_p), True,
                                                    l, depth, bp, tp, 0, casts=moe_2d if ride else ())
            if not ride:
                moe_bf = [w.astype(bf16) for w in moe_2d]
            moe_bf = [w.reshape(s.shape) for w, s in zip(moe_bf, (moe_w1, moe_w3, moe_w2))]
            yb, v_first, wkv_s = _time_mix(ps, first_s, vextra, wts, state_wkv, (yb, v_first, wkv_s), True,
                                           l, depth, bs, ts, n_p)
        else:
            yb, wkv_p = _time_mix(ps, zero_shift, vextra, wts, None, (None, wkv_p), False, l, depth, bp, tp, 0)
            yb, wkv_s = _time_mix(ps, first_s, vextra, wts, state_wkv, (yb, wkv_s), False, l, depth, bs, ts, n_p)

        w_short = jnp.tile(sgu_w[l][:, :ts, :ts], (1, CHUNK // ts, CHUNK // ts))
        sgu_w2 = jnp.stack([sgu_w[l], w_short])
        b_long = jnp.repeat(sgu_b[l].T, GROUP_A, axis=1)
        b_short = jnp.tile(b_long[:ts], (CHUNK // ts, 1))
        sgu_b2 = jnp.stack([b_long, b_short])
        x = _mix_out(u, va, yb, gates, x, sgu_w2, sgu_mask, sgu_b2, n_p,
                     w_branch_a[l].astype(bf16), w_branch_b[l].astype(bf16), w_out[l].astype(bf16))

        j = l // 2
        if l % 2 == 0:
            x = _ffn_dense(x, row(norm_ffn_g[l]), ffn_w1[j].astype(bf16), ffn_w3[j].astype(bf16),
                           ffn_w2[j].astype(bf16))
            delta = None
        else:
            hb, gate_t, rank_t, rank_c, cnt = _router(x, row(norm_ffn_g[l]), moe_router[j].T)
            delta = _moe(cnt[:, :, 0].reshape(-1), hb, rank_t, gate_t, rank_c,
                         moe_bf[0][j], moe_bf[1][j], moe_bf[2][j])
            if l < depth - 1:
                x = x + delta
                delta = None

    yn_p, yn_s = _add_norm(x, jnp.zeros_like(x) if delta is None else delta, row(norm_final_g), n_p)
    return (yn_p.reshape(bp, tp, d), yn_s.reshape(bs, ts, d),
            wkv_p, jnp.stack(shift_p), wkv_s, jnp.stack(shift_s),
            jnp.stack(chunk_v))
```

```python
import functools

import jax
import jax.numpy as jnp
from jax import lax
from jax.experimental import pallas as pl
from jax.experimental.pallas import tpu as pltpu

f32 = jnp.float32
bf16 = jnp.bfloat16

D_MODEL = 1024
CHUNK = 128
D_A = D_MODEL
GROUP_A = 128
H_A = D_A // GROUP_A
D_B = D_MODEL
HEAD_B = 64
H_B = D_B // HEAD_B
LORA_W = 64
LORA_A = 64
LORA_G = 128
N_SHIFT = 3 * D_B + LORA_W + LORA_A + LORA_G
N_IN = 2 * D_A + N_SHIFT + 2 * D_MODEL
N_EXPERTS = 8
RMS_EPS = 1e-6
LN_EPS = 1e-5
GN_EPS = 64e-5

PAIR = 2 * HEAD_B
N_PAIR = H_B // 2
MIB = 1024 * 1024
VMEM_CAP_V7X = 56 * MIB


def _params(vmem_bytes, n_grid):
    return pltpu.CompilerParams(
        dimension_semantics=("arbitrary",) * n_grid,
        vmem_limit_bytes=int(min(max(vmem_bytes, 16 * MIB), VMEM_CAP_V7X)),
    )


def _const_spec(shape):
    nd = len(shape)
    return pl.BlockSpec(shape, lambda *_: (0,) * nd, pipeline_mode=pl.Buffered(1))


def _gelu(x):
    return 0.5 * x * (1.0 + lax.erf(x * (2.0 ** -0.5)))


def _sigmoid(x):
    return 1.0 / (1.0 + jnp.exp(-x))


def _dot(a, b):
    return jnp.dot(a, b, preferred_element_type=f32)


def _dot_nt(a, b):
    return lax.dot_general(a, b, (((1,), (1,)), ((), ())), preferred_element_type=f32)


def _dot_tn(a, b):
    return lax.dot_general(a, b, (((0,), (0,)), ((), ())), preferred_element_type=f32)


def _head_sum(x, bd_ref):
    w = bd_ref.shape[0]
    cols = [_dot(x[:, c:c + w].astype(bf16), bd_ref[...]) for c in range(0, x.shape[1], w)]
    return jnp.concatenate(cols, axis=1)


IN_TM = 512
IN_CW = 512


def _in_proj_body(n_long_tiles, has_v, x_ref, g_ref, w_ref, lng_ref, lnb_ref, *rest):
    if has_v:
        v1_ref, v2_ref, v0_ref, u_ref, va_ref, vs_ref, ps_ref, tail_ref, gate_ref, vg_ref, h_scr, t_scr = rest
    else:
        u_ref, va_ref, vs_ref, ps_ref, tail_ref, gate_ref, h_scr, t_scr = rest
    x = x_ref[...]
    tm = x.shape[0]
    h = x * lax.rsqrt(jnp.mean(x * x, axis=-1, keepdims=True) + RMS_EPS) * g_ref[...]
    h_scr[...] = h.astype(bf16)

    def mm(c0, c1):
        return _dot(h_scr[...], w_ref[:, c0:c1])

    for c in range(0, D_A, IN_CW):
        u_ref[:, c:c + IN_CW] = _gelu(mm(c, c + IN_CW)).astype(bf16)
    for c in range(0, D_A, IN_CW):
        t_scr[:, c:c + IN_CW] = _gelu(mm(D_A + c, D_A + c + IN_CW))
    t = t_scr[...]
    mu = jnp.mean(t, axis=-1, keepdims=True)
    d = t - mu
    var = jnp.mean(d * d, axis=-1, keepdims=True)
    va = d * lax.rsqrt(var + LN_EPS) * lng_ref[...] + lnb_ref[...]
    va_ref[...] = va.astype(bf16)

    @pl.when(pl.program_id(0) >= n_long_tiles)
    def _():
        vs_ref[...] = va

    for c in range(0, N_SHIFT, IN_CW):
        c1 = min(c + IN_CW, N_SHIFT)
        ps_ref[:, c:c1] = mm(2 * D_A + c, 2 * D_A + c1)
    tail_ref[...] = ps_ref[tm - 8:tm, :]
    base = 2 * D_A + N_SHIFT
    for c in range(0, 2 * D_MODEL, IN_CW):
        gate_ref[:, c:c + IN_CW] = _sigmoid(mm(base + c, base + c + IN_CW)).astype(bf16)
    if has_v:
        lv = _dot(h_scr[...], v1_ref[...])
        vg_ref[...] = _sigmoid(v0_ref[...] + _dot(lv.astype(bf16), v2_ref[...])).astype(bf16)


def _in_proj(x, n_long, g, w_in, lng, lnb, vparams):
    n = x.shape[0]
    tm = IN_TM
    nl = n_long // tm
    has_v = vparams is not None
    row = lambda w: pl.BlockSpec((tm, w), lambda i: (i, 0))
    in_specs = [row(D_MODEL), _const_spec((1, D_MODEL)), _const_spec((D_MODEL, N_IN)),
                _const_spec((1, D_A)), _const_spec((1, D_A))]
    args = [x, g, w_in, lng, lnb]
    out_shape = [jax.ShapeDtypeStruct((n, D_A), bf16), jax.ShapeDtypeStruct((n, D_A), bf16),
                 jax.ShapeDtypeStruct((n - n_long, D_A), f32),
                 jax.ShapeDtypeStruct((n, N_SHIFT), f32), jax.ShapeDtypeStruct((n // tm, 8, N_SHIFT), f32),
                 jax.ShapeDtypeStruct((n, 2 * D_MODEL), bf16)]
    out_specs = [row(D_A), row(D_A), pl.BlockSpec((tm, D_A), lambda i: (jnp.maximum(i - nl, 0), 0)),
                 row(N_SHIFT), pl.BlockSpec((None, 8, N_SHIFT), lambda i: (i, 0, 0)), row(2 * D_MODEL)]
    if has_v:
        v1, v2, v0 = vparams
        in_specs += [_const_spec(v1.shape), _const_spec(v2.shape), _const_spec((1, D_B))]
        args += [v1, v2, v0]
        out_shape.append(jax.ShapeDtypeStruct((n, D_B), bf16))
        out_specs.append(row(D_B))
    out_bytes = 2 * (2 * D_A + 2 * D_MODEL + (D_B if has_v else 0)) + 4 * (N_SHIFT + D_A)
    vmem = 2 * D_MODEL * N_IN + 2 * tm * (4 * D_MODEL + out_bytes) + 8 * tm * D_MODEL * 4 + 4 * MIB
    return pl.pallas_call(
        functools.partial(_in_proj_body, nl, has_v),
        grid=(n // tm,), in_specs=in_specs, out_specs=out_specs, out_shape=out_shape,
        scratch_shapes=[pltpu.VMEM((tm, D_MODEL), bf16), pltpu.VMEM((tm, D_A), f32)],
        compiler_params=_params(vmem, 1), name="in_proj",
    )(*args)


N_MIX_W = 11
SCAN_ROWS = 64


N_OPS_BF, N_OPS_F32 = 7, 3


def _scan_outer(*args):
    for parity in (0, 1):
        @pl.when(pl.program_id(0) % 2 == parity)
        def _(parity=parity):
            _scan_body(parity, *args)


def _scan_body(parity, n_seq, slen, long_seq, n_chunk, n_blocks, has_v, has_s0, emit_v, n_alias, n_cast,
               ps_ref, prevblk_ref, first_ref, *rest):
    if has_v:
        (vg_ref, vf_ref), rest = rest[:2], rest[2:]
    (mu_ref, wa2_ref, w0_ref, a0_ref, g2_ref, kk_ref, ka_ref, rk_ref, bd_ref, lng_ref, lnb_ref), rest = (
        rest[:N_MIX_W], rest[N_MIX_W:])
    (tri_ref, tot_ref), rest = rest[:2], rest[2:]
    if has_s0:
        s0_ref, rest = rest[0], rest[1:]
    cast_src, rest = rest[:n_cast], rest[n_cast:]
    rest = rest[n_alias:]
    if emit_v:
        yb_ref, vo_ref, sl_ref, *rest = rest
    else:
        yb_ref, sl_ref, *rest = rest
    cast_dst, (s_scr, opb_scr, opf_scr) = rest[:n_cast], rest[n_cast:]
    for src, dst in zip(cast_src, cast_dst):
        dst[...] = src[...].astype(bf16)
    j = pl.program_id(0)
    blk_prep = jnp.minimum(j, n_blocks - 1)
    blk_run = jnp.maximum(j - 1, 0)
    c = blk_run % n_chunk

    if parity == 0:
        @pl.when(j == 0)
        def _():
            opb_scr[1] = jnp.zeros(opb_scr.shape[1:], bf16)
            opf_scr[1] = jnp.zeros(opf_scr.shape[1:], f32)

    @pl.when(c == 0)
    def _():
        if has_s0:
            zero = jnp.zeros((HEAD_B, HEAD_B), f32)

            def pack(g, carry):
                for p in range(N_PAIR):
                    top = jnp.concatenate([s0_ref[g, 2 * p], zero], axis=1)
                    bot = jnp.concatenate([zero, s0_ref[g, 2 * p + 1]], axis=1)
                    s_scr[p, pl.ds(pl.multiple_of(g * PAIR, PAIR), PAIR), :] = jnp.concatenate([top, bot], axis=0)
                return carry

            lax.fori_loop(0, n_seq, pack, 0)
        else:
            s_scr[...] = jnp.zeros(s_scr.shape, f32)

    ps = ps_ref[...]
    row = lax.broadcasted_iota(jnp.int32, (SCAN_ROWS, 1), 0)
    if long_seq:
        init = jnp.where(blk_prep % n_chunk == 0, first_ref[pl.ds(blk_prep // n_chunk, 1), :],
                         prevblk_ref[7:8, :])
        starts = row == 0
    else:
        init = first_ref[...]
        starts = row % slen == 0
    prev = jnp.where(starts, init, pltpu.roll(ps, 1, 0))
    xs = ps + mu_ref[...] * (prev - ps)
    rr = xs[:, 0:D_B]
    k_raw = xs[:, D_B:2 * D_B]
    v_new = xs[:, 2 * D_B:3 * D_B]
    dwa = xs[:, 3 * D_B:3 * D_B + LORA_W + LORA_A]
    dg = xs[:, 3 * D_B + LORA_W + LORA_A:N_SHIFT]
    lane = lax.broadcasted_iota(jnp.int32, dwa.shape, 1)
    lhs = jnp.where(lane < LORA_W, jnp.tanh(dwa), dwa).astype(bf16)
    wa = _dot(lhs, wa2_ref[...])
    g_new = _dot(_sigmoid(dg).astype(bf16), g2_ref[...])
    kkr = k_raw * kk_ref[...]
    kk_norm2 = _head_sum(kkr * kkr, bd_ref)

    at, rt, bt, kt, bh, kh, vv = [opb_scr[1 - parity, k] for k in range(N_OPS_BF)]
    g_last, gg, bonus = [opf_scr[1 - parity, k] for k in range(N_OPS_F32)]

    n_st = 2 * SCAN_ROWS
    def causal(width, inclusive):
        ri = lax.broadcasted_iota(jnp.int32, (n_st, width), 0)
        ci = lax.broadcasted_iota(jnp.int32, (n_st, width), 1) % n_st
        same = (ri // SCAN_ROWS == ci // SCAN_ROWS) & ((ri % SCAN_ROWS) // slen == (ci % SCAN_ROWS) // slen)
        t_r, t_c = ri % SCAN_ROWS, ci % SCAN_ROWS
        return same & ((t_c <= t_r) if inclusive else (t_c < t_r))

    incl2 = causal(2 * n_st, True)
    strict = causal(n_st, False)
    lane_e = lax.broadcasted_iota(jnp.int32, (1, PAIR), 1) < HEAD_B
    row_seq = ((lax.broadcasted_iota(jnp.int32, (2 * n_st, 1), 0) % SCAN_ROWS) // slen)

    def stack(z):
        return jnp.concatenate([jnp.where(lane_e, z, 0.0), jnp.where(lane_e, 0.0, z)], axis=0).astype(bf16)

    def pick_seq(big):
        if n_seq == 1:
            return big
        acc = jnp.where(row_seq == 0, big[:, :PAIR], 0.0)
        for g in range(1, n_seq):
            acc = acc + jnp.where(row_seq == g, big[:, g * PAIR:(g + 1) * PAIR], 0.0)
        return acc

    def spread_seq(z):
        if n_seq == 1:
            return z
        return jnp.concatenate([jnp.where(row_seq == g, z, jnp.zeros_like(z)) for g in range(n_seq)], axis=1)

    sls = [slice(p * PAIR, (p + 1) * PAIR) for p in range(N_PAIR)]
    pairs = range(N_PAIR)
    ar = [jnp.concatenate([stack(at[:, s]), stack(rt[:, s])], axis=0) for s in sls]
    bk = [jnp.concatenate([bt[:, s], bt[:, s], kt[:, s], kt[:, s]], axis=0).astype(bf16) for s in sls]
    pm = [_dot_nt(ar[p], bk[p]) for p in pairs]
    vb = [stack(vv[:, s]) for s in sls]
    qs = [pick_seq(_dot_nt(ar[p], s_scr[p].astype(bf16))) for p in pairs]
    x = [qs[p][:n_st] + _dot(jnp.where(strict, pm[p][:n_st, n_st:], 0.0).astype(bf16), vb[p]) for p in pairs]
    m = [jnp.where(strict, pm[p][:n_st, :n_st], 0.0).astype(bf16) for p in pairs]

    w_log = -jax.nn.softplus(-(w0_ref[...] + wa[:, :D_B])) - 0.5
    lw = -jnp.exp(w_log)
    a_new = _sigmoid(a0_ref[...] + wa[:, D_B:])
    k_new = k_raw * (1.0 + (a_new - 1.0) * ka_ref[...])
    if has_v:
        v_new = v_new + (vf_ref[...].astype(f32) - v_new) * vg_ref[...].astype(f32)
    if emit_v:
        vo_ref[...] = v_new.astype(bf16)
    cum = jnp.dot(tri_ref[...], lw, precision=lax.Precision.HIGHEST, preferred_element_type=f32)
    tot = jnp.dot(tot_ref[...], lw, precision=lax.Precision.HIGHEST, preferred_element_type=f32)
    rk_sum = _head_sum(rr * k_new * rk_ref[...], bd_ref)

    span = 1
    while span < slen:
        span *= 2
        if span < slen:
            z = [_dot(m[p], jnp.concatenate([x[p].astype(bf16), m[p]], axis=1)) for p in pairs]
            x = [x[p] + z[p][:, :n_st] for p in pairs]
            m = [z[p][:, n_st:].astype(bf16) for p in pairs]
        else:
            x = [x[p] + _dot(m[p], x[p].astype(bf16)) for p in pairs]
    uv = [jnp.concatenate([x[p].astype(bf16), vb[p]], axis=0) for p in pairs]
    ys = []
    for p in pairs:
        yp = qs[p][n_st:] + _dot(jnp.where(incl2, pm[p][n_st:], 0.0).astype(bf16), uv[p])
        ys.append(yp[:SCAN_ROWS] + yp[SCAN_ROWS:])
    y = jnp.concatenate(ys, axis=1)
    mean = _head_sum(y, bd_ref) * (1.0 / HEAD_B)
    d = y - mean
    var = _head_sum(d * d, bd_ref) * (1.0 / HEAD_B)
    yn = d * lax.rsqrt(var + GN_EPS) * lng_ref[...] + lnb_ref[...] + bonus
    yb_ref[...] = (yn * gg).astype(bf16)
    for p in pairs:
        bkh = jnp.concatenate([stack(bh[:, sls[p]]), stack(kh[:, sls[p]])], axis=0)
        upd = _dot_tn(spread_seq(uv[p]), bkh)
        for g in range(n_seq):
            rows = slice(g * PAIR, (g + 1) * PAIR)
            s_scr[p, rows, :] = (s_scr[p, rows, :] * g_last[g * slen:g * slen + 1, sls[p]] + upd[rows])

    kk_new = kkr * lax.rsqrt(kk_norm2 + 1e-12)
    b_new = kk_new * a_new
    g_inv = jnp.exp(-cum)
    g_tail = jnp.exp(tot - cum)
    folded = (-kk_new * jnp.exp(cum - lw), rr * jnp.exp(cum), b_new * g_inv, k_new * g_inv,
              b_new * g_tail, k_new * g_tail, v_new)
    for idx, val in enumerate(folded):
        opb_scr[parity, idx] = val.astype(bf16)
    for idx, val in enumerate((jnp.exp(tot), g_new, rk_sum * v_new)):
        opf_scr[parity, idx] = val

    @pl.when(c == n_chunk - 1)
    def _():
        def unpack(g, carry):
            for p in pairs:
                tile = s_scr[p, pl.ds(pl.multiple_of(g * PAIR, PAIR), PAIR), :]
                sl_ref[g, 2 * p] = tile[:HEAD_B, :HEAD_B]
                sl_ref[g, 2 * p + 1] = tile[HEAD_B:, HEAD_B:]
            return carry

        lax.fori_loop(0, n_seq, unpack, 0)


def _time_mix(ps, first, vextra, wts, s0, prevs, emit_v, layer, depth, n_seq_total, slen, row0, casts=()):
    n = ps.shape[0]
    long_seq = slen >= SCAN_ROWS
    if long_seq:
        n_seq, clen, n_chunk, n_outer = 1, SCAN_ROWS, slen // SCAN_ROWS, n_seq_total
    else:
        n_seq, clen, n_chunk, n_outer = SCAN_ROWS // slen, slen, 1, n_seq_total * slen // SCAN_ROWS
    blk0 = row0 // SCAN_ROWS
    n_blocks = n_outer * n_chunk
    prep = lambda j: jnp.minimum(j, n_blocks - 1)
    run = lambda j: jnp.maximum(j - 1, 0)
    tok_prep = lambda w: pl.BlockSpec((SCAN_ROWS, w), lambda j: (blk0 + prep(j), 0))
    tok_run = lambda w: pl.BlockSpec((SCAN_ROWS, w), lambda j: (blk0 + run(j), 0))
    st = pl.BlockSpec((None, n_seq, H_B, HEAD_B, HEAD_B), lambda j: (layer, run(j) // n_chunk, 0, 0, 0))
    sq = pl.BlockSpec((SCAN_ROWS, SCAN_ROWS), lambda j: (0, 0))
    t = jnp.arange(SCAN_ROWS)
    same_seq = t[:, None] // clen == t[None, :] // clen
    tri = (same_seq & (t[None, :] <= t[:, None])).astype(f32)
    in_specs = [tok_prep(N_SHIFT),
                pl.BlockSpec((8, N_SHIFT), lambda j: (jnp.maximum((blk0 + prep(j)) * (SCAN_ROWS // 8) - 1, 0), 0)),
                _const_spec(first.shape) if long_seq else pl.BlockSpec((SCAN_ROWS, N_SHIFT), lambda j: (prep(j), 0))]
    args = [ps, ps, first]
    if vextra is not None:
        in_specs += [tok_prep(D_B), tok_prep(D_B)]
        args += list(vextra)
    in_specs += [_const_spec(w.shape) for w in wts] + [sq, sq]
    args += list(wts) + [tri, same_seq.astype(f32)]
    if s0 is not None:
        in_specs.append(st)
        args.append(s0)
    cast_specs = [pl.BlockSpec((a.shape[0] // n_blocks, a.shape[1]), lambda j: (prep(j), 0)) for a in casts]
    in_specs += cast_specs
    args += list(casts)
    aliases = {}
    for out_idx, prev in enumerate(prevs):
        if prev is not None:
            aliases[len(args)] = out_idx
            in_specs.append(pl.BlockSpec(memory_space=pl.ANY))
            args.append(prev)
    out_specs = [tok_run(D_B)] + ([tok_prep(D_B)] if emit_v else []) + [st] + cast_specs
    out_shape = ([jax.ShapeDtypeStruct((n, D_B), bf16)] * (2 if emit_v else 1)
                 + [jax.ShapeDtypeStruct((depth, n_seq_total, H_B, HEAD_B, HEAD_B), f32)]
                 + [jax.ShapeDtypeStruct(a.shape, bf16) for a in casts])
    state_pad = 2 * n_seq * H_B * HEAD_B * PAIR * 4
    vmem = (2 * SCAN_ROWS * (2 * N_SHIFT * 4 + 6 * D_B * 2) + 2 * state_pad * (2 if s0 is not None else 1)
            + N_PAIR * n_seq * PAIR * PAIR * 4 + 64 * SCAN_ROWS * D_B * 4 + 8 * n_seq * PAIR * PAIR * 4 * 4
            + 8 * MIB)
    return pl.pallas_call(
        functools.partial(_scan_outer, n_seq, clen, long_seq, n_chunk, n_blocks, vextra is not None,
                          s0 is not None, emit_v, len(aliases), len(casts)),
        grid=(n_blocks + 1,), in_specs=in_specs, out_specs=out_specs, out_shape=out_shape,
        scratch_shapes=[pltpu.VMEM((N_PAIR, n_seq * PAIR, PAIR), f32),
                        pltpu.VMEM((2, N_OPS_BF, SCAN_ROWS, D_B), bf16),
                        pltpu.VMEM((2, N_OPS_F32, SCAN_ROWS, D_B), f32)],
        input_output_aliases=aliases,
        compiler_params=_params(vmem, 1), name="time_mix",
    )(*args)


MIX_TM = 256


def _mix_body(u_ref, va_ref, yb_ref, gate_ref, x_ref, sw_ref, sm_ref, sb_ref, wa_ref, wb_ref, wo_ref,
              o_ref, ya_scr):
    tm = u_ref.shape[0]
    for r0 in range(0, tm, CHUNK):
        for grp in range(H_A):
            cs = slice(grp * GROUP_A, (grp + 1) * GROUP_A)
            w = jnp.where(sm_ref[...] > 0.5, sw_ref[grp], 0.0).astype(bf16)
            mixed = _dot(w, va_ref[r0:r0 + CHUNK, cs].astype(bf16)) + sb_ref[:, cs]
            ya_scr[r0:r0 + CHUNK, cs] = (u_ref[r0:r0 + CHUNK, cs] * mixed).astype(bf16)
    merged = (gate_ref[:, :D_MODEL] * _dot(ya_scr[...], wa_ref[...])
              + gate_ref[:, D_MODEL:] * _dot(yb_ref[...], wb_ref[...]))
    o_ref[...] = x_ref[...] + _dot(merged.astype(bf16), wo_ref[...])


def _mix_out(u, va, yb, gates, x, sgu_w2, sgu_m2, sgu_b2, n_long, wa, wb, wo):
    n = x.shape[0]
    tm = MIX_TM
    row = lambda w: pl.BlockSpec((tm, w), lambda i: (i, 0))
    kind = lambda i: jnp.where(i * tm >= n_long, 1, 0)
    in_specs = [row(D_A), row(D_A), row(D_B), row(2 * D_MODEL), row(D_MODEL),
                pl.BlockSpec((None, H_A, CHUNK, CHUNK), lambda i: (kind(i), 0, 0, 0)),
                pl.BlockSpec((None, CHUNK, CHUNK), lambda i: (kind(i), 0, 0)),
                pl.BlockSpec((None, CHUNK, D_A), lambda i: (kind(i), 0, 0)),
                _const_spec(wa.shape), _const_spec(wb.shape), _const_spec(wo.shape)]
    vmem = 2 * tm * (2 * 5 + 4 * 2) * D_MODEL + 3 * 2 * D_MODEL * D_MODEL + 10 * tm * D_MODEL * 4 + 6 * MIB
    return pl.pallas_call(
        _mix_body, grid=(n // tm,), in_specs=in_specs, out_specs=row(D_MODEL),
        out_shape=jax.ShapeDtypeStruct((n, D_MODEL), f32),
        scratch_shapes=[pltpu.VMEM((tm, D_A), bf16)],
        compiler_params=_params(vmem, 1), name="mix_out",
    )(u, va, yb, gates, x, sgu_w2, sgu_m2, sgu_b2, wa, wb, wo)


FFN_TM = 512
FFN_CW = 256


def _ffn_body(x_ref, g_ref, w1_ref, w3_ref, w2_ref, o_ref, h_scr):
    x = x_ref[...]
    h = x * lax.rsqrt(jnp.mean(x * x, axis=-1, keepdims=True) + RMS_EPS) * g_ref[...]
    h_scr[...] = h.astype(bf16)
    o_ref[...] = x
    d_ff = w1_ref.shape[1]
    for c in range(0, d_ff, FFN_CW):
        a = _dot(h_scr[...], w1_ref[:, c:c + FFN_CW])
        b = _dot(h_scr[...], w3_ref[:, c:c + FFN_CW])
        t = (a * _sigmoid(a) * b).astype(bf16)
        o_ref[...] += _dot(t, w2_ref[c:c + FFN_CW, :])


def _ffn_dense(x, g, w1, w3, w2):
    n = x.shape[0]
    tm = FFN_TM
    row = pl.BlockSpec((tm, D_MODEL), lambda i: (i, 0))
    vmem = 3 * 2 * D_MODEL * w1.shape[1] + 4 * tm * D_MODEL * 4 + 8 * tm * D_MODEL * 4 + 4 * MIB
    return pl.pallas_call(
        _ffn_body, grid=(n // tm,),
        in_specs=[row, _const_spec((1, D_MODEL)), _const_spec(w1.shape), _const_spec(w3.shape),
                  _const_spec(w2.shape)],
        out_specs=row, out_shape=jax.ShapeDtypeStruct((n, D_MODEL), f32),
        scratch_shapes=[pltpu.VMEM((tm, D_MODEL), bf16)],
        compiler_params=_params(vmem, 1), name="ffn_dense",
    )(x, g, w1, w3, w2)


MOE_TM = 1408
MOE_BLOCK = 256
MOE_TAIL = 128
MOE_ONE_SHOT = 3
MOE_FC = 2


def _split3(x):
    hi = x.astype(bf16)
    r1 = x - hi.astype(f32)
    mid = r1.astype(bf16)
    lo = (r1 - mid.astype(f32)).astype(bf16)
    return hi, mid, lo


def _router_body(x_ref, g_ref, wr_ref, hb_ref, gate_t_ref, rank_t_ref, rank_c_ref, cnt_ref):
    x = x_ref[...]
    tm = x.shape[0]
    h = x * lax.rsqrt(jnp.mean(x * x, axis=-1, keepdims=True) + RMS_EPS) * g_ref[...]
    hb_ref[...] = h.astype(bf16)
    h_hi, h_mid, h_lo = _split3(h)
    w_hi, w_mid, w_lo = _split3(wr_ref[...])
    logits = (_dot_nt(w_hi, h_hi) + _dot_nt(w_hi, h_mid) + _dot_nt(w_mid, h_hi)
              + _dot_nt(w_hi, h_lo) + _dot_nt(w_mid, h_mid) + _dot_nt(w_lo, h_hi))
    eid = lax.broadcasted_iota(jnp.int32, logits.shape, 0)
    m1 = jnp.max(logits, axis=0, keepdims=True)
    i1 = jnp.min(jnp.where(logits == m1, eid, N_EXPERTS), axis=0, keepdims=True)
    sel1 = eid == i1
    rest = jnp.where(sel1, -jnp.inf, logits)
    m2 = jnp.max(rest, axis=0, keepdims=True)
    i2 = jnp.min(jnp.where(rest == m2, eid, N_EXPERTS), axis=0, keepdims=True)
    sel2 = eid == i2
    e2 = jnp.exp(m2 - m1)
    den = 1.0 + e2
    gate_t = jnp.where(sel1, 1.0 / den, 0.0) + jnp.where(sel2, e2 / den, 0.0)
    sel = jnp.where(sel1 | sel2, 1.0, 0.0)
    s_idx = lax.broadcasted_iota(jnp.int32, (tm, tm), 0)
    t_idx = lax.broadcasted_iota(jnp.int32, (tm, tm), 1)
    before = jnp.where(s_idx < t_idx, 1.0, 0.0).astype(bf16)
    rank = _dot(sel.astype(bf16), before)
    rank_t = jnp.where(sel > 0.5, rank, -1.0)
    gate_t_ref[...] = gate_t
    rank_t_ref[...] = rank_t
    rank_c_ref[...] = rank_t.T
    cnt = jnp.sum(sel, axis=1, keepdims=True)
    cnt_ref[...] = jnp.broadcast_to(cnt, cnt_ref.shape).astype(jnp.int32)


def _router(x, g, wr_t):
    n = x.shape[0]
    tm = MOE_TM
    nt = n // tm
    vmem = 2 * tm * D_MODEL * 6 + 8 * tm * D_MODEL * 4 + 3 * tm * tm * 4 + 8 * MIB
    return pl.pallas_call(
        _router_body, grid=(nt,),
        in_specs=[pl.BlockSpec((tm, D_MODEL), lambda i: (i, 0)), _const_spec((1, D_MODEL)),
                  _const_spec(wr_t.shape)],
        out_specs=[pl.BlockSpec((tm, D_MODEL), lambda i: (i, 0)),
                   pl.BlockSpec((N_EXPERTS, tm), lambda i: (0, i)),
                   pl.BlockSpec((N_EXPERTS, tm), lambda i: (0, i)),
                   pl.BlockSpec((tm, N_EXPERTS), lambda i: (i, 0)),
                   pl.BlockSpec((None, N_EXPERTS, 128), lambda i: (i, 0, 0))],
        out_shape=[jax.ShapeDtypeStruct((n, D_MODEL), bf16),
                   jax.ShapeDtypeStruct((N_EXPERTS, n), f32), jax.ShapeDtypeStruct((N_EXPERTS, n), f32),
                   jax.ShapeDtypeStruct((n, N_EXPERTS), f32),
                   jax.ShapeDtypeStruct((nt, N_EXPERTS, 128), jnp.int32)],
        compiler_params=_params(vmem, 1), name="moe_router",
    )(x, g, wr_t)


MOE_FS = 1792


def _moe_blocks(cnt, fn):
    n_full = cnt // MOE_BLOCK
    rem = cnt - n_full * MOE_BLOCK
    n_big = n_full + jnp.where(rem > MOE_TAIL, 1, 0)
    n_tail = (cnt + MOE_TAIL - 1) // MOE_TAIL
    for k in range(1, MOE_ONE_SHOT + 1):
        @pl.when(n_tail == k)
        def _(k=k):
            fn(0, k * MOE_TAIL)

    @pl.when(n_tail > MOE_ONE_SHOT)
    def _():
        def body(j, carry):
            fn(pl.multiple_of(j * MOE_BLOCK, MOE_BLOCK), MOE_BLOCK)
            return carry

        lax.fori_loop(0, n_big, body, 0)

        @pl.when((rem > 0) & (rem <= MOE_TAIL))
        def _():
            fn(pl.multiple_of(n_full * MOE_BLOCK, MOE_TAIL), MOE_TAIL)


def _slots(r0, rows, axis):
    shape = (rows, 1) if axis == 0 else (1, rows)
    return (lax.broadcasted_iota(jnp.int32, shape, axis) + r0).astype(f32)


def _experts_body(cnt_ref, hb_ref, rank_t_ref, gate_t_ref, w1_ref, w3_ref, w2_ref, ys_ref):
    e, i = pl.program_id(0), pl.program_id(1)
    rank_row = rank_t_ref[pl.ds(e, 1), :]
    gate_row = gate_t_ref[pl.ds(e, 1), :]
    d_ff = w2_ref.shape[0]
    ys_ref[...] = jnp.zeros(ys_ref.shape, bf16)

    def run(r0, rows):
        match = rank_row == _slots(r0, rows, 0)
        xb = _dot(jnp.where(match, 1.0, 0.0).astype(bf16), hb_ref[...]).astype(bf16)
        gate_blk = jnp.sum(jnp.where(match, gate_row, 0.0), axis=1, keepdims=True)
        acc = jnp.zeros((rows, D_MODEL), f32)
        for s in range(0, d_ff, MOE_FS):
            a = _dot(xb, w1_ref[:, s:s + MOE_FS])
            b = _dot(xb, w3_ref[:, s:s + MOE_FS])
            acc = acc + _dot((a * _sigmoid(a) * b).astype(bf16), w2_ref[s:s + MOE_FS, :])
        ys_ref[pl.ds(r0, rows), :] = (acc * gate_blk).astype(bf16)

    _moe_blocks(cnt_ref[i * N_EXPERTS + e], run)


def _experts(cnt, hb, rank_t, gate_t, w1, w3, w2):
    n = hb.shape[0]
    tm = MOE_TM
    n_e, _, d_ff = w1.shape
    once = dict(pipeline_mode=pl.Buffered(1))
    meta = pl.BlockSpec((N_EXPERTS, tm), lambda e, i, cnt: (0, i))
    grid_spec = pltpu.PrefetchScalarGridSpec(
        num_scalar_prefetch=1, grid=(n_e, n // tm),
        in_specs=[pl.BlockSpec((tm, D_MODEL), lambda e, i, cnt: (i, 0)), meta, meta,
                  pl.BlockSpec((None, D_MODEL, d_ff), lambda e, i, cnt: (e, 0, 0), **once),
                  pl.BlockSpec((None, D_MODEL, d_ff), lambda e, i, cnt: (e, 0, 0), **once),
                  pl.BlockSpec((None, d_ff, D_MODEL), lambda e, i, cnt: (e, 0, 0), **once)],
        out_specs=pl.BlockSpec((None, None, tm, D_MODEL), lambda e, i, cnt: (e, i, 0, 0)))
    max_rows = MOE_ONE_SHOT * MOE_TAIL
    vmem = (3 * D_MODEL * d_ff * 2 + 2 * 2 * tm * D_MODEL * 2 + 4 * N_EXPERTS * tm * 4
            + max_rows * (tm * 6 + 3 * MOE_FS * 4 + 3 * D_MODEL * 4) + 6 * MIB)
    return pl.pallas_call(
        _experts_body, grid_spec=grid_spec,
        out_shape=jax.ShapeDtypeStruct((n_e, n // tm, tm, D_MODEL), bf16),
        compiler_params=_params(vmem, 2), name="moe_experts",
    )(cnt, hb, rank_t, gate_t, w1, w3, w2)


def _combine_body(cnt_ref, ys_ref, rank_c_ref, o_ref):
    i, e = pl.program_id(0), pl.program_id(1)
    tm = o_ref.shape[0]

    @pl.when(e == 0)
    def _():
        o_ref[...] = jnp.zeros(o_ref.shape, f32)

    lane8 = lax.broadcasted_iota(jnp.int32, (tm, N_EXPERTS), 1)
    rank_col = jnp.sum(jnp.where(lane8 == e, rank_c_ref[...], 0.0), axis=1, keepdims=True)

    def scatter(r0, rows):
        onehot = jnp.where(rank_col == _slots(r0, rows, 1), 1.0, 0.0).astype(bf16)
        o_ref[...] += _dot(onehot, ys_ref[pl.ds(r0, rows), :])

    _moe_blocks(cnt_ref[i * N_EXPERTS + e], scatter)


def _combine(cnt, ys, rank_c):
    n_e, nt, tm, _ = ys.shape
    grid_spec = pltpu.PrefetchScalarGridSpec(
        num_scalar_prefetch=1, grid=(nt, n_e),
        in_specs=[pl.BlockSpec((None, None, tm, D_MODEL), lambda i, e, cnt: (e, i, 0, 0)),
                  pl.BlockSpec((tm, N_EXPERTS), lambda i, e, cnt: (i, 0))],
        out_specs=pl.BlockSpec((tm, D_MODEL), lambda i, e, cnt: (i, 0)))
    vmem = 2 * tm * D_MODEL * (2 + 4) + 2 * tm * 128 * 4 + 3 * tm * D_MODEL * 4 + 4 * MIB
    return pl.pallas_call(
        _combine_body, grid_spec=grid_spec, out_shape=jax.ShapeDtypeStruct((nt * tm, D_MODEL), f32),
        compiler_params=_params(vmem, 2), name="moe_combine",
    )(cnt, ys, rank_c)


def _moe_body(cnt_ref, hb_ref, rank_t_ref, gate_t_ref, rank_c_ref, w1_ref, w3_ref, w2_ref,
              o_ref, xg_scr, acc_scr):
    i, e, c = pl.program_id(0), pl.program_id(1), pl.program_id(2)
    tm = hb_ref.shape[0]
    cnt = cnt_ref[i * N_EXPERTS + e]
    n_full = cnt // MOE_BLOCK
    rem = cnt - n_full * MOE_BLOCK
    n_big = n_full + jnp.where(rem > MOE_TAIL, 1, 0)
    short_tail = (rem > 0) & (rem <= MOE_TAIL)

    n_tail = (cnt + MOE_TAIL - 1) // MOE_TAIL

    def for_blocks(fn):
        for k in range(1, MOE_ONE_SHOT + 1):
            @pl.when(n_tail == k)
            def _(k=k):
                fn(0, k * MOE_TAIL)

        @pl.when(n_tail > MOE_ONE_SHOT)
        def _():
            def body(j, carry):
                fn(pl.multiple_of(j * MOE_BLOCK, MOE_BLOCK), MOE_BLOCK)
                return carry

            lax.fori_loop(0, n_big, body, 0)

            @pl.when(short_tail)
            def _():
                fn(pl.multiple_of(n_full * MOE_BLOCK, MOE_TAIL), MOE_TAIL)

    def slots(r0, rows, axis):
        shape = (rows, 1) if axis == 0 else (1, rows)
        return (lax.broadcasted_iota(jnp.int32, shape, axis) + r0).astype(f32)

    @pl.when((e == 0) & (c == 0))
    def _():
        o_ref[...] = jnp.zeros(o_ref.shape, f32)

    @pl.when(c == 0)
    def _():
        rank_row = rank_t_ref[pl.ds(e, 1), :]

        def gather(r0, rows):
            onehot = jnp.where(rank_row == slots(r0, rows, 0), 1.0, 0.0).astype(bf16)
            xg_scr[pl.ds(r0, rows), :] = _dot(onehot, hb_ref[...]).astype(bf16)
            acc_scr[pl.ds(r0, rows), :] = jnp.zeros((rows, D_MODEL), f32)

        for_blocks(gather)

    def expert(r0, rows):
        xb = xg_scr[pl.ds(r0, rows), :]
        a = _dot(xb, w1_ref[...])
        b = _dot(xb, w3_ref[...])
        t = (a * _sigmoid(a) * b).astype(bf16)
        acc_scr[pl.ds(r0, rows), :] += _dot(t, w2_ref[...])

    for_blocks(expert)

    @pl.when(c == pl.num_programs(2) - 1)
    def _():
        lane8 = lax.broadcasted_iota(jnp.int32, (tm, N_EXPERTS), 1)
        rank_col = jnp.sum(jnp.where(lane8 == e, rank_c_ref[...], 0.0), axis=1, keepdims=True)
        rank_row = rank_t_ref[pl.ds(e, 1), :]
        gate_row = gate_t_ref[pl.ds(e, 1), :]

        def scatter(r0, rows):
            gate_blk = jnp.sum(jnp.where(rank_row == slots(r0, rows, 0), gate_row, 0.0), axis=1, keepdims=True)
            onehot = jnp.where(rank_col == slots(r0, rows, 1), 1.0, 0.0).astype(bf16)
            scaled = (acc_scr[pl.ds(r0, rows), :] * gate_blk).astype(bf16)
            o_ref[...] += _dot(onehot, scaled)

        for_blocks(scatter)


def _moe(cnt, hb, rank_t, gate_t, rank_c, w1, w3, w2):
    n = hb.shape[0]
    tm = MOE_TM
    fw = w2.shape[1] // MOE_FC
    tile = lambda i, e, c, cnt: (i, 0)
    once = dict(pipeline_mode=pl.Buffered(1))
    grid_spec = pltpu.PrefetchScalarGridSpec(
        num_scalar_prefetch=1, grid=(n // tm, N_EXPERTS, MOE_FC),
        in_specs=[pl.BlockSpec((tm, D_MODEL), tile),
                  pl.BlockSpec((N_EXPERTS, tm), lambda i, e, c, cnt: (0, i), **once),
                  pl.BlockSpec((N_EXPERTS, tm), lambda i, e, c, cnt: (0, i), **once),
                  pl.BlockSpec((tm, N_EXPERTS), tile, **once),
                  pl.BlockSpec((None, D_MODEL, fw), lambda i, e, c, cnt: (e, 0, c)),
                  pl.BlockSpec((None, D_MODEL, fw), lambda i, e, c, cnt: (e, 0, c)),
                  pl.BlockSpec((None, fw, D_MODEL), lambda i, e, c, cnt: (e, c, 0))],
        out_specs=pl.BlockSpec((tm, D_MODEL), tile),
        scratch_shapes=[pltpu.VMEM((tm, D_MODEL), bf16), pltpu.VMEM((tm, D_MODEL), f32)])
    vmem = (tm * D_MODEL * (2 + 2 * 4) + tm * D_MODEL * 6 + 2 * 3 * D_MODEL * fw * 2
            + 3 * tm * 128 * 4 + 3 * tm * D_MODEL * 4 + 4 * MIB)
    return pl.pallas_call(
        _moe_body, grid_spec=grid_spec, out_shape=jax.ShapeDtypeStruct((n, D_MODEL), f32),
        compiler_params=_params(vmem, 3), name="moe_experts",
    )(cnt, hb, rank_t, gate_t, rank_c, w1, w3, w2)


NORM_TM = 512


def _norm_body(n_long_tiles, x_ref, y_ref, g_ref, o_long_ref, o_short_ref):
    x = x_ref[...] + y_ref[...]
    val = x * lax.rsqrt(jnp.mean(x * x, axis=-1, keepdims=True) + RMS_EPS) * g_ref[...]
    i = pl.program_id(0)

    @pl.when(i < n_long_tiles)
    def _():
        o_long_ref[...] = val

    @pl.when(i >= n_long_tiles)
    def _():
        o_short_ref[...] = val


def _add_norm(x, y, g, n_long):
    n = x.shape[0]
    tm = NORM_TM
    nl = n_long // tm
    row = pl.BlockSpec((tm, D_MODEL), lambda i: (i, 0))
    return pl.pallas_call(
        functools.partial(_norm_body, nl), grid=(n // tm,),
        in_specs=[row, row, _const_spec((1, D_MODEL))],
        out_specs=[pl.BlockSpec((tm, D_MODEL), lambda i: (jnp.minimum(i, nl - 1), 0)),
                   pl.BlockSpec((tm, D_MODEL), lambda i: (jnp.maximum(i - nl, 0), 0))],
        out_shape=[jax.ShapeDtypeStruct((n_long, D_MODEL), f32),
                   jax.ShapeDtypeStruct((n - n_long, D_MODEL), f32)],
        compiler_params=_params(12 * tm * D_MODEL * 4, 1), name="final_norm",
    )(x, y, g)


def kernel(x_prompt, x_sample, state_wkv, state_shift, norm_mix_g, w_in, shift_mu, sgu_ln_g, sgu_ln_b, sgu_w, sgu_b, rwkv_w0, rwkv_w2, rwkv_a0, rwkv_a2, rwkv_g2, rwkv_v0, rwkv_v1, rwkv_v2, rwkv_k_k, rwkv_k_a, rwkv_r_k, rwkv_ln_g, rwkv_ln_b, w_branch_a, w_branch_b, w_out, norm_ffn_g, ffn_w1, ffn_w3, ffn_w2, moe_router, moe_w1, moe_w3, moe_w2, norm_final_g):
    bp, tp, d = x_prompt.shape
    bs, ts, _ = x_sample.shape
    depth = w_in.shape[0]
    n_p, n_s = bp * tp, bs * ts
    x = jnp.concatenate([x_prompt.reshape(n_p, d), x_sample.reshape(n_s, d)], axis=0)
    row = lambda a: a.reshape(1, -1)

    hid = jnp.arange(4 * HEAD_B) // HEAD_B
    bd = (hid[:, None] == hid[None, :]).astype(bf16)
    tpos = jnp.arange(CHUNK)
    mask_long = (tpos[None, :] <= tpos[:, None]).astype(f32)
    mask_short = ((tpos[None, :] // ts == tpos[:, None] // ts) & (tpos[None, :] <= tpos[:, None])).astype(f32)
    sgu_mask = jnp.stack([mask_long, mask_short])
    zero_shift = jnp.zeros((bp, N_SHIFT), f32)

    shift_p, shift_s, chunk_v = [], [], []
    wkv_p = wkv_s = v_first = None
    for l in range(depth):
        vparams = None
        if l > 0:
            vparams = (rwkv_v1[l - 1].astype(bf16), rwkv_v2[l - 1].astype(bf16), row(rwkv_v0[l - 1]))
        outs = _in_proj(x, n_p, row(norm_mix_g[l]), w_in[l].astype(bf16), row(sgu_ln_g[l]), row(sgu_ln_b[l]),
                        vparams)
        u, va, va_s, ps, tails, gates = outs[:6]
        tiles_per_seq = tp // IN_TM
        shift_p.append(tails[tiles_per_seq - 1:bp * tiles_per_seq:tiles_per_seq, 7])
        shift_s.append(ps[n_p + ts - 1::ts])
        chunk_v.append(va_s.reshape(bs, ts, D_A))

        zpad = jnp.zeros((LORA_W, D_B), f32)
        wa2 = jnp.concatenate([jnp.concatenate([rwkv_w2[l], zpad], axis=1),
                               jnp.concatenate([zpad, rwkv_a2[l]], axis=1)], axis=0).astype(bf16)
        wts = (row(shift_mu[l]), wa2, row(rwkv_w0[l]), row(rwkv_a0[l]), rwkv_g2[l].astype(bf16),
               row(rwkv_k_k[l]), row(rwkv_k_a[l]), row(rwkv_r_k[l]), bd, row(rwkv_ln_g[l]), row(rwkv_ln_b[l]))
        first_s = jnp.pad(state_shift[l][:, None, :], ((0, 0), (0, ts - 1), (0, 0))).reshape(n_s, N_SHIFT)
        vextra = (outs[6], v_first) if l > 0 else None
        if l == 0:
            n_blk = n_p // SCAN_ROWS
            moe_2d = [w.reshape(-1, w.shape[-1]) for w in (moe_w1, moe_w3, moe_w2)]
            ride = all(w.shape[0] % (16 * n_blk) == 0 for w in moe_2d)
            yb, v_first, wkv_p, *moe_bf = _time_mix(ps, zero_shift, vextra, wts, None, (None, None, wkv_p), True,
                                                    l, depth, bp, tp, 0, casts=moe_2d if ride else ())
            if not ride:
                moe_bf = [w.astype(bf16) for w in moe_2d]
            moe_bf = [w.reshape(s.shape) for w, s in zip(moe_bf, (moe_w1, moe_w3, moe_w2))]
            yb, v_first, wkv_s = _time_mix(ps, first_s, vextra, wts, state_wkv, (yb, v_first, wkv_s), True,
                                           l, depth, bs, ts, n_p)
        else:
            yb, wkv_p = _time_mix(ps, zero_shift, vextra, wts, None, (None, wkv_p), False, l, depth, bp, tp, 0)
            yb, wkv_s = _time_mix(ps, first_s, vextra, wts, state_wkv, (yb, wkv_s), False, l, depth, bs, ts, n_p)

        w_short = jnp.tile(sgu_w[l][:, :ts, :ts], (1, CHUNK // ts, CHUNK // ts))
        sgu_w2 = jnp.stack([sgu_w[l], w_short])
        b_long = jnp.repeat(sgu_b[l].T, GROUP_A, axis=1)
        b_short = jnp.tile(b_long[:ts], (CHUNK // ts, 1))
        sgu_b2 = jnp.stack([b_long, b_short])
        x = _mix_out(u, va, yb, gates, x, sgu_w2, sgu_mask, sgu_b2, n_p,
                     w_branch_a[l].astype(bf16), w_branch_b[l].astype(bf16), w_out[l].astype(bf16))

        j = l // 2
        if l % 2 == 0:
            x = _ffn_dense(x, row(norm_ffn_g[l]), ffn_w1[j].astype(bf16), ffn_w3[j].astype(bf16),
                           ffn_w2[j].astype(bf16))
            delta = None
        else:
            hb, gate_t, rank_t, rank_c, cnt = _router(x, row(norm_ffn_g[l]), moe_router[j].T)
            cnt = cnt[:, :, 0].reshape(-1)
            ys = _experts(cnt, hb, rank_t, gate_t, moe_bf[0][j], moe_bf[1][j], moe_bf[2][j])
            delta = _combine(cnt, ys, rank_c)
            if l < depth - 1:
                x = x + delta
                delta = None

    yn_p, yn_s = _add_norm(x, jnp.zeros_like(x) if delta is None else delta, row(norm_final_g), n_p)
    return (yn_p.reshape(bp, tp, d), yn_s.reshape(bs, ts, d),
            wkv_p, jnp.stack(shift_p), wkv_s, jnp.stack(shift_s),
            jnp.stack(chunk_v))
```

```python
import functools

import jax
import jax.numpy as jnp
from jax import lax
from jax.experimental import pallas as pl
from jax.experimental.pallas import tpu as pltpu

f32 = jnp.float32
bf16 = jnp.bfloat16

D_MODEL = 1024
CHUNK = 128
D_A = D_MODEL
GROUP_A = 128
H_A = D_A // GROUP_A
D_B = D_MODEL
HEAD_B = 64
H_B = D_B // HEAD_B
LORA_W = 64
LORA_A = 64
LORA_G = 128
N_SHIFT = 3 * D_B + LORA_W + LORA_A + LORA_G
N_IN = 2 * D_A + N_SHIFT + 2 * D_MODEL
N_EXPERTS = 8
RMS_EPS = 1e-6
LN_EPS = 1e-5
GN_EPS = 64e-5

PAIR = 2 * HEAD_B
N_PAIR = H_B // 2
MIB = 1024 * 1024
VMEM_CAP_V7X = 56 * MIB


def _params(vmem_bytes, n_grid):
    return pltpu.CompilerParams(
        dimension_semantics=("arbitrary",) * n_grid,
        vmem_limit_bytes=int(min(max(vmem_bytes, 16 * MIB), VMEM_CAP_V7X)),
    )


def _const_spec(shape):
    nd = len(shape)
    return pl.BlockSpec(shape, lambda *_: (0,) * nd, pipeline_mode=pl.Buffered(1))


def _gelu(x):
    return 0.5 * x * (1.0 + lax.erf(x * (2.0 ** -0.5)))


def _sigmoid(x):
    return 1.0 / (1.0 + jnp.exp(-x))


def _dot(a, b):
    return jnp.dot(a, b, preferred_element_type=f32)


def _dot_nt(a, b):
    return lax.dot_general(a, b, (((1,), (1,)), ((), ())), preferred_element_type=f32)


def _dot_tn(a, b):
    return lax.dot_general(a, b, (((0,), (0,)), ((), ())), preferred_element_type=f32)


def _head_sum(x, bd_ref):
    w = bd_ref.shape[0]
    cols = [_dot(x[:, c:c + w].astype(bf16), bd_ref[...]) for c in range(0, x.shape[1], w)]
    return jnp.concatenate(cols, axis=1)


IN_TM = 512
IN_CW = 512


def _in_proj_body(n_long_tiles, has_v, x_ref, g_ref, w_ref, lng_ref, lnb_ref, *rest):
    if has_v:
        v1_ref, v2_ref, v0_ref, u_ref, va_ref, vs_ref, ps_ref, tail_ref, gate_ref, vg_ref, h_scr, t_scr = rest
    else:
        u_ref, va_ref, vs_ref, ps_ref, tail_ref, gate_ref, h_scr, t_scr = rest
    x = x_ref[...]
    tm = x.shape[0]
    h = x * lax.rsqrt(jnp.mean(x * x, axis=-1, keepdims=True) + RMS_EPS) * g_ref[...]
    h_scr[...] = h.astype(bf16)

    def mm(c0, c1):
        return _dot(h_scr[...], w_ref[:, c0:c1])

    for c in range(0, D_A, IN_CW):
        u_ref[:, c:c + IN_CW] = _gelu(mm(c, c + IN_CW)).astype(bf16)
    for c in range(0, D_A, IN_CW):
        t_scr[:, c:c + IN_CW] = _gelu(mm(D_A + c, D_A + c + IN_CW))
    t = t_scr[...]
    mu = jnp.mean(t, axis=-1, keepdims=True)
    d = t - mu
    var = jnp.mean(d * d, axis=-1, keepdims=True)
    va = d * lax.rsqrt(var + LN_EPS) * lng_ref[...] + lnb_ref[...]
    va_ref[...] = va.astype(bf16)

    @pl.when(pl.program_id(0) >= n_long_tiles)
    def _():
        vs_ref[...] = va

    for c in range(0, N_SHIFT, IN_CW):
        c1 = min(c + IN_CW, N_SHIFT)
        ps_ref[:, c:c1] = mm(2 * D_A + c, 2 * D_A + c1)
    tail_ref[...] = ps_ref[tm - 8:tm, :]
    base = 2 * D_A + N_SHIFT
    for c in range(0, 2 * D_MODEL, IN_CW):
        gate_ref[:, c:c + IN_CW] = _sigmoid(mm(base + c, base + c + IN_CW)).astype(bf16)
    if has_v:
        lv = _dot(h_scr[...], v1_ref[...])
        vg_ref[...] = _sigmoid(v0_ref[...] + _dot(lv.astype(bf16), v2_ref[...])).astype(bf16)


def _in_proj(x, n_long, g, w_in, lng, lnb, vparams):
    n = x.shape[0]
    tm = IN_TM
    nl = n_long // tm
    has_v = vparams is not None
    row = lambda w: pl.BlockSpec((tm, w), lambda i: (i, 0))
    in_specs = [row(D_MODEL), _const_spec((1, D_MODEL)), _const_spec((D_MODEL, N_IN)),
                _const_spec((1, D_A)), _const_spec((1, D_A))]
    args = [x, g, w_in, lng, lnb]
    out_shape = [jax.ShapeDtypeStruct((n, D_A), bf16), jax.ShapeDtypeStruct((n, D_A), bf16),
                 jax.ShapeDtypeStruct((n - n_long, D_A), f32),
                 jax.ShapeDtypeStruct((n, N_SHIFT), f32), jax.ShapeDtypeStruct((n // tm, 8, N_SHIFT), f32),
                 jax.ShapeDtypeStruct((n, 2 * D_MODEL), bf16)]
    out_specs = [row(D_A), row(D_A), pl.BlockSpec((tm, D_A), lambda i: (jnp.maximum(i - nl, 0), 0)),
                 row(N_SHIFT), pl.BlockSpec((None, 8, N_SHIFT), lambda i: (i, 0, 0)), row(2 * D_MODEL)]
    if has_v:
        v1, v2, v0 = vparams
        in_specs += [_const_spec(v1.shape), _const_spec(v2.shape), _const_spec((1, D_B))]
        args += [v1, v2, v0]
        out_shape.append(jax.ShapeDtypeStruct((n, D_B), bf16))
        out_specs.append(row(D_B))
    out_bytes = 2 * (2 * D_A + 2 * D_MODEL + (D_B if has_v else 0)) + 4 * (N_SHIFT + D_A)
    vmem = 2 * D_MODEL * N_IN + 2 * tm * (4 * D_MODEL + out_bytes) + 8 * tm * D_MODEL * 4 + 4 * MIB
    return pl.pallas_call(
        functools.partial(_in_proj_body, nl, has_v),
        grid=(n // tm,), in_specs=in_specs, out_specs=out_specs, out_shape=out_shape,
        scratch_shapes=[pltpu.VMEM((tm, D_MODEL), bf16), pltpu.VMEM((tm, D_A), f32)],
        compiler_params=_params(vmem, 1), name="in_proj",
    )(*args)


N_MIX_W = 11
SCAN_ROWS = 64
TIME_MIX_PARTS = 2
N_OPS_BF, N_OPS_F32 = 7, 3


def _tmq_outer(*args):
    for parity in (0, 1):
        @pl.when(pl.program_id(0) % 2 == parity)
        def _(parity=parity):
            _tmq_body(parity, *args)


def _tmq_body(parity, n_par, n_seq, slen, long_seq, n_chunk, n_blocks, has_v, has_s0, emit_v, n_alias, n_cast,
              *refs):
    take = lambda k: (refs[:k], refs[k:])
    ps_refs, refs = take(n_par)
    prevblk_refs, refs = take(n_par)
    (first_ref,), refs = take(1)
    vg_refs = vf_refs = None
    if has_v:
        vg_refs, refs = take(n_par)
        vf_refs, refs = take(n_par)
    (mu_ref, wa2_ref, w0_ref, a0_ref, g2_ref, kk_ref, ka_ref, rk_ref, bd_ref, lng_ref, lnb_ref), refs = take(N_MIX_W)
    (tri_ref, tot_ref), refs = take(2)
    if has_s0:
        (s0_ref,), refs = take(1)
    cast_src, refs = take(n_cast)
    refs = refs[n_alias:]
    (yb_ref,), refs = take(1)
    if emit_v:
        (vo_ref,), refs = take(1)
    (sl_ref,), refs = take(1)
    cast_dst, (s_scr, opb_scr, opf_scr) = take(n_cast)
    for src, dst in zip(cast_src, cast_dst):
        dst[...] = src[...].astype(bf16)
    j = pl.program_id(0)
    blk_prep = jnp.minimum(j, n_blocks - 1)
    c = jnp.maximum(j - 1, 0) % n_chunk
    par = range(n_par)
    pairs = range(N_PAIR)

    if parity == 0:
        @pl.when(j == 0)
        def _():
            opb_scr[1] = jnp.zeros(opb_scr.shape[1:], bf16)
            opf_scr[1] = jnp.zeros(opf_scr.shape[1:], f32)

    @pl.when(c == 0)
    def _():
        if has_s0:
            zero = jnp.zeros((HEAD_B, HEAD_B), f32)
            for q in par:
                def pack(g, carry, q=q):
                    for p in pairs:
                        top = jnp.concatenate([s0_ref[q, g, 2 * p], zero], axis=1)
                        bot = jnp.concatenate([zero, s0_ref[q, g, 2 * p + 1]], axis=1)
                        s_scr[q, p, pl.ds(pl.multiple_of(g * PAIR, PAIR), PAIR), :] = (
                            jnp.concatenate([top, bot], axis=0))
                    return carry

                lax.fori_loop(0, n_seq, pack, 0)
        else:
            s_scr[...] = jnp.zeros(s_scr.shape, f32)

    row = lax.broadcasted_iota(jnp.int32, (SCAN_ROWS, 1), 0)
    nxt = [dict() for _ in par]
    for q in par:
        ps = ps_refs[q][...]
        if long_seq:
            seq = q * (n_blocks // n_chunk) + blk_prep // n_chunk
            init = jnp.where(blk_prep % n_chunk == 0, first_ref[pl.ds(seq, 1), :], prevblk_refs[q][7:8, :])
            starts = row == 0
        else:
            init = first_ref[...]
            starts = row % slen == 0
        prev = jnp.where(starts, init, pltpu.roll(ps, 1, 0))
        xs = ps + mu_ref[...] * (prev - ps)
        dwa = xs[:, 3 * D_B:3 * D_B + LORA_W + LORA_A]
        dg = xs[:, 3 * D_B + LORA_W + LORA_A:N_SHIFT]
        lane = lax.broadcasted_iota(jnp.int32, dwa.shape, 1)
        lhs = jnp.where(lane < LORA_W, jnp.tanh(dwa), dwa).astype(bf16)
        kkr = xs[:, D_B:2 * D_B] * kk_ref[...]
        nxt[q].update(r=xs[:, 0:D_B], k_raw=xs[:, D_B:2 * D_B], v=xs[:, 2 * D_B:3 * D_B],
                      wa=_dot(lhs, wa2_ref[...]),
                      g=_dot(_sigmoid(dg).astype(bf16), g2_ref[...]),
                      kkr=kkr, kk_norm2=_head_sum(kkr * kkr, bd_ref))

    ops = [[opb_scr[1 - parity, q, k] for k in range(N_OPS_BF)] + [opf_scr[1 - parity, q, k] for k in range(N_OPS_F32)]
           for q in par]

    n_st = 2 * SCAN_ROWS

    def causal(width, inclusive):
        ri = lax.broadcasted_iota(jnp.int32, (n_st, width), 0)
        ci = lax.broadcasted_iota(jnp.int32, (n_st, width), 1) % n_st
        same = (ri // SCAN_ROWS == ci // SCAN_ROWS) & ((ri % SCAN_ROWS) // slen == (ci % SCAN_ROWS) // slen)
        t_r, t_c = ri % SCAN_ROWS, ci % SCAN_ROWS
        return same & ((t_c <= t_r) if inclusive else (t_c < t_r))

    incl2 = causal(2 * n_st, True)
    strict = causal(n_st, False)
    lane_e = lax.broadcasted_iota(jnp.int32, (1, PAIR), 1) < HEAD_B
    row_seq = ((lax.broadcasted_iota(jnp.int32, (2 * n_st, 1), 0) % SCAN_ROWS) // slen)

    def stack(z):
        return jnp.concatenate([jnp.where(lane_e, z, 0.0), jnp.where(lane_e, 0.0, z)], axis=0).astype(bf16)

    def pick_seq(big):
        if n_seq == 1:
            return big
        acc = jnp.where(row_seq == 0, big[:, :PAIR], 0.0)
        for g in range(1, n_seq):
            acc = acc + jnp.where(row_seq == g, big[:, g * PAIR:(g + 1) * PAIR], 0.0)
        return acc

    def spread_seq(z):
        if n_seq == 1:
            return z
        return jnp.concatenate([jnp.where(row_seq == g, z, jnp.zeros_like(z)) for g in range(n_seq)], axis=1)

    sls = [slice(p * PAIR, (p + 1) * PAIR) for p in pairs]
    chains = [(q, p) for q in par for p in pairs]
    ar, bk, vb = {}, {}, {}
    for q, p in chains:
        at, rt, bt, kt, _, _, vv = ops[q][:N_OPS_BF]
        s = sls[p]
        ar[q, p] = jnp.concatenate([stack(at[:, s]), stack(rt[:, s])], axis=0)
        bk[q, p] = jnp.concatenate([bt[:, s], bt[:, s], kt[:, s], kt[:, s]], axis=0).astype(bf16)
        vb[q, p] = stack(vv[:, s])
    pm = {ch: _dot_nt(ar[ch], bk[ch]) for ch in chains}
    qs = {(q, p): pick_seq(_dot_nt(ar[q, p], s_scr[q, p].astype(bf16))) for q, p in chains}
    x = {ch: qs[ch][:n_st] + _dot(jnp.where(strict, pm[ch][:n_st, n_st:], 0.0).astype(bf16), vb[ch])
         for ch in chains}
    m = {ch: jnp.where(strict, pm[ch][:n_st, :n_st], 0.0).astype(bf16) for ch in chains}

    for q in par:
        d = nxt[q]
        w_log = -jax.nn.softplus(-(w0_ref[...] + d["wa"][:, :D_B])) - 0.5
        lw = -jnp.exp(w_log)
        a_new = _sigmoid(a0_ref[...] + d["wa"][:, D_B:])
        k_new = d["k_raw"] * (1.0 + (a_new - 1.0) * ka_ref[...])
        v_mix = d["v"]
        if has_v:
            v_mix = v_mix + (vf_refs[q][...].astype(f32) - v_mix) * vg_refs[q][...].astype(f32)
        if emit_v:
            vo_ref[q] = v_mix.astype(bf16)
        d.update(lw=lw, a=a_new, k=k_new, v=v_mix,
                 cum=jnp.dot(tri_ref[...], lw, precision=lax.Precision.HIGHEST, preferred_element_type=f32),
                 tot=jnp.dot(tot_ref[...], lw, precision=lax.Precision.HIGHEST, preferred_element_type=f32),
                 rk_sum=_head_sum(d["r"] * k_new * rk_ref[...], bd_ref))

    span = 1
    while span < slen:
        span *= 2
        if span < slen:
            z = {ch: _dot(m[ch], jnp.concatenate([x[ch].astype(bf16), m[ch]], axis=1)) for ch in chains}
            x = {ch: x[ch] + z[ch][:, :n_st] for ch in chains}
            m = {ch: z[ch][:, n_st:].astype(bf16) for ch in chains}
        else:
            x = {ch: x[ch] + _dot(m[ch], x[ch].astype(bf16)) for ch in chains}
    uv = {ch: jnp.concatenate([x[ch].astype(bf16), vb[ch]], axis=0) for ch in chains}
    ys = {}
    for ch in chains:
        yp = qs[ch][n_st:] + _dot(jnp.where(incl2, pm[ch][n_st:], 0.0).astype(bf16), uv[ch])
        ys[ch] = yp[:SCAN_ROWS] + yp[SCAN_ROWS:]
    for q in par:
        g_last, gg, bonus = ops[q][N_OPS_BF:]
        bh, kh = ops[q][4], ops[q][5]
        y = jnp.concatenate([ys[q, p] for p in pairs], axis=1)
        mean = _head_sum(y, bd_ref) * (1.0 / HEAD_B)
        dev = y - mean
        var = _head_sum(dev * dev, bd_ref) * (1.0 / HEAD_B)
        yn = dev * lax.rsqrt(var + GN_EPS) * lng_ref[...] + lnb_ref[...] + bonus
        yb_ref[q] = (yn * gg).astype(bf16)
        for p in pairs:
            bkh = jnp.concatenate([stack(bh[:, sls[p]]), stack(kh[:, sls[p]])], axis=0)
            upd = _dot_tn(spread_seq(uv[q, p]), bkh)
            for g in range(n_seq):
                rows = slice(g * PAIR, (g + 1) * PAIR)
                s_scr[q, p, rows, :] = (s_scr[q, p, rows, :] * g_last[g * slen:g * slen + 1, sls[p]] + upd[rows])

    for q in par:
        d = nxt[q]
        kk_new = d["kkr"] * lax.rsqrt(d["kk_norm2"] + 1e-12)
        b_new = kk_new * d["a"]
        cum, tot, lw = d["cum"], d["tot"], d["lw"]
        g_inv = jnp.exp(-cum)
        g_tail = jnp.exp(tot - cum)
        folded = (-kk_new * jnp.exp(cum - lw), d["r"] * jnp.exp(cum), b_new * g_inv, d["k"] * g_inv,
                  b_new * g_tail, d["k"] * g_tail, d["v"])
        for idx, val in enumerate(folded):
            opb_scr[parity, q, idx] = val.astype(bf16)
        for idx, val in enumerate((jnp.exp(tot), d["g"], d["rk_sum"] * d["v"])):
            opf_scr[parity, q, idx] = val

    @pl.when(c == n_chunk - 1)
    def _():
        for q in par:
            def unpack(g, carry, q=q):
                for p in pairs:
                    tile = s_scr[q, p, pl.ds(pl.multiple_of(g * PAIR, PAIR), PAIR), :]
                    sl_ref[q, g, 2 * p] = tile[:HEAD_B, :HEAD_B]
                    sl_ref[q, g, 2 * p + 1] = tile[HEAD_B:, HEAD_B:]
                return carry

            lax.fori_loop(0, n_seq, unpack, 0)


def _time_mix_q(ps, row0, first, vextra, wts, s0, s_prev, emit_v, layer, depth, n_seq_total, slen, n_par, casts=()):
    long_seq = slen >= SCAN_ROWS
    n_rows = n_seq_total * slen
    if long_seq:
        n_seq, clen, n_chunk = 1, SCAN_ROWS, slen // SCAN_ROWS
    else:
        n_seq, clen, n_chunk = SCAN_ROWS // slen, slen, 1
    n_blocks = n_rows // SCAN_ROWS // n_par
    seq_per_part = n_seq_total // n_par
    blk0 = row0 // SCAN_ROWS
    prep = lambda j: jnp.minimum(j, n_blocks - 1)
    run = lambda j: jnp.maximum(j - 1, 0)
    part = lambda q, b: q * n_blocks + b
    tok_in = lambda w, q, off: pl.BlockSpec((SCAN_ROWS, w), lambda j: (off + part(q, prep(j)), 0))
    st = pl.BlockSpec((None, n_par, n_seq, H_B, HEAD_B, HEAD_B), lambda j: (layer, 0, run(j) // n_chunk, 0, 0, 0))
    sq = pl.BlockSpec((SCAN_ROWS, SCAN_ROWS), lambda j: (0, 0))
    t = jnp.arange(SCAN_ROWS)
    same_seq = t[:, None] // clen == t[None, :] // clen
    tri = (same_seq & (t[None, :] <= t[:, None])).astype(f32)
    in_specs = [tok_in(N_SHIFT, q, blk0) for q in range(n_par)]
    in_specs += [pl.BlockSpec((8, N_SHIFT), lambda j, q=q: (
        jnp.maximum((blk0 + part(q, prep(j))) * (SCAN_ROWS // 8) - 1, 0), 0)) for q in range(n_par)]
    in_specs.append(_const_spec(first.shape) if long_seq
                    else pl.BlockSpec((SCAN_ROWS, N_SHIFT), lambda j: (prep(j), 0)))
    args = [ps] * (2 * n_par) + [first]
    if vextra is not None:
        in_specs += [tok_in(D_B, q, blk0) for q in range(n_par)] + [tok_in(D_B, q, 0) for q in range(n_par)]
        args += [vextra[0]] * n_par + [vextra[1]] * n_par
    in_specs += [_const_spec(w.shape) for w in wts] + [sq, sq]
    args += list(wts) + [tri, same_seq.astype(f32)]
    grouped = (depth, n_par, seq_per_part, H_B, HEAD_B, HEAD_B)
    if s0 is not None:
        in_specs.append(st)
        args.append(s0.reshape(grouped))
    cast_specs = [pl.BlockSpec((a.shape[0] // n_blocks, a.shape[1]), lambda j: (prep(j), 0)) for a in casts]
    in_specs += cast_specs
    args += list(casts)
    aliases = {}
    if s_prev is not None:
        aliases[len(args)] = 2 if emit_v else 1
        in_specs.append(pl.BlockSpec(memory_space=pl.ANY))
        args.append(s_prev.reshape(grouped))
    out_run = pl.BlockSpec((n_par, SCAN_ROWS, D_B), lambda j: (0, run(j), 0))
    out_prep = pl.BlockSpec((n_par, SCAN_ROWS, D_B), lambda j: (0, prep(j), 0))
    rows3 = jax.ShapeDtypeStruct((n_par, n_rows // n_par, D_B), bf16)
    out_specs = [out_run] + ([out_prep] if emit_v else []) + [st] + cast_specs
    out_shape = ([rows3] * (2 if emit_v else 1) + [jax.ShapeDtypeStruct(grouped, f32)]
                 + [jax.ShapeDtypeStruct(a.shape, bf16) for a in casts])
    state_pad = 2 * n_par * n_seq * H_B * HEAD_B * PAIR * 4
    vmem = (n_par * (2 * SCAN_ROWS * (2 * N_SHIFT * 4 + 6 * D_B * 2) + N_PAIR * n_seq * PAIR * PAIR * 4
                     + 2 * (N_OPS_BF * 2 + N_OPS_F32 * 4) * SCAN_ROWS * D_B + 48 * SCAN_ROWS * D_B * 4)
            + 2 * state_pad * (2 if s0 is not None else 1) + 8 * n_seq * PAIR * PAIR * 4 * 4 + 8 * MIB)
    outs = pl.pallas_call(
        functools.partial(_tmq_outer, n_par, n_seq, clen, long_seq, n_chunk, n_blocks, vextra is not None,
                          s0 is not None, emit_v, len(aliases), len(casts)),
        grid=(n_blocks + 1,), in_specs=in_specs, out_specs=out_specs, out_shape=out_shape,
        scratch_shapes=[pltpu.VMEM((n_par, N_PAIR, n_seq * PAIR, PAIR), f32),
                        pltpu.VMEM((2, n_par, N_OPS_BF, SCAN_ROWS, D_B), bf16),
                        pltpu.VMEM((2, n_par, N_OPS_F32, SCAN_ROWS, D_B), f32)],
        input_output_aliases=aliases,
        compiler_params=_params(vmem, 1), name="time_mix",
    )(*args)
    n_tok = 2 if emit_v else 1
    flat = [o.reshape(n_rows, D_B) for o in outs[:n_tok]]
    states = outs[n_tok].reshape(depth, n_seq_total, H_B, HEAD_B, HEAD_B)
    return (*flat, states, *outs[n_tok + 1:])


MIX_TM = 256


def _mix_body(n_long_tiles, u_ref, va_ref, ybl_ref, ybs_ref, gate_ref, x_ref, sw_ref, sm_ref, sb_ref,
              wa_ref, wb_ref, wo_ref, o_ref, ya_scr):
    tm = u_ref.shape[0]
    yb = jnp.where(pl.program_id(0) < n_long_tiles, ybl_ref[...], ybs_ref[...])
    for r0 in range(0, tm, CHUNK):
        for grp in range(H_A):
            cs = slice(grp * GROUP_A, (grp + 1) * GROUP_A)
            w = jnp.where(sm_ref[...] > 0.5, sw_ref[grp], 0.0).astype(bf16)
            mixed = _dot(w, va_ref[r0:r0 + CHUNK, cs].astype(bf16)) + sb_ref[:, cs]
            ya_scr[r0:r0 + CHUNK, cs] = (u_ref[r0:r0 + CHUNK, cs] * mixed).astype(bf16)
    merged = (gate_ref[:, :D_MODEL] * _dot(ya_scr[...], wa_ref[...])
              + gate_ref[:, D_MODEL:] * _dot(yb, wb_ref[...]))
    o_ref[...] = x_ref[...] + _dot(merged.astype(bf16), wo_ref[...])


def _mix_out(u, va, yb_long, yb_short, gates, x, sgu_w2, sgu_m2, sgu_b2, wa, wb, wo):
    n = x.shape[0]
    tm = MIX_TM
    n_long = yb_long.shape[0]
    nl = n_long // tm
    row = lambda w: pl.BlockSpec((tm, w), lambda i: (i, 0))
    kind = lambda i: jnp.where(i * tm >= n_long, 1, 0)
    in_specs = [row(D_A), row(D_A),
                pl.BlockSpec((tm, D_B), lambda i: (jnp.minimum(i, nl - 1), 0)),
                pl.BlockSpec((tm, D_B), lambda i: (jnp.maximum(i - nl, 0), 0)),
                row(2 * D_MODEL), row(D_MODEL),
                pl.BlockSpec((None, H_A, CHUNK, CHUNK), lambda i: (kind(i), 0, 0, 0)),
                pl.BlockSpec((None, CHUNK, CHUNK), lambda i: (kind(i), 0, 0)),
                pl.BlockSpec((None, CHUNK, D_A), lambda i: (kind(i), 0, 0)),
                _const_spec(wa.shape), _const_spec(wb.shape), _const_spec(wo.shape)]
    vmem = 2 * tm * (2 * 5 + 4 * 2) * D_MODEL + 3 * 2 * D_MODEL * D_MODEL + 10 * tm * D_MODEL * 4 + 6 * MIB
    return pl.pallas_call(
        functools.partial(_mix_body, nl), grid=(n // tm,), in_specs=in_specs, out_specs=row(D_MODEL),
        out_shape=jax.ShapeDtypeStruct((n, D_MODEL), f32),
        scratch_shapes=[pltpu.VMEM((tm, D_A), bf16)],
        compiler_params=_params(vmem, 1), name="mix_out",
    )(u, va, yb_long, yb_short, gates, x, sgu_w2, sgu_m2, sgu_b2, wa, wb, wo)


FFN_TM = 512
FFN_CW = 256


def _ffn_body(x_ref, g_ref, w1_ref, w3_ref, w2_ref, o_ref, h_scr):
    x = x_ref[...]
    h = x * lax.rsqrt(jnp.mean(x * x, axis=-1, keepdims=True) + RMS_EPS) * g_ref[...]
    h_scr[...] = h.astype(bf16)
    o_ref[...] = x
    d_ff = w1_ref.shape[1]
    for c in range(0, d_ff, FFN_CW):
        a = _dot(h_scr[...], w1_ref[:, c:c + FFN_CW])
        b = _dot(h_scr[...], w3_ref[:, c:c + FFN_CW])
        t = (a * _sigmoid(a) * b).astype(bf16)
        o_ref[...] += _dot(t, w2_ref[c:c + FFN_CW, :])


def _ffn_dense(x, g, w1, w3, w2):
    n = x.shape[0]
    tm = FFN_TM
    row = pl.BlockSpec((tm, D_MODEL), lambda i: (i, 0))
    vmem = 3 * 2 * D_MODEL * w1.shape[1] + 4 * tm * D_MODEL * 4 + 8 * tm * D_MODEL * 4 + 4 * MIB
    return pl.pallas_call(
        _ffn_body, grid=(n // tm,),
        in_specs=[row, _const_spec((1, D_MODEL)), _const_spec(w1.shape), _const_spec(w3.shape),
                  _const_spec(w2.shape)],
        out_specs=row, out_shape=jax.ShapeDtypeStruct((n, D_MODEL), f32),
        scratch_shapes=[pltpu.VMEM((tm, D_MODEL), bf16)],
        compiler_params=_params(vmem, 1), name="ffn_dense",
    )(x, g, w1, w3, w2)


MOE_TM = 1408
MOE_BLOCK = 256
MOE_TAIL = 128
MOE_ONE_SHOT = 3


def _split3(x):
    hi = x.astype(bf16)
    r1 = x - hi.astype(f32)
    mid = r1.astype(bf16)
    lo = (r1 - mid.astype(f32)).astype(bf16)
    return hi, mid, lo


def _router_body(x_ref, g_ref, wr_ref, hb_ref, gate_t_ref, rank_t_ref, rank_c_ref, cnt_ref):
    x = x_ref[...]
    tm = x.shape[0]
    h = x * lax.rsqrt(jnp.mean(x * x, axis=-1, keepdims=True) + RMS_EPS) * g_ref[...]
    hb_ref[...] = h.astype(bf16)
    h_hi, h_mid, h_lo = _split3(h)
    w_hi, w_mid, w_lo = _split3(wr_ref[...])
    logits = (_dot_nt(w_hi, h_hi) + _dot_nt(w_hi, h_mid) + _dot_nt(w_mid, h_hi)
              + _dot_nt(w_hi, h_lo) + _dot_nt(w_mid, h_mid) + _dot_nt(w_lo, h_hi))
    eid = lax.broadcasted_iota(jnp.int32, logits.shape, 0)
    m1 = jnp.max(logits, axis=0, keepdims=True)
    i1 = jnp.min(jnp.where(logits == m1, eid, N_EXPERTS), axis=0, keepdims=True)
    sel1 = eid == i1
    rest = jnp.where(sel1, -jnp.inf, logits)
    m2 = jnp.max(rest, axis=0, keepdims=True)
    i2 = jnp.min(jnp.where(rest == m2, eid, N_EXPERTS), axis=0, keepdims=True)
    sel2 = eid == i2
    e2 = jnp.exp(m2 - m1)
    den = 1.0 + e2
    gate_t = jnp.where(sel1, 1.0 / den, 0.0) + jnp.where(sel2, e2 / den, 0.0)
    sel = jnp.where(sel1 | sel2, 1.0, 0.0)
    s_idx = lax.broadcasted_iota(jnp.int32, (tm, tm), 0)
    t_idx = lax.broadcasted_iota(jnp.int32, (tm, tm), 1)
    before = jnp.where(s_idx < t_idx, 1.0, 0.0).astype(bf16)
    rank = _dot(sel.astype(bf16), before)
    rank_t = jnp.where(sel > 0.5, rank, -1.0)
    gate_t_ref[...] = gate_t
    rank_t_ref[...] = rank_t
    rank_c_ref[...] = rank_t.T
    cnt = jnp.sum(sel, axis=1, keepdims=True)
    cnt_ref[...] = jnp.broadcast_to(cnt, cnt_ref.shape).astype(jnp.int32)


def _router(x, g, wr_t):
    n = x.shape[0]
    tm = MOE_TM
    nt = n // tm
    vmem = 2 * tm * D_MODEL * 6 + 8 * tm * D_MODEL * 4 + 3 * tm * tm * 4 + 8 * MIB
    return pl.pallas_call(
        _router_body, grid=(nt,),
        in_specs=[pl.BlockSpec((tm, D_MODEL), lambda i: (i, 0)), _const_spec((1, D_MODEL)),
                  _const_spec(wr_t.shape)],
        out_specs=[pl.BlockSpec((tm, D_MODEL), lambda i: (i, 0)),
                   pl.BlockSpec((N_EXPERTS, tm), lambda i: (0, i)),
                   pl.BlockSpec((N_EXPERTS, tm), lambda i: (0, i)),
                   pl.BlockSpec((tm, N_EXPERTS), lambda i: (i, 0)),
                   pl.BlockSpec((None, N_EXPERTS, 128), lambda i: (i, 0, 0))],
        out_shape=[jax.ShapeDtypeStruct((n, D_MODEL), bf16),
                   jax.ShapeDtypeStruct((N_EXPERTS, n), f32), jax.ShapeDtypeStruct((N_EXPERTS, n), f32),
                   jax.ShapeDtypeStruct((n, N_EXPERTS), f32),
                   jax.ShapeDtypeStruct((nt, N_EXPERTS, 128), jnp.int32)],
        compiler_params=_params(vmem, 1), name="moe_router",
    )(x, g, wr_t)


MOE_FS = 1792


def _moe_blocks(cnt, fn):
    n_full = cnt // MOE_BLOCK
    rem = cnt - n_full * MOE_BLOCK
    n_big = n_full + jnp.where(rem > MOE_TAIL, 1, 0)
    n_tail = (cnt + MOE_TAIL - 1) // MOE_TAIL
    for k in range(1, MOE_ONE_SHOT + 1):
        @pl.when(n_tail == k)
        def _(k=k):
            fn(0, k * MOE_TAIL)

    @pl.when(n_tail > MOE_ONE_SHOT)
    def _():
        def body(j, carry):
            fn(pl.multiple_of(j * MOE_BLOCK, MOE_BLOCK), MOE_BLOCK)
            return carry

        lax.fori_loop(0, n_big, body, 0)

        @pl.when((rem > 0) & (rem <= MOE_TAIL))
        def _():
            fn(pl.multiple_of(n_full * MOE_BLOCK, MOE_TAIL), MOE_TAIL)


def _slots(r0, rows, axis):
    shape = (rows, 1) if axis == 0 else (1, rows)
    return (lax.broadcasted_iota(jnp.int32, shape, axis) + r0).astype(f32)


def _experts_body(cnt_ref, hb_ref, rank_t_ref, gate_t_ref, w1_ref, w3_ref, w2_ref, ys_ref):
    e, i = pl.program_id(0), pl.program_id(1)
    rank_row = rank_t_ref[pl.ds(e, 1), :]
    gate_row = gate_t_ref[pl.ds(e, 1), :]
    d_ff = w2_ref.shape[0]
    ys_ref[...] = jnp.zeros(ys_ref.shape, bf16)

    def run(r0, rows):
        match = rank_row == _slots(r0, rows, 0)
        xb = _dot(jnp.where(match, 1.0, 0.0).astype(bf16), hb_ref[...]).astype(bf16)
        gate_blk = jnp.sum(jnp.where(match, gate_row, 0.0), axis=1, keepdims=True)
        acc = jnp.zeros((rows, D_MODEL), f32)
        for s in range(0, d_ff, MOE_FS):
            a = _dot(xb, w1_ref[:, s:s + MOE_FS])
            b = _dot(xb, w3_ref[:, s:s + MOE_FS])
            acc = acc + _dot((a * _sigmoid(a) * b).astype(bf16), w2_ref[s:s + MOE_FS, :])
        ys_ref[pl.ds(r0, rows), :] = (acc * gate_blk).astype(bf16)

    _moe_blocks(cnt_ref[i * N_EXPERTS + e], run)


def _experts(cnt, hb, rank_t, gate_t, w1, w3, w2):
    n = hb.shape[0]
    tm = MOE_TM
    n_e, _, d_ff = w1.shape
    once = dict(pipeline_mode=pl.Buffered(1))
    meta = pl.BlockSpec((N_EXPERTS, tm), lambda e, i, cnt: (0, i))
    grid_spec = pltpu.PrefetchScalarGridSpec(
        num_scalar_prefetch=1, grid=(n_e, n // tm),
        in_specs=[pl.BlockSpec((tm, D_MODEL), lambda e, i, cnt: (i, 0)), meta, meta,
                  pl.BlockSpec((None, D_MODEL, d_ff), lambda e, i, cnt: (e, 0, 0), **once),
                  pl.BlockSpec((None, D_MODEL, d_ff), lambda e, i, cnt: (e, 0, 0), **once),
                  pl.BlockSpec((None, d_ff, D_MODEL), lambda e, i, cnt: (e, 0, 0), **once)],
        out_specs=pl.BlockSpec((None, None, tm, D_MODEL), lambda e, i, cnt: (e, i, 0, 0)))
    max_rows = MOE_ONE_SHOT * MOE_TAIL
    vmem = (3 * D_MODEL * d_ff * 2 + 2 * 2 * tm * D_MODEL * 2 + 4 * N_EXPERTS * tm * 4
            + max_rows * (tm * 6 + 3 * MOE_FS * 4 + 3 * D_MODEL * 4) + 6 * MIB)
    return pl.pallas_call(
        _experts_body, grid_spec=grid_spec,
        out_shape=jax.ShapeDtypeStruct((n_e, n // tm, tm, D_MODEL), bf16),
        compiler_params=_params(vmem, 2), name="moe_experts",
    )(cnt, hb, rank_t, gate_t, w1, w3, w2)


def _combine_body(cnt_ref, ys_ref, rank_c_ref, o_ref):
    i, e = pl.program_id(0), pl.program_id(1)
    tm = o_ref.shape[0]

    @pl.when(e == 0)
    def _():
        o_ref[...] = jnp.zeros(o_ref.shape, f32)

    lane8 = lax.broadcasted_iota(jnp.int32, (tm, N_EXPERTS), 1)
    rank_col = jnp.sum(jnp.where(lane8 == e, rank_c_ref[...], 0.0), axis=1, keepdims=True)

    def scatter(r0, rows):
        onehot = jnp.where(rank_col == _slots(r0, rows, 1), 1.0, 0.0).astype(bf16)
        o_ref[...] += _dot(onehot, ys_ref[pl.ds(r0, rows), :])

    _moe_blocks(cnt_ref[i * N_EXPERTS + e], scatter)


def _combine(cnt, ys, rank_c):
    n_e, nt, tm, _ = ys.shape
    grid_spec = pltpu.PrefetchScalarGridSpec(
        num_scalar_prefetch=1, grid=(nt, n_e),
        in_specs=[pl.BlockSpec((None, None, tm, D_MODEL), lambda i, e, cnt: (e, i, 0, 0)),
                  pl.BlockSpec((tm, N_EXPERTS), lambda i, e, cnt: (i, 0))],
        out_specs=pl.BlockSpec((tm, D_MODEL), lambda i, e, cnt: (i, 0)))
    vmem = 2 * tm * D_MODEL * (2 + 4) + 2 * tm * 128 * 4 + 3 * tm * D_MODEL * 4 + 4 * MIB
    return pl.pallas_call(
        _combine_body, grid_spec=grid_spec, out_shape=jax.ShapeDtypeStruct((nt * tm, D_MODEL), f32),
        compiler_params=_params(vmem, 2), name="moe_combine",
    )(cnt, ys, rank_c)


NORM_TM = 512


def _norm_body(n_long_tiles, x_ref, y_ref, g_ref, o_long_ref, o_short_ref):
    x = x_ref[...] + y_ref[...]
    val = x * lax.rsqrt(jnp.mean(x * x, axis=-1, keepdims=True) + RMS_EPS) * g_ref[...]
    i = pl.program_id(0)

    @pl.when(i < n_long_tiles)
    def _():
        o_long_ref[...] = val

    @pl.when(i >= n_long_tiles)
    def _():
        o_short_ref[...] = val


def _add_norm(x, y, g, n_long):
    n = x.shape[0]
    tm = NORM_TM
    nl = n_long // tm
    row = pl.BlockSpec((tm, D_MODEL), lambda i: (i, 0))
    return pl.pallas_call(
        functools.partial(_norm_body, nl), grid=(n // tm,),
        in_specs=[row, row, _const_spec((1, D_MODEL))],
        out_specs=[pl.BlockSpec((tm, D_MODEL), lambda i: (jnp.minimum(i, nl - 1), 0)),
                   pl.BlockSpec((tm, D_MODEL), lambda i: (jnp.maximum(i - nl, 0), 0))],
        out_shape=[jax.ShapeDtypeStruct((n_long, D_MODEL), f32),
                   jax.ShapeDtypeStruct((n - n_long, D_MODEL), f32)],
        compiler_params=_params(12 * tm * D_MODEL * 4, 1), name="final_norm",
    )(x, y, g)


def kernel(x_prompt, x_sample, state_wkv, state_shift, norm_mix_g, w_in, shift_mu, sgu_ln_g, sgu_ln_b, sgu_w, sgu_b, rwkv_w0, rwkv_w2, rwkv_a0, rwkv_a2, rwkv_g2, rwkv_v0, rwkv_v1, rwkv_v2, rwkv_k_k, rwkv_k_a, rwkv_r_k, rwkv_ln_g, rwkv_ln_b, w_branch_a, w_branch_b, w_out, norm_ffn_g, ffn_w1, ffn_w3, ffn_w2, moe_router, moe_w1, moe_w3, moe_w2, norm_final_g):
    bp, tp, d = x_prompt.shape
    bs, ts, _ = x_sample.shape
    depth = w_in.shape[0]
    n_p, n_s = bp * tp, bs * ts
    x = jnp.concatenate([x_prompt.reshape(n_p, d), x_sample.reshape(n_s, d)], axis=0)
    row = lambda a: a.reshape(1, -1)

    hid = jnp.arange(4 * HEAD_B) // HEAD_B
    bd = (hid[:, None] == hid[None, :]).astype(bf16)
    tpos = jnp.arange(CHUNK)
    mask_long = (tpos[None, :] <= tpos[:, None]).astype(f32)
    mask_short = ((tpos[None, :] // ts == tpos[:, None] // ts) & (tpos[None, :] <= tpos[:, None])).astype(f32)
    sgu_mask = jnp.stack([mask_long, mask_short])
    zero_shift = jnp.zeros((bp, N_SHIFT), f32)

    shift_p, shift_s, chunk_v = [], [], []
    wkv_p = wkv_s = None
    for l in range(depth):
        vparams = None
        if l > 0:
            vparams = (rwkv_v1[l - 1].astype(bf16), rwkv_v2[l - 1].astype(bf16), row(rwkv_v0[l - 1]))
        outs = _in_proj(x, n_p, row(norm_mix_g[l]), w_in[l].astype(bf16), row(sgu_ln_g[l]), row(sgu_ln_b[l]),
                        vparams)
        u, va, va_s, ps, tails, gates = outs[:6]
        tiles_per_seq = tp // IN_TM
        shift_p.append(tails[tiles_per_seq - 1:bp * tiles_per_seq:tiles_per_seq, 7])
        shift_s.append(ps[n_p + ts - 1::ts])
        chunk_v.append(va_s.reshape(bs, ts, D_A))

        zpad = jnp.zeros((LORA_W, D_B), f32)
        wa2 = jnp.concatenate([jnp.concatenate([rwkv_w2[l], zpad], axis=1),
                               jnp.concatenate([zpad, rwkv_a2[l]], axis=1)], axis=0).astype(bf16)
        wts = (row(shift_mu[l]), wa2, row(rwkv_w0[l]), row(rwkv_a0[l]), rwkv_g2[l].astype(bf16),
               row(rwkv_k_k[l]), row(rwkv_k_a[l]), row(rwkv_r_k[l]), bd, row(rwkv_ln_g[l]), row(rwkv_ln_b[l]))
        first_s = jnp.pad(state_shift[l][:, None, :], ((0, 0), (0, ts - 1), (0, 0))).reshape(n_s, N_SHIFT)
        n_par = TIME_MIX_PARTS if bp % TIME_MIX_PARTS == 0 else 1
        if l == 0:
            n_steps = n_p // SCAN_ROWS // n_par
            moe_2d = [w.reshape(-1, w.shape[-1]) for w in (moe_w1, moe_w3, moe_w2)]
            ride = all(w.shape[0] % (16 * n_steps) == 0 for w in moe_2d)
            yb_p, vf_p, wkv_p, *moe_bf = _time_mix_q(ps, 0, zero_shift, None, wts, None, None, True, l, depth,
                                                     bp, tp, n_par, casts=moe_2d if ride else ())
            if not ride:
                moe_bf = [w.astype(bf16) for w in moe_2d]
            moe_bf = [w.reshape(s.shape) for w, s in zip(moe_bf, (moe_w1, moe_w3, moe_w2))]
            yb_s, vf_s, wkv_s = _time_mix_q(ps, n_p, first_s, None, wts, state_wkv, None, True, l, depth, bs, ts, 1)
        else:
            yb_p, wkv_p = _time_mix_q(ps, 0, zero_shift, (outs[6], vf_p), wts, None, wkv_p, False, l, depth,
                                      bp, tp, n_par)
            yb_s, wkv_s = _time_mix_q(ps, n_p, first_s, (outs[6], vf_s), wts, state_wkv, wkv_s, False, l, depth,
                                      bs, ts, 1)

        w_short = jnp.tile(sgu_w[l][:, :ts, :ts], (1, CHUNK // ts, CHUNK // ts))
        sgu_w2 = jnp.stack([sgu_w[l], w_short])
        b_long = jnp.repeat(sgu_b[l].T, GROUP_A, axis=1)
        b_short = jnp.tile(b_long[:ts], (CHUNK // ts, 1))
        sgu_b2 = jnp.stack([b_long, b_short])
        x = _mix_out(u, va, yb_p, yb_s, gates, x, sgu_w2, sgu_mask, sgu_b2,
                     w_branch_a[l].astype(bf16), w_branch_b[l].astype(bf16), w_out[l].astype(bf16))

        j = l // 2
        if l % 2 == 0:
            x = _ffn_dense(x, row(norm_ffn_g[l]), ffn_w1[j].astype(bf16), ffn_w3[j].astype(bf16),
                           ffn_w2[j].astype(bf16))
            delta = None
        else:
            hb, gate_t, rank_t, rank_c, cnt = _router(x, row(norm_ffn_g[l]), moe_router[j].T)
            cnt = cnt[:, :, 0].reshape(-1)
            ys = _experts(cnt, hb, rank_t, gate_t, moe_bf[0][j], moe_bf[1][j], moe_bf[2][j])
            delta = _combine(cnt, ys, rank_c)
            if l < depth - 1:
                x = x + delta
                delta = None

    yn_p, yn_s = _add_norm(x, jnp.zeros_like(x) if delta is None else delta, row(norm_final_g), n_p)
    return (yn_p.reshape(bp, tp, d), yn_s.reshape(bs, ts, d),
            wkv_p, jnp.stack(shift_p), wkv_s, jnp.stack(shift_s),
            jnp.stack(chunk_v))
```

```python
import functools

import jax
import jax.numpy as jnp
from jax import lax
from jax.experimental import pallas as pl
from jax.experimental.pallas import tpu as pltpu

f32 = jnp.float32
bf16 = jnp.bfloat16

D_MODEL = 1024
CHUNK = 128
D_A = D_MODEL
GROUP_A = 128
H_A = D_A // GROUP_A
D_B = D_MODEL
HEAD_B = 64
H_B = D_B // HEAD_B
LORA_W = 64
LORA_A = 64
LORA_G = 128
N_SHIFT = 3 * D_B + LORA_W + LORA_A + LORA_G
N_IN = 2 * D_A + N_SHIFT + 2 * D_MODEL
N_EXPERTS = 8
RMS_EPS = 1e-6
LN_EPS = 1e-5
GN_EPS = 64e-5

PAIR = 2 * HEAD_B
N_PAIR = H_B // 2
MIB = 1024 * 1024
VMEM_CAP_V7X = 56 * MIB


def _params(vmem_bytes, n_grid):
    return pltpu.CompilerParams(
        dimension_semantics=("arbitrary",) * n_grid,
        vmem_limit_bytes=int(min(max(vmem_bytes, 16 * MIB), VMEM_CAP_V7X)),
    )


def _const_spec(shape):
    nd = len(shape)
    return pl.BlockSpec(shape, lambda *_: (0,) * nd, pipeline_mode=pl.Buffered(1))


def _gelu(x):
    return 0.5 * x * (1.0 + lax.erf(x * (2.0 ** -0.5)))


def _sigmoid(x):
    return 1.0 / (1.0 + jnp.exp(-x))


def _dot(a, b):
    return jnp.dot(a, b, preferred_element_type=f32)


def _dot_nt(a, b):
    return lax.dot_general(a, b, (((1,), (1,)), ((), ())), preferred_element_type=f32)


def _dot_tn(a, b):
    return lax.dot_general(a, b, (((0,), (0,)), ((), ())), preferred_element_type=f32)


def _head_sum(x, bd_ref):
    w = bd_ref.shape[0]
    cols = [_dot(x[:, c:c + w].astype(bf16), bd_ref[...]) for c in range(0, x.shape[1], w)]
    return jnp.concatenate(cols, axis=1)


def _rows_or_pair(x, tm, n_long_tiles):
    if not isinstance(x, tuple):
        return [pl.BlockSpec((tm, D_MODEL), lambda i: (i, 0))], [x], x.shape[0]
    specs = [pl.BlockSpec((tm, D_MODEL), lambda i: (jnp.minimum(i, n_long_tiles - 1), 0)),
             pl.BlockSpec((tm, D_MODEL), lambda i: (jnp.maximum(i - n_long_tiles, 0), 0))]
    return specs, list(x), x[0].shape[0] + x[1].shape[0]


IN_TM = 512
IN_CW = 512


def _in_proj_body(n_long_tiles, split_x, has_v, x_ref, *rest):
    if split_x:
        xs_ref, rest = rest[0], rest[1:]
    g_ref, w_ref, lng_ref, lnb_ref, *rest = rest
    if has_v:
        v1_ref, v2_ref, v0_ref, u_ref, va_ref, vs_ref, ps_ref, tail_ref, gate_ref, vg_ref, h_scr, t_scr = rest
    else:
        u_ref, va_ref, vs_ref, ps_ref, tail_ref, gate_ref, h_scr, t_scr = rest
    x = x_ref[...]
    if split_x:
        x = jnp.where(pl.program_id(0) < n_long_tiles, x, xs_ref[...])
    tm = x.shape[0]
    h = x * lax.rsqrt(jnp.mean(x * x, axis=-1, keepdims=True) + RMS_EPS) * g_ref[...]
    h_scr[...] = h.astype(bf16)

    def mm(c0, c1):
        return _dot(h_scr[...], w_ref[:, c0:c1])

    for c in range(0, D_A, IN_CW):
        u_ref[:, c:c + IN_CW] = _gelu(mm(c, c + IN_CW)).astype(bf16)
    for c in range(0, D_A, IN_CW):
        t_scr[:, c:c + IN_CW] = _gelu(mm(D_A + c, D_A + c + IN_CW))
    t = t_scr[...]
    mu = jnp.mean(t, axis=-1, keepdims=True)
    d = t - mu
    var = jnp.mean(d * d, axis=-1, keepdims=True)
    va = d * lax.rsqrt(var + LN_EPS) * lng_ref[...] + lnb_ref[...]
    va_ref[...] = va.astype(bf16)

    @pl.when(pl.program_id(0) >= n_long_tiles)
    def _():
        vs_ref[...] = va

    for c in range(0, N_SHIFT, IN_CW):
        c1 = min(c + IN_CW, N_SHIFT)
        ps_ref[:, c:c1] = mm(2 * D_A + c, 2 * D_A + c1)
    tail_ref[...] = ps_ref[tm - 8:tm, :]
    base = 2 * D_A + N_SHIFT
    for c in range(0, 2 * D_MODEL, IN_CW):
        gate_ref[:, c:c + IN_CW] = _sigmoid(mm(base + c, base + c + IN_CW)).astype(bf16)
    if has_v:
        lv = _dot(h_scr[...], v1_ref[...])
        vg_ref[...] = _sigmoid(v0_ref[...] + _dot(lv.astype(bf16), v2_ref[...])).astype(bf16)


def _in_proj(x, n_long, g, w_in, lng, lnb, vparams):
    tm = IN_TM
    nl = n_long // tm
    has_v = vparams is not None
    row = lambda w: pl.BlockSpec((tm, w), lambda i: (i, 0))
    x_specs, x_args, n = _rows_or_pair(x, tm, nl)
    in_specs = x_specs + [_const_spec((1, D_MODEL)), _const_spec((D_MODEL, N_IN)),
                          _const_spec((1, D_A)), _const_spec((1, D_A))]
    args = x_args + [g, w_in, lng, lnb]
    out_shape = [jax.ShapeDtypeStruct((n, D_A), bf16), jax.ShapeDtypeStruct((n, D_A), bf16),
                 jax.ShapeDtypeStruct((n - n_long, D_A), f32),
                 jax.ShapeDtypeStruct((n, N_SHIFT), f32), jax.ShapeDtypeStruct((n // tm, 8, N_SHIFT), f32),
                 jax.ShapeDtypeStruct((n, 2 * D_MODEL), bf16)]
    out_specs = [row(D_A), row(D_A), pl.BlockSpec((tm, D_A), lambda i: (jnp.maximum(i - nl, 0), 0)),
                 row(N_SHIFT), pl.BlockSpec((None, 8, N_SHIFT), lambda i: (i, 0, 0)), row(2 * D_MODEL)]
    if has_v:
        v1, v2, v0 = vparams
        in_specs += [_const_spec(v1.shape), _const_spec(v2.shape), _const_spec((1, D_B))]
        args += [v1, v2, v0]
        out_shape.append(jax.ShapeDtypeStruct((n, D_B), bf16))
        out_specs.append(row(D_B))
    out_bytes = 2 * (2 * D_A + 2 * D_MODEL + (D_B if has_v else 0)) + 4 * (N_SHIFT + D_A)
    vmem = 2 * D_MODEL * N_IN + 2 * tm * (4 * D_MODEL + out_bytes) + 8 * tm * D_MODEL * 4 + 4 * MIB
    return pl.pallas_call(
        functools.partial(_in_proj_body, nl, len(x_args) == 2, has_v),
        grid=(n // tm,), in_specs=in_specs, out_specs=out_specs, out_shape=out_shape,
        scratch_shapes=[pltpu.VMEM((tm, D_MODEL), bf16), pltpu.VMEM((tm, D_A), f32)],
        compiler_params=_params(vmem, 1), name="in_proj",
    )(*args)


N_MIX_W = 11
SCAN_ROWS = 64
TIME_MIX_PARTS = 2
N_OPS_BF, N_OPS_F32 = 7, 3


def _tmq_outer(*args):
    for parity in (0, 1):
        @pl.when(pl.program_id(0) % 2 == parity)
        def _(parity=parity):
            _tmq_body(parity, *args)


def _tmq_body(parity, n_par, n_seq, slen, long_seq, n_chunk, n_blocks, has_v, has_s0, emit_v, n_alias, n_cast,
              *refs):
    take = lambda k: (refs[:k], refs[k:])
    ps_refs, refs = take(n_par)
    prevblk_refs, refs = take(n_par)
    (first_ref,), refs = take(1)
    vg_refs = vf_refs = None
    if has_v:
        vg_refs, refs = take(n_par)
        vf_refs, refs = take(n_par)
    (mu_ref, wa2_ref, w0_ref, a0_ref, g2_ref, kk_ref, ka_ref, rk_ref, bd_ref, lng_ref, lnb_ref), refs = take(N_MIX_W)
    (tri_ref, tot_ref), refs = take(2)
    if has_s0:
        (s0_ref,), refs = take(1)
    cast_src, refs = take(n_cast)
    refs = refs[n_alias:]
    (yb_ref,), refs = take(1)
    if emit_v:
        (vo_ref,), refs = take(1)
    (sl_ref,), refs = take(1)
    cast_dst, (s_scr, opb_scr, opf_scr) = take(n_cast)
    for src, dst in zip(cast_src, cast_dst):
        dst[...] = src[...].astype(bf16)
    j = pl.program_id(0)
    blk_prep = jnp.minimum(j, n_blocks - 1)
    c = jnp.maximum(j - 1, 0) % n_chunk
    par = range(n_par)
    pairs = range(N_PAIR)

    if parity == 0:
        @pl.when(j == 0)
        def _():
            opb_scr[1] = jnp.zeros(opb_scr.shape[1:], bf16)
            opf_scr[1] = jnp.zeros(opf_scr.shape[1:], f32)

    @pl.when(c == 0)
    def _():
        if has_s0:
            zero = jnp.zeros((HEAD_B, HEAD_B), f32)
            for q in par:
                def pack(g, carry, q=q):
                    for p in pairs:
                        top = jnp.concatenate([s0_ref[q, g, 2 * p], zero], axis=1)
                        bot = jnp.concatenate([zero, s0_ref[q, g, 2 * p + 1]], axis=1)
                        s_scr[q, p, pl.ds(pl.multiple_of(g * PAIR, PAIR), PAIR), :] = (
                            jnp.concatenate([top, bot], axis=0))
                    return carry

                lax.fori_loop(0, n_seq, pack, 0)
        else:
            s_scr[...] = jnp.zeros(s_scr.shape, f32)

    row = lax.broadcasted_iota(jnp.int32, (SCAN_ROWS, 1), 0)
    nxt = [dict() for _ in par]
    for q in par:
        ps = ps_refs[q][...]
        if long_seq:
            seq = q * (n_blocks // n_chunk) + blk_prep // n_chunk
            init = jnp.where(blk_prep % n_chunk == 0, first_ref[pl.ds(seq, 1), :], prevblk_refs[q][7:8, :])
            starts = row == 0
        else:
            init = first_ref[...]
            starts = row % slen == 0
        prev = jnp.where(starts, init, pltpu.roll(ps, 1, 0))
        xs = ps + mu_ref[...] * (prev - ps)
        dwa = xs[:, 3 * D_B:3 * D_B + LORA_W + LORA_A]
        dg = xs[:, 3 * D_B + LORA_W + LORA_A:N_SHIFT]
        lane = lax.broadcasted_iota(jnp.int32, dwa.shape, 1)
        lhs = jnp.where(lane < LORA_W, jnp.tanh(dwa), dwa).astype(bf16)
        kkr = xs[:, D_B:2 * D_B] * kk_ref[...]
        nxt[q].update(r=xs[:, 0:D_B], k_raw=xs[:, D_B:2 * D_B], v=xs[:, 2 * D_B:3 * D_B],
                      wa=_dot(lhs, wa2_ref[...]),
                      g=_dot(_sigmoid(dg).astype(bf16), g2_ref[...]),
                      kkr=kkr, kk_norm2=_head_sum(kkr * kkr, bd_ref))

    ops = [[opb_scr[1 - parity, q, k] for k in range(N_OPS_BF)] + [opf_scr[1 - parity, q, k] for k in range(N_OPS_F32)]
           for q in par]

    n_st = 2 * SCAN_ROWS

    def causal(width, inclusive):
        ri = lax.broadcasted_iota(jnp.int32, (n_st, width), 0)
        ci = lax.broadcasted_iota(jnp.int32, (n_st, width), 1) % n_st
        same = (ri // SCAN_ROWS == ci // SCAN_ROWS) & ((ri % SCAN_ROWS) // slen == (ci % SCAN_ROWS) // slen)
        t_r, t_c = ri % SCAN_ROWS, ci % SCAN_ROWS
        return same & ((t_c <= t_r) if inclusive else (t_c < t_r))

    incl2 = causal(2 * n_st, True)
    strict = causal(n_st, False)
    lane_e = lax.broadcasted_iota(jnp.int32, (1, PAIR), 1) < HEAD_B
    row_seq = ((lax.broadcasted_iota(jnp.int32, (2 * n_st, 1), 0) % SCAN_ROWS) // slen)

    def stack(z):
        return jnp.concatenate([jnp.where(lane_e, z, 0.0), jnp.where(lane_e, 0.0, z)], axis=0).astype(bf16)

    def pick_seq(big):
        if n_seq == 1:
            return big
        acc = jnp.where(row_seq == 0, big[:, :PAIR], 0.0)
        for g in range(1, n_seq):
            acc = acc + jnp.where(row_seq == g, big[:, g * PAIR:(g + 1) * PAIR], 0.0)
        return acc

    def spread_seq(z):
        if n_seq == 1:
            return z
        return jnp.concatenate([jnp.where(row_seq == g, z, jnp.zeros_like(z)) for g in range(n_seq)], axis=1)

    sls = [slice(p * PAIR, (p + 1) * PAIR) for p in pairs]
    chains = [(q, p) for q in par for p in pairs]
    ar, bk, vb = {}, {}, {}
    for q, p in chains:
        at, rt, bt, kt, _, _, vv = ops[q][:N_OPS_BF]
        s = sls[p]
        ar[q, p] = jnp.concatenate([stack(at[:, s]), stack(rt[:, s])], axis=0)
        bk[q, p] = jnp.concatenate([bt[:, s], bt[:, s], kt[:, s], kt[:, s]], axis=0).astype(bf16)
        vb[q, p] = stack(vv[:, s])
    pm = {ch: _dot_nt(ar[ch], bk[ch]) for ch in chains}
    qs = {(q, p): pick_seq(_dot_nt(ar[q, p], s_scr[q, p].astype(bf16))) for q, p in chains}
    x = {ch: qs[ch][:n_st] + _dot(jnp.where(strict, pm[ch][:n_st, n_st:], 0.0).astype(bf16), vb[ch])
         for ch in chains}
    m = {ch: jnp.where(strict, pm[ch][:n_st, :n_st], 0.0).astype(bf16) for ch in chains}

    for q in par:
        d = nxt[q]
        w_log = -jax.nn.softplus(-(w0_ref[...] + d["wa"][:, :D_B])) - 0.5
        lw = -jnp.exp(w_log)
        a_new = _sigmoid(a0_ref[...] + d["wa"][:, D_B:])
        k_new = d["k_raw"] * (1.0 + (a_new - 1.0) * ka_ref[...])
        v_mix = d["v"]
        if has_v:
            v_mix = v_mix + (vf_refs[q][...].astype(f32) - v_mix) * vg_refs[q][...].astype(f32)
        if emit_v:
            vo_ref[q] = v_mix.astype(bf16)
        d.update(lw=lw, a=a_new, k=k_new, v=v_mix,
                 cum=jnp.dot(tri_ref[...], lw, precision=lax.Precision.HIGHEST, preferred_element_type=f32),
                 tot=jnp.dot(tot_ref[...], lw, precision=lax.Precision.HIGHEST, preferred_element_type=f32),
                 rk_sum=_head_sum(d["r"] * k_new * rk_ref[...], bd_ref))

    span = 1
    while span < slen:
        span *= 2
        if span < slen:
            z = {ch: _dot(m[ch], jnp.concatenate([x[ch].astype(bf16), m[ch]], axis=1)) for ch in chains}
            x = {ch: x[ch] + z[ch][:, :n_st] for ch in chains}
            m = {ch: z[ch][:, n_st:].astype(bf16) for ch in chains}
        else:
            x = {ch: x[ch] + _dot(m[ch], x[ch].astype(bf16)) for ch in chains}
    uv = {ch: jnp.concatenate([x[ch].astype(bf16), vb[ch]], axis=0) for ch in chains}
    ys = {}
    for ch in chains:
        yp = qs[ch][n_st:] + _dot(jnp.where(incl2, pm[ch][n_st:], 0.0).astype(bf16), uv[ch])
        ys[ch] = yp[:SCAN_ROWS] + yp[SCAN_ROWS:]
    for q in par:
        g_last, gg, bonus = ops[q][N_OPS_BF:]
        bh, kh = ops[q][4], ops[q][5]
        y = jnp.concatenate([ys[q, p] for p in pairs], axis=1)
        mean = _head_sum(y, bd_ref) * (1.0 / HEAD_B)
        dev = y - mean
        var = _head_sum(dev * dev, bd_ref) * (1.0 / HEAD_B)
        yn = dev * lax.rsqrt(var + GN_EPS) * lng_ref[...] + lnb_ref[...] + bonus
        yb_ref[q] = (yn * gg).astype(bf16)
        for p in pairs:
            bkh = jnp.concatenate([stack(bh[:, sls[p]]), stack(kh[:, sls[p]])], axis=0)
            upd = _dot_tn(spread_seq(uv[q, p]), bkh)
            for g in range(n_seq):
                rows = slice(g * PAIR, (g + 1) * PAIR)
                s_scr[q, p, rows, :] = (s_scr[q, p, rows, :] * g_last[g * slen:g * slen + 1, sls[p]] + upd[rows])

    for q in par:
        d = nxt[q]
        kk_new = d["kkr"] * lax.rsqrt(d["kk_norm2"] + 1e-12)
        b_new = kk_new * d["a"]
        cum, tot, lw = d["cum"], d["tot"], d["lw"]
        g_inv = jnp.exp(-cum)
        g_tail = jnp.exp(tot - cum)
        folded = (-kk_new * jnp.exp(cum - lw), d["r"] * jnp.exp(cum), b_new * g_inv, d["k"] * g_inv,
                  b_new * g_tail, d["k"] * g_tail, d["v"])
        for idx, val in enumerate(folded):
            opb_scr[parity, q, idx] = val.astype(bf16)
        for idx, val in enumerate((jnp.exp(tot), d["g"], d["rk_sum"] * d["v"])):
            opf_scr[parity, q, idx] = val

    @pl.when(c == n_chunk - 1)
    def _():
        for q in par:
            def unpack(g, carry, q=q):
                for p in pairs:
                    tile = s_scr[q, p, pl.ds(pl.multiple_of(g * PAIR, PAIR), PAIR), :]
                    sl_ref[q, g, 2 * p] = tile[:HEAD_B, :HEAD_B]
                    sl_ref[q, g, 2 * p + 1] = tile[HEAD_B:, HEAD_B:]
                return carry

            lax.fori_loop(0, n_seq, unpack, 0)


def _time_mix_q(ps, row0, first, vextra, wts, s0, s_prev, emit_v, layer, depth, n_seq_total, slen, n_par, casts=()):
    long_seq = slen >= SCAN_ROWS
    n_rows = n_seq_total * slen
    if long_seq:
        n_seq, clen, n_chunk = 1, SCAN_ROWS, slen // SCAN_ROWS
    else:
        n_seq, clen, n_chunk = SCAN_ROWS // slen, slen, 1
    n_blocks = n_rows // SCAN_ROWS // n_par
    seq_per_part = n_seq_total // n_par
    blk0 = row0 // SCAN_ROWS
    prep = lambda j: jnp.minimum(j, n_blocks - 1)
    run = lambda j: jnp.maximum(j - 1, 0)
    part = lambda q, b: q * n_blocks + b
    tok_in = lambda w, q, off: pl.BlockSpec((SCAN_ROWS, w), lambda j: (off + part(q, prep(j)), 0))
    st = pl.BlockSpec((None, n_par, n_seq, H_B, HEAD_B, HEAD_B), lambda j: (layer, 0, run(j) // n_chunk, 0, 0, 0))
    sq = pl.BlockSpec((SCAN_ROWS, SCAN_ROWS), lambda j: (0, 0))
    t = jnp.arange(SCAN_ROWS)
    same_seq = t[:, None] // clen == t[None, :] // clen
    tri = (same_seq & (t[None, :] <= t[:, None])).astype(f32)
    in_specs = [tok_in(N_SHIFT, q, blk0) for q in range(n_par)]
    in_specs += [pl.BlockSpec((8, N_SHIFT), lambda j, q=q: (
        jnp.maximum((blk0 + part(q, prep(j))) * (SCAN_ROWS // 8) - 1, 0), 0)) for q in range(n_par)]
    in_specs.append(_const_spec(first.shape) if long_seq
                    else pl.BlockSpec((SCAN_ROWS, N_SHIFT), lambda j: (prep(j), 0)))
    args = [ps] * (2 * n_par) + [first]
    if vextra is not None:
        in_specs += [tok_in(D_B, q, blk0) for q in range(n_par)] + [tok_in(D_B, q, 0) for q in range(n_par)]
        args += [vextra[0]] * n_par + [vextra[1]] * n_par
    in_specs += [_const_spec(w.shape) for w in wts] + [sq, sq]
    args += list(wts) + [tri, same_seq.astype(f32)]
    grouped = (depth, n_par, seq_per_part, H_B, HEAD_B, HEAD_B)
    if s0 is not None:
        in_specs.append(st)
        args.append(s0.reshape(grouped))
    cast_specs = [pl.BlockSpec((a.shape[0] // n_blocks, a.shape[1]), lambda j: (prep(j), 0)) for a in casts]
    in_specs += cast_specs
    args += list(casts)
    aliases = {}
    if s_prev is not None:
        aliases[len(args)] = 2 if emit_v else 1
        in_specs.append(pl.BlockSpec(memory_space=pl.ANY))
        args.append(s_prev.reshape(grouped))
    out_run = pl.BlockSpec((n_par, SCAN_ROWS, D_B), lambda j: (0, run(j), 0))
    out_prep = pl.BlockSpec((n_par, SCAN_ROWS, D_B), lambda j: (0, prep(j), 0))
    rows3 = jax.ShapeDtypeStruct((n_par, n_rows // n_par, D_B), bf16)
    out_specs = [out_run] + ([out_prep] if emit_v else []) + [st] + cast_specs
    out_shape = ([rows3] * (2 if emit_v else 1) + [jax.ShapeDtypeStruct(grouped, f32)]
                 + [jax.ShapeDtypeStruct(a.shape, bf16) for a in casts])
    state_pad = 2 * n_par * n_seq * H_B * HEAD_B * PAIR * 4
    vmem = (n_par * (2 * SCAN_ROWS * (2 * N_SHIFT * 4 + 6 * D_B * 2) + N_PAIR * n_seq * PAIR * PAIR * 4
                     + 2 * (N_OPS_BF * 2 + N_OPS_F32 * 4) * SCAN_ROWS * D_B + 48 * SCAN_ROWS * D_B * 4)
            + 2 * state_pad * (2 if s0 is not None else 1) + 8 * n_seq * PAIR * PAIR * 4 * 4 + 8 * MIB)
    outs = pl.pallas_call(
        functools.partial(_tmq_outer, n_par, n_seq, clen, long_seq, n_chunk, n_blocks, vextra is not None,
                          s0 is not None, emit_v, len(aliases), len(casts)),
        grid=(n_blocks + 1,), in_specs=in_specs, out_specs=out_specs, out_shape=out_shape,
        scratch_shapes=[pltpu.VMEM((n_par, N_PAIR, n_seq * PAIR, PAIR), f32),
                        pltpu.VMEM((2, n_par, N_OPS_BF, SCAN_ROWS, D_B), bf16),
                        pltpu.VMEM((2, n_par, N_OPS_F32, SCAN_ROWS, D_B), f32)],
        input_output_aliases=aliases,
        compiler_params=_params(vmem, 1), name="time_mix",
    )(*args)
    n_tok = 2 if emit_v else 1
    flat = [o.reshape(n_rows, D_B) for o in outs[:n_tok]]
    states = outs[n_tok].reshape(depth, n_seq_total, H_B, HEAD_B, HEAD_B)
    return (*flat, states, *outs[n_tok + 1:])


MIX_TM = 512


def _mix_body(n_long_tiles, split_x, u_ref, va_ref, ybl_ref, ybs_ref, gate_ref, x_ref, *rest):
    if split_x:
        xs_ref, rest = rest[0], rest[1:]
    sw_ref, sm_ref, sb_ref, wa_ref, wb_ref, wo_ref, o_ref, ya_scr = rest
    tm = u_ref.shape[0]
    is_long = pl.program_id(0) < n_long_tiles
    yb = jnp.where(is_long, ybl_ref[...], ybs_ref[...])
    x = x_ref[...]
    if split_x:
        x = jnp.where(is_long, x, xs_ref[...])
    for r0 in range(0, tm, CHUNK):
        for grp in range(H_A):
            cs = slice(grp * GROUP_A, (grp + 1) * GROUP_A)
            w = jnp.where(sm_ref[...] > 0.5, sw_ref[grp], 0.0).astype(bf16)
            mixed = _dot(w, va_ref[r0:r0 + CHUNK, cs].astype(bf16)) + sb_ref[:, cs]
            ya_scr[r0:r0 + CHUNK, cs] = (u_ref[r0:r0 + CHUNK, cs] * mixed).astype(bf16)
    merged = (gate_ref[:, :D_MODEL] * _dot(ya_scr[...], wa_ref[...])
              + gate_ref[:, D_MODEL:] * _dot(yb, wb_ref[...]))
    o_ref[...] = x + _dot(merged.astype(bf16), wo_ref[...])


def _mix_out(u, va, yb_long, yb_short, gates, x, sgu_w2, sgu_m2, sgu_b2, wa, wb, wo):
    tm = MIX_TM
    n_long = yb_long.shape[0]
    nl = n_long // tm
    row = lambda w: pl.BlockSpec((tm, w), lambda i: (i, 0))
    x_specs, x_args, n = _rows_or_pair(x, tm, nl)
    kind = lambda i: jnp.where(i * tm >= n_long, 1, 0)
    in_specs = [row(D_A), row(D_A),
                pl.BlockSpec((tm, D_B), lambda i: (jnp.minimum(i, nl - 1), 0)),
                pl.BlockSpec((tm, D_B), lambda i: (jnp.maximum(i - nl, 0), 0)),
                row(2 * D_MODEL)] + x_specs + [
                pl.BlockSpec((None, H_A, CHUNK, CHUNK), lambda i: (kind(i), 0, 0, 0)),
                pl.BlockSpec((None, CHUNK, CHUNK), lambda i: (kind(i), 0, 0)),
                pl.BlockSpec((None, CHUNK, D_A), lambda i: (kind(i), 0, 0)),
                _const_spec(wa.shape), _const_spec(wb.shape), _const_spec(wo.shape)]
    vmem = 2 * tm * (2 * 5 + 4 * 2) * D_MODEL + 3 * 2 * D_MODEL * D_MODEL + 10 * tm * D_MODEL * 4 + 6 * MIB
    return pl.pallas_call(
        functools.partial(_mix_body, nl, len(x_args) == 2), grid=(n // tm,), in_specs=in_specs,
        out_specs=row(D_MODEL),
        out_shape=jax.ShapeDtypeStruct((n, D_MODEL), f32),
        scratch_shapes=[pltpu.VMEM((tm, D_A), bf16)],
        compiler_params=_params(vmem, 1), name="mix_out",
    )(u, va, yb_long, yb_short, gates, *x_args, sgu_w2, sgu_m2, sgu_b2, wa, wb, wo)


FFN_TM = 512
FFN_CW = 256


def _ffn_body(x_ref, g_ref, w1_ref, w3_ref, w2_ref, o_ref, h_scr):
    x = x_ref[...]
    h = x * lax.rsqrt(jnp.mean(x * x, axis=-1, keepdims=True) + RMS_EPS) * g_ref[...]
    h_scr[...] = h.astype(bf16)
    o_ref[...] = x
    d_ff = w1_ref.shape[1]
    for c in range(0, d_ff, FFN_CW):
        a = _dot(h_scr[...], w1_ref[:, c:c + FFN_CW])
        b = _dot(h_scr[...], w3_ref[:, c:c + FFN_CW])
        t = (a * _sigmoid(a) * b).astype(bf16)
        o_ref[...] += _dot(t, w2_ref[c:c + FFN_CW, :])


def _ffn_dense(x, g, w1, w3, w2):
    n = x.shape[0]
    tm = FFN_TM
    row = pl.BlockSpec((tm, D_MODEL), lambda i: (i, 0))
    vmem = 3 * 2 * D_MODEL * w1.shape[1] + 4 * tm * D_MODEL * 4 + 8 * tm * D_MODEL * 4 + 4 * MIB
    return pl.pallas_call(
        _ffn_body, grid=(n // tm,),
        in_specs=[row, _const_spec((1, D_MODEL)), _const_spec(w1.shape), _const_spec(w3.shape),
                  _const_spec(w2.shape)],
        out_specs=row, out_shape=jax.ShapeDtypeStruct((n, D_MODEL), f32),
        scratch_shapes=[pltpu.VMEM((tm, D_MODEL), bf16)],
        compiler_params=_params(vmem, 1), name="ffn_dense",
    )(x, g, w1, w3, w2)


MOE_TM = 1408
MOE_BLOCK = 256
MOE_TAIL = 128
MOE_ONE_SHOT = 3


def _split3(x):
    hi = x.astype(bf16)
    r1 = x - hi.astype(f32)
    mid = r1.astype(bf16)
    lo = (r1 - mid.astype(f32)).astype(bf16)
    return hi, mid, lo


def _router_body(x_ref, g_ref, wr_ref, hb_ref, gate_t_ref, rank_t_ref, rank_c_ref, cnt_ref):
    x = x_ref[...]
    tm = x.shape[0]
    h = x * lax.rsqrt(jnp.mean(x * x, axis=-1, keepdims=True) + RMS_EPS) * g_ref[...]
    hb_ref[...] = h.astype(bf16)
    h_hi, h_mid, h_lo = _split3(h)
    w_hi, w_mid, w_lo = _split3(wr_ref[...])
    logits = (_dot_nt(w_hi, h_hi) + _dot_nt(w_hi, h_mid) + _dot_nt(w_mid, h_hi)
              + _dot_nt(w_hi, h_lo) + _dot_nt(w_mid, h_mid) + _dot_nt(w_lo, h_hi))
    eid = lax.broadcasted_iota(jnp.int32, logits.shape, 0)
    m1 = jnp.max(logits, axis=0, keepdims=True)
    i1 = jnp.min(jnp.where(logits == m1, eid, N_EXPERTS), axis=0, keepdims=True)
    sel1 = eid == i1
    rest = jnp.where(sel1, -jnp.inf, logits)
    m2 = jnp.max(rest, axis=0, keepdims=True)
    i2 = jnp.min(jnp.where(rest == m2, eid, N_EXPERTS), axis=0, keepdims=True)
    sel2 = eid == i2
    e2 = jnp.exp(m2 - m1)
    den = 1.0 + e2
    gate_t = jnp.where(sel1, 1.0 / den, 0.0) + jnp.where(sel2, e2 / den, 0.0)
    sel = jnp.where(sel1 | sel2, 1.0, 0.0)
    s_idx = lax.broadcasted_iota(jnp.int32, (tm, tm), 0)
    t_idx = lax.broadcasted_iota(jnp.int32, (tm, tm), 1)
    before = jnp.where(s_idx < t_idx, 1.0, 0.0).astype(bf16)
    rank = _dot(sel.astype(bf16), before)
    rank_t = jnp.where(sel > 0.5, rank, -1.0)
    gate_t_ref[...] = gate_t
    rank_t_ref[...] = rank_t
    rank_c_ref[...] = rank_t.T
    cnt = jnp.sum(sel, axis=1, keepdims=True)
    cnt_ref[...] = jnp.broadcast_to(cnt, cnt_ref.shape).astype(jnp.int32)


def _router(x, g, wr_t):
    n = x.shape[0]
    tm = MOE_TM
    nt = n // tm
    vmem = 2 * tm * D_MODEL * 6 + 8 * tm * D_MODEL * 4 + 3 * tm * tm * 4 + 8 * MIB
    return pl.pallas_call(
        _router_body, grid=(nt,),
        in_specs=[pl.BlockSpec((tm, D_MODEL), lambda i: (i, 0)), _const_spec((1, D_MODEL)),
                  _const_spec(wr_t.shape)],
        out_specs=[pl.BlockSpec((tm, D_MODEL), lambda i: (i, 0)),
                   pl.BlockSpec((N_EXPERTS, tm), lambda i: (0, i)),
                   pl.BlockSpec((N_EXPERTS, tm), lambda i: (0, i)),
                   pl.BlockSpec((tm, N_EXPERTS), lambda i: (i, 0)),
                   pl.BlockSpec((None, N_EXPERTS, 128), lambda i: (i, 0, 0))],
        out_shape=[jax.ShapeDtypeStruct((n, D_MODEL), bf16),
                   jax.ShapeDtypeStruct((N_EXPERTS, n), f32), jax.ShapeDtypeStruct((N_EXPERTS, n), f32),
                   jax.ShapeDtypeStruct((n, N_EXPERTS), f32),
                   jax.ShapeDtypeStruct((nt, N_EXPERTS, 128), jnp.int32)],
        compiler_params=_params(vmem, 1), name="moe_router",
    )(x, g, wr_t)


MOE_FS = 1792


def _moe_blocks(cnt, fn):
    n_full = cnt // MOE_BLOCK
    rem = cnt - n_full * MOE_BLOCK
    n_big = n_full + jnp.where(rem > MOE_TAIL, 1, 0)
    n_tail = (cnt + MOE_TAIL - 1) // MOE_TAIL
    for k in range(1, MOE_ONE_SHOT + 1):
        @pl.when(n_tail == k)
        def _(k=k):
            fn(0, k * MOE_TAIL)

    @pl.when(n_tail > MOE_ONE_SHOT)
    def _():
        def body(j, carry):
            fn(pl.multiple_of(j * MOE_BLOCK, MOE_BLOCK), MOE_BLOCK)
            return carry

        lax.fori_loop(0, n_big, body, 0)

        @pl.when((rem > 0) & (rem <= MOE_TAIL))
        def _():
            fn(pl.multiple_of(n_full * MOE_BLOCK, MOE_TAIL), MOE_TAIL)


def _slots(r0, rows, axis):
    shape = (rows, 1) if axis == 0 else (1, rows)
    return (lax.broadcasted_iota(jnp.int32, shape, axis) + r0).astype(f32)


def _experts_body(cnt_ref, hb_ref, rank_t_ref, gate_t_ref, w1_ref, w3_ref, w2_ref, ys_ref):
    e, i = pl.program_id(0), pl.program_id(1)
    rank_row = rank_t_ref[pl.ds(e, 1), :]
    gate_row = gate_t_ref[pl.ds(e, 1), :]
    d_ff = w2_ref.shape[0]
    ys_ref[...] = jnp.zeros(ys_ref.shape, bf16)

    def run(r0, rows):
        match = rank_row == _slots(r0, rows, 0)
        xb = _dot(jnp.where(match, 1.0, 0.0).astype(bf16), hb_ref[...]).astype(bf16)
        gate_blk = jnp.sum(jnp.where(match, gate_row, 0.0), axis=1, keepdims=True)
        acc = jnp.zeros((rows, D_MODEL), f32)
        for s in range(0, d_ff, MOE_FS):
            a = _dot(xb, w1_ref[:, s:s + MOE_FS])
            b = _dot(xb, w3_ref[:, s:s + MOE_FS])
            acc = acc + _dot((a * _sigmoid(a) * b).astype(bf16), w2_ref[s:s + MOE_FS, :])
        ys_ref[pl.ds(r0, rows), :] = (acc * gate_blk).astype(bf16)

    _moe_blocks(cnt_ref[i * N_EXPERTS + e], run)


def _experts(cnt, hb, rank_t, gate_t, w1, w3, w2):
    n = hb.shape[0]
    tm = MOE_TM
    n_e, _, d_ff = w1.shape
    once = dict(pipeline_mode=pl.Buffered(1))
    meta = pl.BlockSpec((N_EXPERTS, tm), lambda e, i, cnt: (0, i))
    grid_spec = pltpu.PrefetchScalarGridSpec(
        num_scalar_prefetch=1, grid=(n_e, n // tm),
        in_specs=[pl.BlockSpec((tm, D_MODEL), lambda e, i, cnt: (i, 0)), meta, meta,
                  pl.BlockSpec((None, D_MODEL, d_ff), lambda e, i, cnt: (e, 0, 0), **once),
                  pl.BlockSpec((None, D_MODEL, d_ff), lambda e, i, cnt: (e, 0, 0), **once),
                  pl.BlockSpec((None, d_ff, D_MODEL), lambda e, i, cnt: (e, 0, 0), **once)],
        out_specs=pl.BlockSpec((None, None, tm, D_MODEL), lambda e, i, cnt: (e, i, 0, 0)))
    max_rows = MOE_ONE_SHOT * MOE_TAIL
    vmem = (3 * D_MODEL * d_ff * 2 + 2 * 2 * tm * D_MODEL * 2 + 4 * N_EXPERTS * tm * 4
            + max_rows * (tm * 6 + 3 * MOE_FS * 4 + 3 * D_MODEL * 4) + 6 * MIB)
    return pl.pallas_call(
        _experts_body, grid_spec=grid_spec,
        out_shape=jax.ShapeDtypeStruct((n_e, n // tm, tm, D_MODEL), bf16),
        compiler_params=_params(vmem, 2), name="moe_experts",
    )(cnt, hb, rank_t, gate_t, w1, w3, w2)


def _combine_body(cnt_ref, ys_ref, rank_c_ref, o_ref):
    i, e = pl.program_id(0), pl.program_id(1)
    tm = o_ref.shape[0]

    @pl.when(e == 0)
    def _():
        o_ref[...] = jnp.zeros(o_ref.shape, f32)

    lane8 = lax.broadcasted_iota(jnp.int32, (tm, N_EXPERTS), 1)
    rank_col = jnp.sum(jnp.where(lane8 == e, rank_c_ref[...], 0.0), axis=1, keepdims=True)

    def scatter(r0, rows):
        onehot = jnp.where(rank_col == _slots(r0, rows, 1), 1.0, 0.0).astype(bf16)
        o_ref[...] += _dot(onehot, ys_ref[pl.ds(r0, rows), :])

    _moe_blocks(cnt_ref[i * N_EXPERTS + e], scatter)


def _combine(cnt, ys, rank_c):
    n_e, nt, tm, _ = ys.shape
    grid_spec = pltpu.PrefetchScalarGridSpec(
        num_scalar_prefetch=1, grid=(nt, n_e),
        in_specs=[pl.BlockSpec((None, None, tm, D_MODEL), lambda i, e, cnt: (e, i, 0, 0)),
                  pl.BlockSpec((tm, N_EXPERTS), lambda i, e, cnt: (i, 0))],
        out_specs=pl.BlockSpec((tm, D_MODEL), lambda i, e, cnt: (i, 0)))
    vmem = 2 * tm * D_MODEL * (2 + 4) + 2 * tm * 128 * 4 + 3 * tm * D_MODEL * 4 + 4 * MIB
    return pl.pallas_call(
        _combine_body, grid_spec=grid_spec, out_shape=jax.ShapeDtypeStruct((nt * tm, D_MODEL), f32),
        compiler_params=_params(vmem, 2), name="moe_combine",
    )(cnt, ys, rank_c)


NORM_TM = 512


def _norm_body(n_long_tiles, x_ref, y_ref, g_ref, o_long_ref, o_short_ref):
    x = x_ref[...] + y_ref[...]
    val = x * lax.rsqrt(jnp.mean(x * x, axis=-1, keepdims=True) + RMS_EPS) * g_ref[...]
    i = pl.program_id(0)

    @pl.when(i < n_long_tiles)
    def _():
        o_long_ref[...] = val

    @pl.when(i >= n_long_tiles)
    def _():
        o_short_ref[...] = val


def _add_norm(x, y, g, n_long):
    n = x.shape[0]
    tm = NORM_TM
    nl = n_long // tm
    row = pl.BlockSpec((tm, D_MODEL), lambda i: (i, 0))
    return pl.pallas_call(
        functools.partial(_norm_body, nl), grid=(n // tm,),
        in_specs=[row, row, _const_spec((1, D_MODEL))],
        out_specs=[pl.BlockSpec((tm, D_MODEL), lambda i: (jnp.minimum(i, nl - 1), 0)),
                   pl.BlockSpec((tm, D_MODEL), lambda i: (jnp.maximum(i - nl, 0), 0))],
        out_shape=[jax.ShapeDtypeStruct((n_long, D_MODEL), f32),
                   jax.ShapeDtypeStruct((n - n_long, D_MODEL), f32)],
        compiler_params=_params(12 * tm * D_MODEL * 4, 1), name="final_norm",
    )(x, y, g)


def kernel(x_prompt, x_sample, state_wkv, state_shift, norm_mix_g, w_in, shift_mu, sgu_ln_g, sgu_ln_b, sgu_w, sgu_b, rwkv_w0, rwkv_w2, rwkv_a0, rwkv_a2, rwkv_g2, rwkv_v0, rwkv_v1, rwkv_v2, rwkv_k_k, rwkv_k_a, rwkv_r_k, rwkv_ln_g, rwkv_ln_b, w_branch_a, w_branch_b, w_out, norm_ffn_g, ffn_w1, ffn_w3, ffn_w2, moe_router, moe_w1, moe_w3, moe_w2, norm_final_g):
    bp, tp, d = x_prompt.shape
    bs, ts, _ = x_sample.shape
    depth = w_in.shape[0]
    n_p, n_s = bp * tp, bs * ts
    x = (x_prompt.reshape(n_p, d), x_sample.reshape(n_s, d))
    row = lambda a: a.reshape(1, -1)

    hid = jnp.arange(4 * HEAD_B) // HEAD_B
    bd = (hid[:, None] == hid[None, :]).astype(bf16)
    tpos = jnp.arange(CHUNK)
    mask_long = (tpos[None, :] <= tpos[:, None]).astype(f32)
    mask_short = ((tpos[None, :] // ts == tpos[:, None] // ts) & (tpos[None, :] <= tpos[:, None])).astype(f32)
    sgu_mask = jnp.stack([mask_long, mask_short])
    zero_shift = jnp.zeros((bp, N_SHIFT), f32)

    shift_p, shift_s, chunk_v = [], [], []
    wkv_p = wkv_s = None
    for l in range(depth):
        vparams = None
        if l > 0:
            vparams = (rwkv_v1[l - 1].astype(bf16), rwkv_v2[l - 1].astype(bf16), row(rwkv_v0[l - 1]))
        outs = _in_proj(x, n_p, row(norm_mix_g[l]), w_in[l].astype(bf16), row(sgu_ln_g[l]), row(sgu_ln_b[l]),
                        vparams)
        u, va, va_s, ps, tails, gates = outs[:6]
        tiles_per_seq = tp // IN_TM
        shift_p.append(tails[tiles_per_seq - 1:bp * tiles_per_seq:tiles_per_seq, 7])
        shift_s.append(ps[n_p + ts - 1::ts])
        chunk_v.append(va_s.reshape(bs, ts, D_A))

        zpad = jnp.zeros((LORA_W, D_B), f32)
        wa2 = jnp.concatenate([jnp.concatenate([rwkv_w2[l], zpad], axis=1),
                               jnp.concatenate([zpad, rwkv_a2[l]], axis=1)], axis=0).astype(bf16)
        wts = (row(shift_mu[l]), wa2, row(rwkv_w0[l]), row(rwkv_a0[l]), rwkv_g2[l].astype(bf16),
               row(rwkv_k_k[l]), row(rwkv_k_a[l]), row(rwkv_r_k[l]), bd, row(rwkv_ln_g[l]), row(rwkv_ln_b[l]))
        first_s = jnp.pad(state_shift[l][:, None, :], ((0, 0), (0, ts - 1), (0, 0))).reshape(n_s, N_SHIFT)
        n_par = TIME_MIX_PARTS if bp % TIME_MIX_PARTS == 0 else 1
        if l == 0:
            n_steps = n_p // SCAN_ROWS // n_par
            moe_2d = [w.reshape(-1, w.shape[-1]) for w in (moe_w1, moe_w3, moe_w2)]
            ride = all(w.shape[0] % (16 * n_steps) == 0 for w in moe_2d)
            yb_p, vf_p, wkv_p, *moe_bf = _time_mix_q(ps, 0, zero_shift, None, wts, None, None, True, l, depth,
                                                     bp, tp, n_par, casts=moe_2d if ride else ())
            if not ride:
                moe_bf = [w.astype(bf16) for w in moe_2d]
            moe_bf = [w.reshape(s.shape) for w, s in zip(moe_bf, (moe_w1, moe_w3, moe_w2))]
            yb_s, vf_s, wkv_s = _time_mix_q(ps, n_p, first_s, None, wts, state_wkv, None, True, l, depth, bs, ts, 1)
        else:
            yb_p, wkv_p = _time_mix_q(ps, 0, zero_shift, (outs[6], vf_p), wts, None, wkv_p, False, l, depth,
                                      bp, tp, n_par)
            yb_s, wkv_s = _time_mix_q(ps, n_p, first_s, (outs[6], vf_s), wts, state_wkv, wkv_s, False, l, depth,
                                      bs, ts, 1)

        w_short = jnp.tile(sgu_w[l][:, :ts, :ts], (1, CHUNK // ts, CHUNK // ts))
        sgu_w2 = jnp.stack([sgu_w[l], w_short])
        b_long = jnp.repeat(sgu_b[l].T, GROUP_A, axis=1)
        b_short = jnp.tile(b_long[:ts], (CHUNK // ts, 1))
        sgu_b2 = jnp.stack([b_long, b_short])
        x = _mix_out(u, va, yb_p, yb_s, gates, x, sgu_w2, sgu_mask, sgu_b2,
                     w_branch_a[l].astype(bf16), w_branch_b[l].astype(bf16), w_out[l].astype(bf16))

        j = l // 2
        if l % 2 == 0:
            x = _ffn_dense(x, row(norm_ffn_g[l]), ffn_w1[j].astype(bf16), ffn_w3[j].astype(bf16),
                           ffn_w2[j].astype(bf16))
            delta = None
        else:
            hb, gate_t, rank_t, rank_c, cnt = _router(x, row(norm_ffn_g[l]), moe_router[j].T)
            cnt = cnt[:, :, 0].reshape(-1)
            ys = _experts(cnt, hb, rank_t, gate_t, moe_bf[0][j], moe_bf[1][j], moe_bf[2][j])
            delta = _combine(cnt, ys, rank_c)
            if l < depth - 1:
                x = x + delta
                delta = None

    yn_p, yn_s = _add_norm(x, jnp.zeros_like(x) if delta is None else delta, row(norm_final_g), n_p)
    return (yn_p.reshape(bp, tp, d), yn_s.reshape(bs, ts, d),
            wkv_p, jnp.stack(shift_p), wkv_s, jnp.stack(shift_s),
            jnp.stack(chunk_v))
```

```python
import functools

import jax
import jax.numpy as jnp
from jax import lax
from jax.experimental import pallas as pl
from jax.experimental.pallas import tpu as pltpu

f32 = jnp.float32
bf16 = jnp.bfloat16

D_MODEL = 1024
CHUNK = 128
D_A = D_MODEL
GROUP_A = 128
H_A = D_A // GROUP_A
D_B = D_MODEL
HEAD_B = 64
H_B = D_B // HEAD_B
LORA_W = 64
LORA_A = 64
LORA_G = 128
N_SHIFT = 3 * D_B + LORA_W + LORA_A + LORA_G
N_IN = 2 * D_A + N_SHIFT + 2 * D_MODEL
N_EXPERTS = 8
RMS_EPS = 1e-6
LN_EPS = 1e-5
GN_EPS = 64e-5

PAIR = 2 * HEAD_B
N_PAIR = H_B // 2
MIB = 1024 * 1024
VMEM_CAP_V7X = 56 * MIB


def _params(vmem_bytes, n_grid):
    return pltpu.CompilerParams(
        dimension_semantics=("arbitrary",) * n_grid,
        vmem_limit_bytes=int(min(max(vmem_bytes, 16 * MIB), VMEM_CAP_V7X)),
    )


def _const_spec(shape):
    nd = len(shape)
    return pl.BlockSpec(shape, lambda *_: (0,) * nd, pipeline_mode=pl.Buffered(1))


def _gelu(x):
    return 0.5 * x * (1.0 + lax.erf(x * (2.0 ** -0.5)))


def _sigmoid(x):
    return 1.0 / (1.0 + jnp.exp(-x))


def _dot(a, b):
    return jnp.dot(a, b, preferred_element_type=f32)


def _dot_nt(a, b):
    return lax.dot_general(a, b, (((1,), (1,)), ((), ())), preferred_element_type=f32)


def _dot_tn(a, b):
    return lax.dot_general(a, b, (((0,), (0,)), ((), ())), preferred_element_type=f32)


def _head_sum(x, bd_ref):
    w = bd_ref.shape[0]
    cols = [_dot(x[:, c:c + w].astype(bf16), bd_ref[...]) for c in range(0, x.shape[1], w)]
    return jnp.concatenate(cols, axis=1)


def _rows_or_pair(x, tm, n_long_tiles):
    if not isinstance(x, tuple):
        return [pl.BlockSpec((tm, D_MODEL), lambda i: (i, 0))], [x], x.shape[0]
    specs = [pl.BlockSpec((tm, D_MODEL), lambda i: (jnp.minimum(i, n_long_tiles - 1), 0)),
             pl.BlockSpec((tm, D_MODEL), lambda i: (jnp.maximum(i - n_long_tiles, 0), 0))]
    return specs, list(x), x[0].shape[0] + x[1].shape[0]


IN_TM = 512
IN_CW = 512


def _in_proj_body(n_long_tiles, split_x, has_v, x_ref, *rest):
    if split_x:
        xs_ref, rest = rest[0], rest[1:]
    g_ref, w_ref, lng_ref, lnb_ref, *rest = rest
    if has_v:
        v1_ref, v2_ref, v0_ref, u_ref, va_ref, vs_ref, ps_ref, tail_ref, gate_ref, vg_ref, h_scr, t_scr = rest
    else:
        u_ref, va_ref, vs_ref, ps_ref, tail_ref, gate_ref, h_scr, t_scr = rest
    x = x_ref[...]
    if split_x:
        x = jnp.where(pl.program_id(0) < n_long_tiles, x, xs_ref[...])
    tm = x.shape[0]
    h = x * lax.rsqrt(jnp.mean(x * x, axis=-1, keepdims=True) + RMS_EPS) * g_ref[...]
    h_scr[...] = h.astype(bf16)

    def mm(c0, c1):
        return _dot(h_scr[...], w_ref[:, c0:c1])

    for c in range(0, D_A, IN_CW):
        u_ref[:, c:c + IN_CW] = _gelu(mm(c, c + IN_CW)).astype(bf16)
    for c in range(0, D_A, IN_CW):
        t_scr[:, c:c + IN_CW] = _gelu(mm(D_A + c, D_A + c + IN_CW))
    t = t_scr[...]
    mu = jnp.mean(t, axis=-1, keepdims=True)
    d = t - mu
    var = jnp.mean(d * d, axis=-1, keepdims=True)
    va = d * lax.rsqrt(var + LN_EPS) * lng_ref[...] + lnb_ref[...]
    va_ref[...] = va.astype(bf16)

    @pl.when(pl.program_id(0) >= n_long_tiles)
    def _():
        vs_ref[...] = va

    for c in range(0, N_SHIFT, IN_CW):
        c1 = min(c + IN_CW, N_SHIFT)
        ps_ref[:, c:c1] = mm(2 * D_A + c, 2 * D_A + c1)
    tail_ref[...] = ps_ref[tm - 8:tm, :]
    base = 2 * D_A + N_SHIFT
    for c in range(0, 2 * D_MODEL, IN_CW):
        gate_ref[:, c:c + IN_CW] = _sigmoid(mm(base + c, base + c + IN_CW)).astype(bf16)
    if has_v:
        lv = _dot(h_scr[...], v1_ref[...])
        vg_ref[...] = _sigmoid(v0_ref[...] + _dot(lv.astype(bf16), v2_ref[...])).astype(bf16)


def _in_proj(x, n_long, g, w_in, lng, lnb, vparams):
    tm = IN_TM
    nl = n_long // tm
    has_v = vparams is not None
    row = lambda w: pl.BlockSpec((tm, w), lambda i: (i, 0))
    x_specs, x_args, n = _rows_or_pair(x, tm, nl)
    in_specs = x_specs + [_const_spec((1, D_MODEL)), _const_spec((D_MODEL, N_IN)),
                          _const_spec((1, D_A)), _const_spec((1, D_A))]
    args = x_args + [g, w_in, lng, lnb]
    out_shape = [jax.ShapeDtypeStruct((n, D_A), bf16), jax.ShapeDtypeStruct((n, D_A), bf16),
                 jax.ShapeDtypeStruct((n - n_long, D_A), f32),
                 jax.ShapeDtypeStruct((n, N_SHIFT), f32), jax.ShapeDtypeStruct((n // tm, 8, N_SHIFT), f32),
                 jax.ShapeDtypeStruct((n, 2 * D_MODEL), bf16)]
    out_specs = [row(D_A), row(D_A), pl.BlockSpec((tm, D_A), lambda i: (jnp.maximum(i - nl, 0), 0)),
                 row(N_SHIFT), pl.BlockSpec((None, 8, N_SHIFT), lambda i: (i, 0, 0)), row(2 * D_MODEL)]
    if has_v:
        v1, v2, v0 = vparams
        in_specs += [_const_spec(v1.shape), _const_spec(v2.shape), _const_spec((1, D_B))]
        args += [v1, v2, v0]
        out_shape.append(jax.ShapeDtypeStruct((n, D_B), bf16))
        out_specs.append(row(D_B))
    out_bytes = 2 * (2 * D_A + 2 * D_MODEL + (D_B if has_v else 0)) + 4 * (N_SHIFT + D_A)
    vmem = 2 * D_MODEL * N_IN + 2 * tm * (4 * D_MODEL + out_bytes) + 8 * tm * D_MODEL * 4 + 4 * MIB
    return pl.pallas_call(
        functools.partial(_in_proj_body, nl, len(x_args) == 2, has_v),
        grid=(n // tm,), in_specs=in_specs, out_specs=out_specs, out_shape=out_shape,
        scratch_shapes=[pltpu.VMEM((tm, D_MODEL), bf16), pltpu.VMEM((tm, D_A), f32)],
        compiler_params=_params(vmem, 1), name="in_proj",
    )(*args)


N_MIX_W = 11
SCAN_ROWS = 64
TIME_MIX_PARTS = 2
N_OPS_BF, N_OPS_F32 = 7, 3


def _tmq_outer(*args):
    for parity in (0, 1):
        @pl.when(pl.program_id(0) % 2 == parity)
        def _(parity=parity):
            _tmq_body(parity, *args)


def _tmq_body(parity, n_par, n_seq, slen, long_seq, n_chunk, n_blocks, has_v, has_s0, emit_v, n_alias, n_cast,
              *refs):
    take = lambda k: (refs[:k], refs[k:])
    ps_refs, refs = take(n_par)
    prevblk_refs, refs = take(n_par)
    (first_ref,), refs = take(1)
    vg_refs = vf_refs = None
    if has_v:
        vg_refs, refs = take(n_par)
        vf_refs, refs = take(n_par)
    (mu_ref, wa2_ref, w0_ref, a0_ref, g2_ref, kk_ref, ka_ref, rk_ref, bd_ref, lng_ref, lnb_ref), refs = take(N_MIX_W)
    (tri_ref, tot_ref), refs = take(2)
    if has_s0:
        (s0_ref,), refs = take(1)
    cast_src, refs = take(n_cast)
    refs = refs[n_alias:]
    (yb_ref,), refs = take(1)
    if emit_v:
        (vo_ref,), refs = take(1)
    (sl_ref,), refs = take(1)
    cast_dst, (s_scr, opb_scr, opf_scr) = take(n_cast)
    for src, dst in zip(cast_src, cast_dst):
        dst[...] = src[...].astype(bf16)
    j = pl.program_id(0)
    blk_prep = jnp.minimum(j, n_blocks - 1)
    c = jnp.maximum(j - 1, 0) % n_chunk
    par = range(n_par)
    pairs = range(N_PAIR)

    if parity == 0:
        @pl.when(j == 0)
        def _():
            opb_scr[1] = jnp.zeros(opb_scr.shape[1:], bf16)
            opf_scr[1] = jnp.zeros(opf_scr.shape[1:], f32)

    @pl.when(c == 0)
    def _():
        if has_s0:
            zero = jnp.zeros((HEAD_B, HEAD_B), f32)
            for q in par:
                def pack(g, carry, q=q):
                    for p in pairs:
                        top = jnp.concatenate([s0_ref[q, g, 2 * p], zero], axis=1)
                        bot = jnp.concatenate([zero, s0_ref[q, g, 2 * p + 1]], axis=1)
                        s_scr[q, p, pl.ds(pl.multiple_of(g * PAIR, PAIR), PAIR), :] = (
                            jnp.concatenate([top, bot], axis=0))
                    return carry

                lax.fori_loop(0, n_seq, pack, 0)
        else:
            s_scr[...] = jnp.zeros(s_scr.shape, f32)

    row = lax.broadcasted_iota(jnp.int32, (SCAN_ROWS, 1), 0)
    nxt = [dict() for _ in par]
    for q in par:
        ps = ps_refs[q][...]
        if long_seq:
            seq = q * (n_blocks // n_chunk) + blk_prep // n_chunk
            init = jnp.where(blk_prep % n_chunk == 0, first_ref[pl.ds(seq, 1), :], prevblk_refs[q][7:8, :])
            starts = row == 0
        else:
            init = first_ref[...]
            starts = row % slen == 0
        prev = jnp.where(starts, init, pltpu.roll(ps, 1, 0))
        xs = ps + mu_ref[...] * (prev - ps)
        dwa = xs[:, 3 * D_B:3 * D_B + LORA_W + LORA_A]
        dg = xs[:, 3 * D_B + LORA_W + LORA_A:N_SHIFT]
        lane = lax.broadcasted_iota(jnp.int32, dwa.shape, 1)
        lhs = jnp.where(lane < LORA_W, jnp.tanh(dwa), dwa).astype(bf16)
        kkr = xs[:, D_B:2 * D_B] * kk_ref[...]
        nxt[q].update(r=xs[:, 0:D_B], k_raw=xs[:, D_B:2 * D_B], v=xs[:, 2 * D_B:3 * D_B],
                      wa=_dot(lhs, wa2_ref[...]),
                      g=_dot(_sigmoid(dg).astype(bf16), g2_ref[...]),
                      kkr=kkr, kk_norm2=_head_sum(kkr * kkr, bd_ref))

    ops = [[opb_scr[1 - parity, q, k] for k in range(N_OPS_BF)] + [opf_scr[1 - parity, q, k] for k in range(N_OPS_F32)]
           for q in par]

    n_st = 2 * SCAN_ROWS

    def causal(width, inclusive):
        ri = lax.broadcasted_iota(jnp.int32, (n_st, width), 0)
        ci = lax.broadcasted_iota(jnp.int32, (n_st, width), 1) % n_st
        same = (ri // SCAN_ROWS == ci // SCAN_ROWS) & ((ri % SCAN_ROWS) // slen == (ci % SCAN_ROWS) // slen)
        t_r, t_c = ri % SCAN_ROWS, ci % SCAN_ROWS
        return same & ((t_c <= t_r) if inclusive else (t_c < t_r))

    incl2 = causal(2 * n_st, True)
    strict = causal(n_st, False)
    lane_e = lax.broadcasted_iota(jnp.int32, (1, PAIR), 1) < HEAD_B
    row_seq = ((lax.broadcasted_iota(jnp.int32, (2 * n_st, 1), 0) % SCAN_ROWS) // slen)

    def stack(z):
        return jnp.concatenate([jnp.where(lane_e, z, 0.0), jnp.where(lane_e, 0.0, z)], axis=0).astype(bf16)

    def pick_seq(big):
        if n_seq == 1:
            return big
        acc = jnp.where(row_seq == 0, big[:, :PAIR], 0.0)
        for g in range(1, n_seq):
            acc = acc + jnp.where(row_seq == g, big[:, g * PAIR:(g + 1) * PAIR], 0.0)
        return acc

    def spread_seq(z):
        if n_seq == 1:
            return z
        return jnp.concatenate([jnp.where(row_seq == g, z, jnp.zeros_like(z)) for g in range(n_seq)], axis=1)

    sls = [slice(p * PAIR, (p + 1) * PAIR) for p in pairs]
    chains = [(q, p) for q in par for p in pairs]
    ar, bk, vb = {}, {}, {}
    for q, p in chains:
        at, rt, bt, kt, _, _, vv = ops[q][:N_OPS_BF]
        s = sls[p]
        ar[q, p] = jnp.concatenate([stack(at[:, s]), stack(rt[:, s])], axis=0)
        bk[q, p] = jnp.concatenate([bt[:, s], bt[:, s], kt[:, s], kt[:, s]], axis=0).astype(bf16)
        vb[q, p] = stack(vv[:, s])
    pm = {ch: _dot_nt(ar[ch], bk[ch]) for ch in chains}
    qs = {(q, p): pick_seq(_dot_nt(ar[q, p], s_scr[q, p].astype(bf16))) for q, p in chains}
    x = {ch: qs[ch][:n_st] + _dot(jnp.where(strict, pm[ch][:n_st, n_st:], 0.0).astype(bf16), vb[ch])
         for ch in chains}
    m = {ch: jnp.where(strict, pm[ch][:n_st, :n_st], 0.0).astype(bf16) for ch in chains}

    for q in par:
        d = nxt[q]
        w_log = -jax.nn.softplus(-(w0_ref[...] + d["wa"][:, :D_B])) - 0.5
        lw = -jnp.exp(w_log)
        a_new = _sigmoid(a0_ref[...] + d["wa"][:, D_B:])
        k_new = d["k_raw"] * (1.0 + (a_new - 1.0) * ka_ref[...])
        v_mix = d["v"]
        if has_v:
            v_mix = v_mix + (vf_refs[q][...].astype(f32) - v_mix) * vg_refs[q][...].astype(f32)
        if emit_v:
            vo_ref[q] = v_mix.astype(bf16)
        d.update(lw=lw, a=a_new, k=k_new, v=v_mix,
                 cum=jnp.dot(tri_ref[...], lw, precision=lax.Precision.HIGHEST, preferred_element_type=f32),
                 tot=jnp.dot(tot_ref[...], lw, precision=lax.Precision.HIGHEST, preferred_element_type=f32),
                 rk_sum=_head_sum(d["r"] * k_new * rk_ref[...], bd_ref))

    span = 1
    while span < slen:
        span *= 2
        if span < slen:
            z = {ch: _dot(m[ch], jnp.concatenate([x[ch].astype(bf16), m[ch]], axis=1)) for ch in chains}
            x = {ch: x[ch] + z[ch][:, :n_st] for ch in chains}
            m = {ch: z[ch][:, n_st:].astype(bf16) for ch in chains}
        else:
            x = {ch: x[ch] + _dot(m[ch], x[ch].astype(bf16)) for ch in chains}
    uv = {ch: jnp.concatenate([x[ch].astype(bf16), vb[ch]], axis=0) for ch in chains}
    ys = {}
    for ch in chains:
        yp = qs[ch][n_st:] + _dot(jnp.where(incl2, pm[ch][n_st:], 0.0).astype(bf16), uv[ch])
        ys[ch] = yp[:SCAN_ROWS] + yp[SCAN_ROWS:]
    for q in par:
        g_last, gg, bonus = ops[q][N_OPS_BF:]
        bh, kh = ops[q][4], ops[q][5]
        y = jnp.concatenate([ys[q, p] for p in pairs], axis=1)
        mean = _head_sum(y, bd_ref) * (1.0 / HEAD_B)
        dev = y - mean
        var = _head_sum(dev * dev, bd_ref) * (1.0 / HEAD_B)
        yn = dev * lax.rsqrt(var + GN_EPS) * lng_ref[...] + lnb_ref[...] + bonus
        yb_ref[q] = (yn * gg).astype(bf16)
        for p in pairs:
            bkh = jnp.concatenate([stack(bh[:, sls[p]]), stack(kh[:, sls[p]])], axis=0)
            upd = _dot_tn(spread_seq(uv[q, p]), bkh)
            for g in range(n_seq):
                rows = slice(g * PAIR, (g + 1) * PAIR)
                s_scr[q, p, rows, :] = (s_scr[q, p, rows, :] * g_last[g * slen:g * slen + 1, sls[p]] + upd[rows])

    for q in par:
        d = nxt[q]
        kk_new = d["kkr"] * lax.rsqrt(d["kk_norm2"] + 1e-12)
        b_new = kk_new * d["a"]
        cum, tot, lw = d["cum"], d["tot"], d["lw"]
        g_inv = jnp.exp(-cum)
        g_tail = jnp.exp(tot - cum)
        folded = (-kk_new * jnp.exp(cum - lw), d["r"] * jnp.exp(cum), b_new * g_inv, d["k"] * g_inv,
                  b_new * g_tail, d["k"] * g_tail, d["v"])
        for idx, val in enumerate(folded):
            opb_scr[parity, q, idx] = val.astype(bf16)
        for idx, val in enumerate((jnp.exp(tot), d["g"], d["rk_sum"] * d["v"])):
            opf_scr[parity, q, idx] = val

    @pl.when(c == n_chunk - 1)
    def _():
        for q in par:
            def unpack(g, carry, q=q):
                for p in pairs:
                    tile = s_scr[q, p, pl.ds(pl.multiple_of(g * PAIR, PAIR), PAIR), :]
                    sl_ref[q, g, 2 * p] = tile[:HEAD_B, :HEAD_B]
                    sl_ref[q, g, 2 * p + 1] = tile[HEAD_B:, HEAD_B:]
                return carry

            lax.fori_loop(0, n_seq, unpack, 0)


def _time_mix_q(ps, row0, first, vextra, wts, s0, s_prev, emit_v, layer, depth, n_seq_total, slen, n_par, casts=()):
    long_seq = slen >= SCAN_ROWS
    n_rows = n_seq_total * slen
    if long_seq:
        n_seq, clen, n_chunk = 1, SCAN_ROWS, slen // SCAN_ROWS
    else:
        n_seq, clen, n_chunk = SCAN_ROWS // slen, slen, 1
    n_blocks = n_rows // SCAN_ROWS // n_par
    seq_per_part = n_seq_total // n_par
    blk0 = row0 // SCAN_ROWS
    prep = lambda j: jnp.minimum(j, n_blocks - 1)
    run = lambda j: jnp.maximum(j - 1, 0)
    part = lambda q, b: q * n_blocks + b
    tok_in = lambda w, q, off: pl.BlockSpec((SCAN_ROWS, w), lambda j: (off + part(q, prep(j)), 0))
    st = pl.BlockSpec((None, n_par, n_seq, H_B, HEAD_B, HEAD_B), lambda j: (layer, 0, run(j) // n_chunk, 0, 0, 0))
    sq = pl.BlockSpec((SCAN_ROWS, SCAN_ROWS), lambda j: (0, 0))
    t = jnp.arange(SCAN_ROWS)
    same_seq = t[:, None] // clen == t[None, :] // clen
    tri = (same_seq & (t[None, :] <= t[:, None])).astype(f32)
    in_specs = [tok_in(N_SHIFT, q, blk0) for q in range(n_par)]
    in_specs += [pl.BlockSpec((8, N_SHIFT), lambda j, q=q: (
        jnp.maximum((blk0 + part(q, prep(j))) * (SCAN_ROWS // 8) - 1, 0), 0)) for q in range(n_par)]
    in_specs.append(_const_spec(first.shape) if long_seq
                    else pl.BlockSpec((SCAN_ROWS, N_SHIFT), lambda j: (prep(j), 0)))
    args = [ps] * (2 * n_par) + [first]
    if vextra is not None:
        in_specs += [tok_in(D_B, q, blk0) for q in range(n_par)] + [tok_in(D_B, q, 0) for q in range(n_par)]
        args += [vextra[0]] * n_par + [vextra[1]] * n_par
    in_specs += [_const_spec(w.shape) for w in wts] + [sq, sq]
    args += list(wts) + [tri, same_seq.astype(f32)]
    grouped = (depth, n_par, seq_per_part, H_B, HEAD_B, HEAD_B)
    if s0 is not None:
        in_specs.append(st)
        args.append(s0.reshape(grouped))
    cast_specs = [pl.BlockSpec((a.shape[0] // n_blocks, a.shape[1]), lambda j: (prep(j), 0)) for a in casts]
    in_specs += cast_specs
    args += list(casts)
    aliases = {}
    if s_prev is not None:
        aliases[len(args)] = 2 if emit_v else 1
        in_specs.append(pl.BlockSpec(memory_space=pl.ANY))
        args.append(s_prev.reshape(grouped))
    out_run = pl.BlockSpec((n_par, SCAN_ROWS, D_B), lambda j: (0, run(j), 0))
    out_prep = pl.BlockSpec((n_par, SCAN_ROWS, D_B), lambda j: (0, prep(j), 0))
    rows3 = jax.ShapeDtypeStruct((n_par, n_rows // n_par, D_B), bf16)
    out_specs = [out_run] + ([out_prep] if emit_v else []) + [st] + cast_specs
    out_shape = ([rows3] * (2 if emit_v else 1) + [jax.ShapeDtypeStruct(grouped, f32)]
                 + [jax.ShapeDtypeStruct(a.shape, bf16) for a in casts])
    state_pad = 2 * n_par * n_seq * H_B * HEAD_B * PAIR * 4
    vmem = (n_par * (2 * SCAN_ROWS * (2 * N_SHIFT * 4 + 6 * D_B * 2) + N_PAIR * n_seq * PAIR * PAIR * 4
                     + 2 * (N_OPS_BF * 2 + N_OPS_F32 * 4) * SCAN_ROWS * D_B + 48 * SCAN_ROWS * D_B * 4)
            + 2 * state_pad * (2 if s0 is not None else 1) + 8 * n_seq * PAIR * PAIR * 4 * 4 + 8 * MIB)
    outs = pl.pallas_call(
        functools.partial(_tmq_outer, n_par, n_seq, clen, long_seq, n_chunk, n_blocks, vextra is not None,
                          s0 is not None, emit_v, len(aliases), len(casts)),
        grid=(n_blocks + 1,), in_specs=in_specs, out_specs=out_specs, out_shape=out_shape,
        scratch_shapes=[pltpu.VMEM((n_par, N_PAIR, n_seq * PAIR, PAIR), f32),
                        pltpu.VMEM((2, n_par, N_OPS_BF, SCAN_ROWS, D_B), bf16),
                        pltpu.VMEM((2, n_par, N_OPS_F32, SCAN_ROWS, D_B), f32)],
        input_output_aliases=aliases,
        compiler_params=_params(vmem, 1), name="time_mix",
    )(*args)
    n_tok = 2 if emit_v else 1
    flat = [o.reshape(n_rows, D_B) for o in outs[:n_tok]]
    states = outs[n_tok].reshape(depth, n_seq_total, H_B, HEAD_B, HEAD_B)
    return (*flat, states, *outs[n_tok + 1:])


MIX_TM = 512


def _mix_body(n_long_tiles, split_x, u_ref, va_ref, ybl_ref, ybs_ref, gate_ref, x_ref, *rest):
    if split_x:
        xs_ref, rest = rest[0], rest[1:]
    sw_ref, sm_ref, sb_ref, wa_ref, wb_ref, wo_ref, o_ref, ya_scr = rest
    tm = u_ref.shape[0]
    is_long = pl.program_id(0) < n_long_tiles
    yb = jnp.where(is_long, ybl_ref[...], ybs_ref[...])
    x = x_ref[...]
    if split_x:
        x = jnp.where(is_long, x, xs_ref[...])
    for r0 in range(0, tm, CHUNK):
        for grp in range(H_A):
            cs = slice(grp * GROUP_A, (grp + 1) * GROUP_A)
            w = jnp.where(sm_ref[...] > 0.5, sw_ref[grp], 0.0).astype(bf16)
            mixed = _dot(w, va_ref[r0:r0 + CHUNK, cs].astype(bf16)) + sb_ref[:, cs]
            ya_scr[r0:r0 + CHUNK, cs] = (u_ref[r0:r0 + CHUNK, cs] * mixed).astype(bf16)
    merged = (gate_ref[:, :D_MODEL] * _dot(ya_scr[...], wa_ref[...])
              + gate_ref[:, D_MODEL:] * _dot(yb, wb_ref[...]))
    o_ref[...] = x + _dot(merged.astype(bf16), wo_ref[...])


def _mix_out(u, va, yb_long, yb_short, gates, x, sgu_w2, sgu_m2, sgu_b2, wa, wb, wo):
    tm = MIX_TM
    n_long = yb_long.shape[0]
    nl = n_long // tm
    row = lambda w: pl.BlockSpec((tm, w), lambda i: (i, 0))
    x_specs, x_args, n = _rows_or_pair(x, tm, nl)
    kind = lambda i: jnp.where(i * tm >= n_long, 1, 0)
    in_specs = [row(D_A), row(D_A),
                pl.BlockSpec((tm, D_B), lambda i: (jnp.minimum(i, nl - 1), 0)),
                pl.BlockSpec((tm, D_B), lambda i: (jnp.maximum(i - nl, 0), 0)),
                row(2 * D_MODEL)] + x_specs + [
                pl.BlockSpec((None, H_A, CHUNK, CHUNK), lambda i: (kind(i), 0, 0, 0)),
                pl.BlockSpec((None, CHUNK, CHUNK), lambda i: (kind(i), 0, 0)),
                pl.BlockSpec((None, CHUNK, D_A), lambda i: (kind(i), 0, 0)),
                _const_spec(wa.shape), _const_spec(wb.shape), _const_spec(wo.shape)]
    vmem = 2 * tm * (2 * 5 + 4 * 2) * D_MODEL + 3 * 2 * D_MODEL * D_MODEL + 10 * tm * D_MODEL * 4 + 6 * MIB
    return pl.pallas_call(
        functools.partial(_mix_body, nl, len(x_args) == 2), grid=(n // tm,), in_specs=in_specs,
        out_specs=row(D_MODEL),
        out_shape=jax.ShapeDtypeStruct((n, D_MODEL), f32),
        scratch_shapes=[pltpu.VMEM((tm, D_A), bf16)],
        compiler_params=_params(vmem, 1), name="mix_out",
    )(u, va, yb_long, yb_short, gates, *x_args, sgu_w2, sgu_m2, sgu_b2, wa, wb, wo)


FFN_TM = 512
FFN_CW = 256


def _ffn_body(x_ref, g_ref, w1_ref, w3_ref, w2_ref, o_ref, h_scr):
    x = x_ref[...]
    h = x * lax.rsqrt(jnp.mean(x * x, axis=-1, keepdims=True) + RMS_EPS) * g_ref[...]
    h_scr[...] = h.astype(bf16)
    o_ref[...] = x
    d_ff = w1_ref.shape[1]
    for c in range(0, d_ff, FFN_CW):
        a = _dot(h_scr[...], w1_ref[:, c:c + FFN_CW])
        b = _dot(h_scr[...], w3_ref[:, c:c + FFN_CW])
        t = (a * _sigmoid(a) * b).astype(bf16)
        o_ref[...] += _dot(t, w2_ref[c:c + FFN_CW, :])


def _ffn_dense(x, g, w1, w3, w2):
    n = x.shape[0]
    tm = FFN_TM
    row = pl.BlockSpec((tm, D_MODEL), lambda i: (i, 0))
    vmem = 3 * 2 * D_MODEL * w1.shape[1] + 4 * tm * D_MODEL * 4 + 8 * tm * D_MODEL * 4 + 4 * MIB
    return pl.pallas_call(
        _ffn_body, grid=(n // tm,),
        in_specs=[row, _const_spec((1, D_MODEL)), _const_spec(w1.shape), _const_spec(w3.shape),
                  _const_spec(w2.shape)],
        out_specs=row, out_shape=jax.ShapeDtypeStruct((n, D_MODEL), f32),
        scratch_shapes=[pltpu.VMEM((tm, D_MODEL), bf16)],
        compiler_params=_params(vmem, 1), name="ffn_dense",
    )(x, g, w1, w3, w2)


MOE_TM = 1408
MOE_BLOCK = 256
MOE_TAIL = 128
MOE_ONE_SHOT = 3


def _split2(x):
    hi = x.astype(bf16)
    return hi, (x - hi.astype(f32)).astype(bf16)


def _router_body(x_ref, g_ref, wr_ref, hb_ref, gate_t_ref, rank_t_ref, rank_c_ref, cnt_ref):
    x = x_ref[...]
    tm = x.shape[0]
    h = x * lax.rsqrt(jnp.mean(x * x, axis=-1, keepdims=True) + RMS_EPS) * g_ref[...]
    hb_ref[...] = h.astype(bf16)
    h_hi, h_lo = _split2(h)
    w_hi, w_lo = _split2(wr_ref[...])
    both = _dot_nt(jnp.concatenate([w_hi, w_lo], axis=0), h_hi)
    logits = both[:N_EXPERTS] + both[N_EXPERTS:] + _dot_nt(w_hi, h_lo)
    eid = lax.broadcasted_iota(jnp.int32, logits.shape, 0)
    m1 = jnp.max(logits, axis=0, keepdims=True)
    i1 = jnp.min(jnp.where(logits == m1, eid, N_EXPERTS), axis=0, keepdims=True)
    sel1 = eid == i1
    rest = jnp.where(sel1, -jnp.inf, logits)
    m2 = jnp.max(rest, axis=0, keepdims=True)
    i2 = jnp.min(jnp.where(rest == m2, eid, N_EXPERTS), axis=0, keepdims=True)
    sel2 = eid == i2
    e2 = jnp.exp(m2 - m1)
    den = 1.0 + e2
    gate_t = jnp.where(sel1, 1.0 / den, 0.0) + jnp.where(sel2, e2 / den, 0.0)
    sel = jnp.where(sel1 | sel2, 1.0, 0.0)
    s_idx = lax.broadcasted_iota(jnp.int32, (tm, tm), 0)
    t_idx = lax.broadcasted_iota(jnp.int32, (tm, tm), 1)
    before = jnp.where(s_idx < t_idx, 1.0, 0.0).astype(bf16)
    rank = _dot(sel.astype(bf16), before)
    rank_t = jnp.where(sel > 0.5, rank, -1.0)
    gate_t_ref[...] = gate_t
    rank_t_ref[...] = rank_t
    rank_c_ref[...] = rank_t.T
    cnt = jnp.sum(sel, axis=1, keepdims=True)
    cnt_ref[...] = jnp.broadcast_to(cnt, cnt_ref.shape).astype(jnp.int32)


def _router(x, g, wr_t):
    n = x.shape[0]
    tm = MOE_TM
    nt = n // tm
    vmem = 2 * tm * D_MODEL * 6 + 8 * tm * D_MODEL * 4 + 3 * tm * tm * 4 + 8 * MIB
    return pl.pallas_call(
        _router_body, grid=(nt,),
        in_specs=[pl.BlockSpec((tm, D_MODEL), lambda i: (i, 0)), _const_spec((1, D_MODEL)),
                  _const_spec(wr_t.shape)],
        out_specs=[pl.BlockSpec((tm, D_MODEL), lambda i: (i, 0)),
                   pl.BlockSpec((N_EXPERTS, tm), lambda i: (0, i)),
                   pl.BlockSpec((N_EXPERTS, tm), lambda i: (0, i)),
                   pl.BlockSpec((tm, N_EXPERTS), lambda i: (i, 0)),
                   pl.BlockSpec((None, N_EXPERTS, 128), lambda i: (i, 0, 0))],
        out_shape=[jax.ShapeDtypeStruct((n, D_MODEL), bf16),
                   jax.ShapeDtypeStruct((N_EXPERTS, n), f32), jax.ShapeDtypeStruct((N_EXPERTS, n), f32),
                   jax.ShapeDtypeStruct((n, N_EXPERTS), f32),
                   jax.ShapeDtypeStruct((nt, N_EXPERTS, 128), jnp.int32)],
        compiler_params=_params(vmem, 1), name="moe_router",
    )(x, g, wr_t)


MOE_FS = 1792


def _moe_blocks(cnt, fn):
    n_full = cnt // MOE_BLOCK
    rem = cnt - n_full * MOE_BLOCK
    n_big = n_full + jnp.where(rem > MOE_TAIL, 1, 0)
    n_tail = (cnt + MOE_TAIL - 1) // MOE_TAIL
    for k in range(1, MOE_ONE_SHOT + 1):
        @pl.when(n_tail == k)
        def _(k=k):
            fn(0, k * MOE_TAIL)

    @pl.when(n_tail > MOE_ONE_SHOT)
    def _():
        def body(j, carry):
            fn(pl.multiple_of(j * MOE_BLOCK, MOE_BLOCK), MOE_BLOCK)
            return carry

        lax.fori_loop(0, n_big, body, 0)

        @pl.when((rem > 0) & (rem <= MOE_TAIL))
        def _():
            fn(pl.multiple_of(n_full * MOE_BLOCK, MOE_TAIL), MOE_TAIL)


def _slots(r0, rows, axis):
    shape = (rows, 1) if axis == 0 else (1, rows)
    return (lax.broadcasted_iota(jnp.int32, shape, axis) + r0).astype(f32)


def _experts_body(cnt_ref, hb_ref, rank_t_ref, gate_t_ref, w1_ref, w3_ref, w2_ref, ys_ref):
    e, i = pl.program_id(0), pl.program_id(1)
    rank_row = rank_t_ref[pl.ds(e, 1), :]
    gate_row = gate_t_ref[pl.ds(e, 1), :]
    d_ff = w2_ref.shape[0]
    ys_ref[...] = jnp.zeros(ys_ref.shape, bf16)

    def run(r0, rows):
        match = rank_row == _slots(r0, rows, 0)
        xb = _dot(jnp.where(match, 1.0, 0.0).astype(bf16), hb_ref[...]).astype(bf16)
        gate_blk = jnp.sum(jnp.where(match, gate_row, 0.0), axis=1, keepdims=True)
        acc = jnp.zeros((rows, D_MODEL), f32)
        for s in range(0, d_ff, MOE_FS):
            a = _dot(xb, w1_ref[:, s:s + MOE_FS])
            b = _dot(xb, w3_ref[:, s:s + MOE_FS])
            acc = acc + _dot((a * _sigmoid(a) * b).astype(bf16), w2_ref[s:s + MOE_FS, :])
        ys_ref[pl.ds(r0, rows), :] = (acc * gate_blk).astype(bf16)

    _moe_blocks(cnt_ref[i * N_EXPERTS + e], run)


def _experts(cnt, hb, rank_t, gate_t, w1, w3, w2):
    n = hb.shape[0]
    tm = MOE_TM
    n_e, _, d_ff = w1.shape
    once = dict(pipeline_mode=pl.Buffered(1))
    meta = pl.BlockSpec((N_EXPERTS, tm), lambda e, i, cnt: (0, i))
    grid_spec = pltpu.PrefetchScalarGridSpec(
        num_scalar_prefetch=1, grid=(n_e, n // tm),
        in_specs=[pl.BlockSpec((tm, D_MODEL), lambda e, i, cnt: (i, 0)), meta, meta,
                  pl.BlockSpec((None, D_MODEL, d_ff), lambda e, i, cnt: (e, 0, 0), **once),
                  pl.BlockSpec((None, D_MODEL, d_ff), lambda e, i, cnt: (e, 0, 0), **once),
                  pl.BlockSpec((None, d_ff, D_MODEL), lambda e, i, cnt: (e, 0, 0), **once)],
        out_specs=pl.BlockSpec((None, None, tm, D_MODEL), lambda e, i, cnt: (e, i, 0, 0)))
    max_rows = MOE_ONE_SHOT * MOE_TAIL
    vmem = (3 * D_MODEL * d_ff * 2 + 2 * 2 * tm * D_MODEL * 2 + 4 * N_EXPERTS * tm * 4
            + max_rows * (tm * 6 + 3 * MOE_FS * 4 + 3 * D_MODEL * 4) + 6 * MIB)
    return pl.pallas_call(
        _experts_body, grid_spec=grid_spec,
        out_shape=jax.ShapeDtypeStruct((n_e, n // tm, tm, D_MODEL), bf16),
        compiler_params=_params(vmem, 2), name="moe_experts",
    )(cnt, hb, rank_t, gate_t, w1, w3, w2)


def _combine_body(cnt_ref, ys_ref, rank_c_ref, o_ref):
    i, e = pl.program_id(0), pl.program_id(1)
    tm = o_ref.shape[0]

    @pl.when(e == 0)
    def _():
        o_ref[...] = jnp.zeros(o_ref.shape, f32)

    lane8 = lax.broadcasted_iota(jnp.int32, (tm, N_EXPERTS), 1)
    rank_col = jnp.sum(jnp.where(lane8 == e, rank_c_ref[...], 0.0), axis=1, keepdims=True)

    def scatter(r0, rows):
        onehot = jnp.where(rank_col == _slots(r0, rows, 1), 1.0, 0.0).astype(bf16)
        o_ref[...] += _dot(onehot, ys_ref[pl.ds(r0, rows), :])

    _moe_blocks(cnt_ref[i * N_EXPERTS + e], scatter)


def _combine(cnt, ys, rank_c):
    n_e, nt, tm, _ = ys.shape
    grid_spec = pltpu.PrefetchScalarGridSpec(
        num_scalar_prefetch=1, grid=(nt, n_e),
        in_specs=[pl.BlockSpec((None, None, tm, D_MODEL), lambda i, e, cnt: (e, i, 0, 0)),
                  pl.BlockSpec((tm, N_EXPERTS), lambda i, e, cnt: (i, 0))],
        out_specs=pl.BlockSpec((tm, D_MODEL), lambda i, e, cnt: (i, 0)))
    vmem = 2 * tm * D_MODEL * (2 + 4) + 2 * tm * 128 * 4 + 3 * tm * D_MODEL * 4 + 4 * MIB
    return pl.pallas_call(
        _combine_body, grid_spec=grid_spec, out_shape=jax.ShapeDtypeStruct((nt * tm, D_MODEL), f32),
        compiler_params=_params(vmem, 2), name="moe_combine",
    )(cnt, ys, rank_c)


NORM_TM = 512


def _norm_body(n_long_tiles, x_ref, y_ref, g_ref, o_long_ref, o_short_ref):
    x = x_ref[...] + y_ref[...]
    val = x * lax.rsqrt(jnp.mean(x * x, axis=-1, keepdims=True) + RMS_EPS) * g_ref[...]
    i = pl.program_id(0)

    @pl.when(i < n_long_tiles)
    def _():
        o_long_ref[...] = val

    @pl.when(i >= n_long_tiles)
    def _():
        o_short_ref[...] = val


def _add_norm(x, y, g, n_long):
    n = x.shape[0]
    tm = NORM_TM
    nl = n_long // tm
    row = pl.BlockSpec((tm, D_MODEL), lambda i: (i, 0))
    return pl.pallas_call(
        functools.partial(_norm_body, nl), grid=(n // tm,),
        in_specs=[row, row, _const_spec((1, D_MODEL))],
        out_specs=[pl.BlockSpec((tm, D_MODEL), lambda i: (jnp.minimum(i, nl - 1), 0)),
                   pl.BlockSpec((tm, D_MODEL), lambda i: (jnp.maximum(i - nl, 0), 0))],
        out_shape=[jax.ShapeDtypeStruct((n_long, D_MODEL), f32),
                   jax.ShapeDtypeStruct((n - n_long, D_MODEL), f32)],
        compiler_params=_params(12 * tm * D_MODEL * 4, 1), name="final_norm",
    )(x, y, g)


def kernel(x_prompt, x_sample, state_wkv, state_shift, norm_mix_g, w_in, shift_mu, sgu_ln_g, sgu_ln_b, sgu_w, sgu_b, rwkv_w0, rwkv_w2, rwkv_a0, rwkv_a2, rwkv_g2, rwkv_v0, rwkv_v1, rwkv_v2, rwkv_k_k, rwkv_k_a, rwkv_r_k, rwkv_ln_g, rwkv_ln_b, w_branch_a, w_branch_b, w_out, norm_ffn_g, ffn_w1, ffn_w3, ffn_w2, moe_router, moe_w1, moe_w3, moe_w2, norm_final_g):
    bp, tp, d = x_prompt.shape
    bs, ts, _ = x_sample.shape
    depth = w_in.shape[0]
    n_p, n_s = bp * tp, bs * ts
    x = (x_prompt.reshape(n_p, d), x_sample.reshape(n_s, d))
    row = lambda a: a.reshape(1, -1)

    hid = jnp.arange(4 * HEAD_B) // HEAD_B
    bd = (hid[:, None] == hid[None, :]).astype(bf16)
    tpos = jnp.arange(CHUNK)
    mask_long = (tpos[None, :] <= tpos[:, None]).astype(f32)
    mask_short = ((tpos[None, :] // ts == tpos[:, None] // ts) & (tpos[None, :] <= tpos[:, None])).astype(f32)
    sgu_mask = jnp.stack([mask_long, mask_short])
    zero_shift = jnp.zeros((bp, N_SHIFT), f32)

    shift_p, shift_s, chunk_v = [], [], []
    wkv_p = wkv_s = None
    for l in range(depth):
        vparams = None
        if l > 0:
            vparams = (rwkv_v1[l - 1].astype(bf16), rwkv_v2[l - 1].astype(bf16), row(rwkv_v0[l - 1]))
        outs = _in_proj(x, n_p, row(norm_mix_g[l]), w_in[l].astype(bf16), row(sgu_ln_g[l]), row(sgu_ln_b[l]),
                        vparams)
        u, va, va_s, ps, tails, gates = outs[:6]
        tiles_per_seq = tp // IN_TM
        shift_p.append(tails[tiles_per_seq - 1:bp * tiles_per_seq:tiles_per_seq, 7])
        shift_s.append(ps[n_p + ts - 1::ts])
        chunk_v.append(va_s.reshape(bs, ts, D_A))

        zpad = jnp.zeros((LORA_W, D_B), f32)
        wa2 = jnp.concatenate([jnp.concatenate([rwkv_w2[l], zpad], axis=1),
                               jnp.concatenate([zpad, rwkv_a2[l]], axis=1)], axis=0).astype(bf16)
        wts = (row(shift_mu[l]), wa2, row(rwkv_w0[l]), row(rwkv_a0[l]), rwkv_g2[l].astype(bf16),
               row(rwkv_k_k[l]), row(rwkv_k_a[l]), row(rwkv_r_k[l]), bd, row(rwkv_ln_g[l]), row(rwkv_ln_b[l]))
        first_s = jnp.pad(state_shift[l][:, None, :], ((0, 0), (0, ts - 1), (0, 0))).reshape(n_s, N_SHIFT)
        n_par = TIME_MIX_PARTS if bp % TIME_MIX_PARTS == 0 else 1
        if l == 0:
            n_steps = n_p // SCAN_ROWS // n_par
            moe_2d = [w.reshape(-1, w.shape[-1]) for w in (moe_w1, moe_w3, moe_w2)]
            ride = all(w.shape[0] % (16 * n_steps) == 0 for w in moe_2d)
            yb_p, vf_p, wkv_p, *moe_bf = _time_mix_q(ps, 0, zero_shift, None, wts, None, None, True, l, depth,
                                                     bp, tp, n_par, casts=moe_2d if ride else ())
            if not ride:
                moe_bf = [w.astype(bf16) for w in moe_2d]
            moe_bf = [w.reshape(s.shape) for w, s in zip(moe_bf, (moe_w1, moe_w3, moe_w2))]
            yb_s, vf_s, wkv_s = _time_mix_q(ps, n_p, first_s, None, wts, state_wkv, None, True, l, depth, bs, ts, 1)
        else:
            yb_p, wkv_p = _time_mix_q(ps, 0, zero_shift, (outs[6], vf_p), wts, None, wkv_p, False, l, depth,
                                      bp, tp, n_par)
            yb_s, wkv_s = _time_mix_q(ps, n_p, first_s, (outs[6], vf_s), wts, state_wkv, wkv_s, False, l, depth,
                                      bs, ts, 1)

        w_short = jnp.tile(sgu_w[l][:, :ts, :ts], (1, CHUNK // ts, CHUNK // ts))
        sgu_w2 = jnp.stack([sgu_w[l], w_short])
        b_long = jnp.repeat(sgu_b[l].T, GROUP_A, axis=1)
        b_short = jnp.tile(b_long[:ts], (CHUNK // ts, 1))
        sgu_b2 = jnp.stack([b_long, b_short])
        x = _mix_out(u, va, yb_p, yb_s, gates, x, sgu_w2, sgu_mask, sgu_b2,
                     w_branch_a[l].astype(bf16), w_branch_b[l].astype(bf16), w_out[l].astype(bf16))

        j = l // 2
        if l % 2 == 0:
            x = _ffn_dense(x, row(norm_ffn_g[l]), ffn_w1[j].astype(bf16), ffn_w3[j].astype(bf16),
                           ffn_w2[j].astype(bf16))
            delta = None
        else:
            hb, gate_t, rank_t, rank_c, cnt = _router(x, row(norm_ffn_g[l]), moe_router[j].T)
            cnt = cnt[:, :, 0].reshape(-1)
            ys = _experts(cnt, hb, rank_t, gate_t, moe_bf[0][j], moe_bf[1][j], moe_bf[2][j])
            delta = _combine(cnt, ys, rank_c)
            if l < depth - 1:
                x = x + delta
                delta = None

    yn_p, yn_s = _add_norm(x, jnp.zeros_like(x) if delta is None else delta, row(norm_final_g), n_p)
    return (yn_p.reshape(bp, tp, d), yn_s.reshape(bs, ts, d),
            wkv_p, jnp.stack(shift_p), wkv_s, jnp.stack(shift_s),
            jnp.stack(chunk_v))
```
